```python
import math
import jax, jax.numpy as jnp
from jax import lax
import numpy as np

D_MODEL = 1024
BATCH = 8
SEQ = 4096
DEPTH = 2

GRID_W = 64
CTX_LEN = 256
EPS = 1e-6

A_HEADS = 4
A_DK = 128
A_DV = 128
A_WIDTH = A_HEADS * A_DK
A_VWIDTH = A_HEADS * A_DV
A_CHUNK = 64

B_GROUPS = 4
B_GDIM = 128
B_WIDTH = B_GROUPS * B_GDIM
B_CHUNK = 128

C_HEADS = 4
C_DH = 64
C_DV = 2 * C_DH
C_QK_WIDTH = C_HEADS * 2 * C_DH
C_WIDTH = C_HEADS * C_DV
Q_BLOCK = 128
ROPE_BASE = 10000.0
ROPE_PAIRS = C_DH // 4

N_BRANCH = 3
BRANCH_WIDTH = 512
IN_SIZES = [A_WIDTH, A_WIDTH, A_WIDTH, A_VWIDTH, A_VWIDTH,
            2 * B_WIDTH,
            C_QK_WIDTH, C_QK_WIDTH, C_WIDTH,
            N_BRANCH * D_MODEL]
IN_COLS = sum(IN_SIZES)

D_FF = 2816
N_EXPERTS = 8
TOP_K = 2
D_FF_E = 3584

kernel_name = "hybrid_dit_hgrn2_gmlp_diffattn_moe"


def rmsnorm(x, g):
    xf = x.astype(jnp.float32)
    y = xf * lax.rsqrt(jnp.mean(xf * xf, axis=-1, keepdims=True) + EPS)
    return (y * g.astype(jnp.float32)).astype(x.dtype)


def adaln(cvec, w, b):
    return jax.nn.silu(cvec) @ w + b


def modulate(hn, shift, scale):
    return hn * (1.0 + scale) + shift


def split_cols(p):
    idx = np.cumsum(IN_SIZES)[:-1].tolist()
    return jnp.split(p, idx, axis=-1)


def to_heads(a, h):
    bsz, n, _ = a.shape
    return a.reshape(bsz, n, h, -1).transpose(0, 2, 1, 3)


def lower_bound(p, layer):
    cs = jnp.cumsum(jax.nn.softmax(p.astype(jnp.float32), axis=0), axis=0)
    return cs[layer] - cs[0]


def forget_gate(z, lb):
    z = z.astype(jnp.float32)
    logf = jnp.logaddexp(jnp.log(lb), jnp.log1p(-lb) + jax.nn.log_sigmoid(z))
    k = (1.0 - lb) * jax.nn.sigmoid(-z)
    return logf, k


def hgrn2_chunk_scan(q, k, v, logf, s0):
    bsz, h, n, dk = q.shape
    dv = v.shape[-1]
    nc = n // A_CHUNK

    def to_chunks(a):
        return jnp.moveaxis(a.astype(jnp.float32).reshape(bsz, h, nc, A_CHUNK, a.shape[-1]), 2, 0)

    mask = jnp.tril(jnp.ones((A_CHUNK, A_CHUNK), dtype=bool))

    def step(S, inp):
        qc, kc, vc, lf = inp
        b = jnp.cumsum(lf, axis=2)
        rel = jnp.where(mask[:, :, None], b[:, :, :, None, :] - b[:, :, None, :, :], -jnp.inf)
        decay = jnp.exp(rel)
        att = jnp.einsum('bhtk,bhsk,bhtsk->bhts', qc, kc, decay)
        o = jnp.einsum('bhts,bhsv->bhtv', att, vc) + jnp.einsum('bhtk,bhkv->bhtv', qc * jnp.exp(b), S)
        b_end = b[:, :, -1:, :]
        S_new = jnp.exp(b_end)[:, :, 0, :, None] * S + jnp.einsum('bhsk,bhsv->bhkv', kc * jnp.exp(b_end - b), vc)
        return S_new, o

    S_fin, o = lax.scan(step, s0, (to_chunks(q), to_chunks(k), to_chunks(v), to_chunks(logf)))
    o = jnp.moveaxis(o, 0, 2).reshape(bsz, h, n, dv)
    return o.astype(v.dtype), S_fin


def hgrn2_direction(q_c, z_c, v_c, q_l, z_l, v_l, lb):
    logf_c, k_c = forget_gate(z_c, lb)
    logf_l, k_l = forget_gate(z_l, lb)
    s0 = jnp.zeros((q_c.shape[0], A_HEADS, A_DK, A_DV), jnp.float32)
    o_c, s_c = hgrn2_chunk_scan(q_c, k_c, v_c, logf_c, s0)
    o_l, _ = hgrn2_chunk_scan(q_l, k_l, v_l, logf_l, s_c)
    return o_c, o_l


def hgrn2_readout(o, og, g):
    bsz, h, n, dv = o.shape
    o = rmsnorm(o.transpose(0, 2, 1, 3), g.reshape(h, dv)).reshape(bsz, n, h * dv)
    return o * jax.nn.silu(og)


def hgrn2_branch(p_l, p_c, lb_f, lb_b, gnorm, with_ctx_out):
    q_l, zf_l, zb_l, v_l, og_l = p_l
    q_c, zf_c, zb_c, v_c, og_c = p_c
    hd = lambda a: to_heads(a, A_HEADS)
    rev = lambda a: jnp.flip(a, axis=2)
    lbf = lb_f.reshape(A_HEADS, 1, A_DK)
    lbb = lb_b.reshape(A_HEADS, 1, A_DK)
    qh_l, vh_l, qh_c, vh_c = hd(q_l), hd(v_l), hd(q_c), hd(v_c)
    of_c, of_l = hgrn2_direction(qh_c, hd(zf_c), vh_c, qh_l, hd(zf_l), vh_l, lbf)
    ob_c, ob_l = hgrn2_direction(rev(qh_c), rev(hd(zb_c)), rev(vh_c), rev(qh_l), rev(hd(zb_l)), rev(vh_l), lbb)
    y_l = hgrn2_readout(of_l + rev(ob_l), og_l, gnorm)
    y_c = hgrn2_readout(of_c + rev(ob_c), og_c, gnorm) if with_ctx_out else None
    return y_l, y_c


def chunk_mlp(uv, vnorm, ws, bs):
    u, v = jnp.split(jax.nn.gelu(uv), 2, axis=-1)
    v = rmsnorm(v, vnorm)
    bsz, n, _ = v.shape
    vc = v.reshape(bsz, n // B_CHUNK, B_CHUNK, B_GROUPS, B_GDIM)
    mixed = jnp.einsum('gts,bcsgd->bctgd', ws, vc) + bs.T[None, None, :, :, None]
    return u * mixed.reshape(bsz, n, B_WIDTH)


def axial_rope_tables(rows):
    t = jnp.arange(rows * GRID_W)
    row = (t // GRID_W).astype(jnp.float32)
    col = (t % GRID_W).astype(jnp.float32)
    freqs = ROPE_BASE ** (-jnp.arange(ROPE_PAIRS, dtype=jnp.float32) / ROPE_PAIRS)
    ang = jnp.stack([row[:, None] * freqs, col[:, None] * freqs], axis=1)
    return jnp.cos(ang), jnp.sin(ang)


def apply_rope(x, cos, sin):
    xs = x.reshape(x.shape[:-1] + (2, 2, ROPE_PAIRS))
    x1, x2 = xs[..., 0, :], xs[..., 1, :]
    c, s = cos.astype(x.dtype), sin.astype(x.dtype)
    out = jnp.stack([x1 * c - x2 * s, x1 * s + x2 * c], axis=-2)
    return out.reshape(x.shape)


def qk_heads(a):
    bsz, n, _ = a.shape
    return a.reshape(bsz, n, C_HEADS, 2, C_DH).transpose(0, 2, 3, 1, 4)


def diff_softmax_attend(q, k, v, lam):
    s = jnp.einsum('bhcqd,bhckd->bhcqk', q.astype(jnp.float32), k.astype(jnp.float32)) * (C_DH ** -0.5)
    p = jax.nn.softmax(s, axis=-1)
    a = p[:, :, 0] - lam * p[:, :, 1]
    return jnp.einsum('bhqk,bhkv->bhqv', a, v.astype(jnp.float32)).astype(v.dtype)


def diff_readout(o, g, lam_init):
    bsz, h, n, dv = o.shape
    o = rmsnorm(o.transpose(0, 2, 1, 3), g.reshape(h, dv)) * (1.0 - lam_init)
    return o.reshape(bsz, n, h * dv)


def diff_attention_branch(p_l, p_c, cos, sin, lam_p, subln, layer, with_ctx_out):
    lam_init = 0.8 - 0.6 * math.exp(-0.3 * layer)
    lam = (jnp.exp(jnp.sum(lam_p[0] * lam_p[1])) - jnp.exp(jnp.sum(lam_p[2] * lam_p[3])) + lam_init).astype(jnp.float32)
    q_l, k_l, v_l = apply_rope(qk_heads(p_l[0]), cos, sin), apply_rope(qk_heads(p_l[1]), cos, sin), to_heads(p_l[2], C_HEADS)
    q_c, k_c, v_c = qk_heads(p_c[0]), qk_heads(p_c[1]), to_heads(p_c[2], C_HEADS)
    k_all = jnp.concatenate([k_l, k_c], axis=3)
    v_all = jnp.concatenate([v_l, v_c], axis=2)
    bsz, h, _, n, dh = q_l.shape
    nb = n // Q_BLOCK
    qb = jnp.moveaxis(q_l.reshape(bsz, h, 2, nb, Q_BLOCK, dh), 3, 0)
    o = lax.map(lambda qblk: diff_softmax_attend(qblk, k_all, v_all, lam), qb)
    o = jnp.moveaxis(o, 0, 2).reshape(bsz, h, n, C_DV)
    y_l = diff_readout(o, subln, lam_init)
    y_c = diff_readout(diff_softmax_attend(q_c, k_c, v_c, lam), subln, lam_init) if with_ctx_out else None
    return y_l, y_c


def merge_branches(ys, gates, w_branch, w_out):
    merged = sum(jax.nn.sigmoid(gates[..., i * D_MODEL:(i + 1) * D_MODEL]) * (ys[i] @ w_branch[i]) for i in range(N_BRANCH))
    return merged @ w_out


def token_mixer(h_l, h_c, layer, cos, sin, w_in_l, hgrn_lb, gnorm_a, vnorm_b, ws_b, bs_b,
                lam_p, subln_c, w_branch_l, w_out_l, with_ctx_out):
    pl = split_cols(h_l @ w_in_l)
    pc = split_cols(h_c @ w_in_l)
    lb_f = lower_bound(hgrn_lb[0], layer)
    lb_b = lower_bound(hgrn_lb[1], layer)
    ya_l, ya_c = hgrn2_branch(pl[0:5], pc[0:5], lb_f, lb_b, gnorm_a, with_ctx_out)
    yb_l = chunk_mlp(pl[5], vnorm_b, ws_b, bs_b)
    yc_l, yc_c = diff_attention_branch(pl[6:9], pc[6:9], cos, sin, lam_p, subln_c, layer, with_ctx_out)
    m_l = merge_branches([ya_l, yb_l, yc_l], pl[9], w_branch_l, w_out_l)
    m_c = None
    if with_ctx_out:
        yb_c = chunk_mlp(pc[5], vnorm_b, ws_b, bs_b)
        m_c = merge_branches([ya_c, yb_c, yc_c], pc[9], w_branch_l, w_out_l)
    return m_l, m_c


def swiglu(x, wg, wu, wd):
    return (jax.nn.silu(x @ wg) * (x @ wu)) @ wd


def moe_ffn(x, w_router, w_eg, w_eu, w_ed):
    logits = (x @ w_router).astype(jnp.float32)
    top_v, top_i = lax.top_k(logits, TOP_K)
    wts = jax.nn.softmax(top_v, axis=-1)
    comb = jnp.sum(jax.nn.one_hot(top_i, N_EXPERTS, dtype=jnp.float32) * wts[..., None], axis=-2)
    out = jnp.zeros_like(x)
    for e in range(N_EXPERTS):
        out = out + comb[..., e:e + 1].astype(x.dtype) * swiglu(x, w_eg[e], w_eu[e], w_ed[e])
    return out


def channel_mixer(h, layer, ffn_wg, ffn_wu, ffn_wd, moe_router, moe_wg, moe_wu, moe_wd):
    j = layer // 2
    if layer % 2 == 0:
        return swiglu(h, ffn_wg[j], ffn_wu[j], ffn_wd[j])
    return moe_ffn(h, moe_router[j], moe_wg[j], moe_wu[j], moe_wd[j])


def setup_inputs(seed: int = 0) -> dict:
    key = jax.random.key(seed)
    ks = jax.random.split(key, 32)
    nrm = lambda k, shape, scale: jax.random.normal(k, shape, jnp.float32) * scale
    n_dense = (DEPTH + 1) // 2
    n_moe = DEPTH // 2
    return {
        "x": nrm(ks[0], (BATCH, SEQ, D_MODEL), 1.0),
        "c": nrm(ks[1], (BATCH, D_MODEL), 1.0),
        "ctx": nrm(ks[2], (BATCH, CTX_LEN, D_MODEL), 1.0),
        "c_ctx": nrm(ks[3], (D_MODEL,), 1.0),
        "w_ada": nrm(ks[4], (DEPTH, D_MODEL, 6 * D_MODEL), 0.5 * D_MODEL ** -0.5),
        "b_ada": nrm(ks[5], (DEPTH, 6 * D_MODEL), 0.02),
        "g_norm1": 1.0 + nrm(ks[6], (DEPTH, D_MODEL), 0.02),
        "g_norm2": 1.0 + nrm(ks[7], (DEPTH, D_MODEL), 0.02),
        "w_in": nrm(ks[8], (DEPTH, D_MODEL, IN_COLS), D_MODEL ** -0.5),
        "hgrn_lb": nrm(ks[9], (2, DEPTH, A_WIDTH), 0.5),
        "hgrn_gnorm": 1.0 + nrm(ks[10], (DEPTH, A_VWIDTH), 0.02),
        "mlp_vnorm": 1.0 + nrm(ks[11], (DEPTH, B_WIDTH), 0.02),
        "mlp_ws": nrm(ks[12], (DEPTH, B_GROUPS, B_CHUNK, B_CHUNK), B_CHUNK ** -0.5),
        "mlp_bs": 1.0 + nrm(ks[13], (DEPTH, B_GROUPS, B_CHUNK), 0.02),
        "diff_lambda": nrm(ks[14], (DEPTH, 4, C_DH), 0.1),
        "diff_subln": 1.0 + nrm(ks[15], (DEPTH, C_WIDTH), 0.02),
        "w_branch": nrm(ks[16], (DEPTH, N_BRANCH, BRANCH_WIDTH, D_MODEL), BRANCH_WIDTH ** -0.5),
        "w_out": nrm(ks[17], (DEPTH, D_MODEL, D_MODEL), D_MODEL ** -0.5),
        "ffn_wg": nrm(ks[18], (n_dense, D_MODEL, D_FF), D_MODEL ** -0.5),
        "ffn_wu": nrm(ks[19], (n_dense, D_MODEL, D_FF), D_MODEL ** -0.5),
        "ffn_wd": nrm(ks[20], (n_dense, D_FF, D_MODEL), D_FF ** -0.5),
        "moe_router": nrm(ks[21], (n_moe, D_MODEL, N_EXPERTS), D_MODEL ** -0.5),
        "moe_wg": nrm(ks[22], (n_moe, N_EXPERTS, D_MODEL, D_FF_E), D_MODEL ** -0.5),
        "moe_wu": nrm(ks[23], (n_moe, N_EXPERTS, D_MODEL, D_FF_E), D_MODEL ** -0.5),
        "moe_wd": nrm(ks[24], (n_moe, N_EXPERTS, D_FF_E, D_MODEL), D_FF_E ** -0.5),
        "g_final": 1.0 + nrm(ks[25], (D_MODEL,), 0.02),
    }


def reference(x, c, ctx, c_ctx, w_ada, b_ada, g_norm1, g_norm2, w_in, hgrn_lb, hgrn_gnorm,
              mlp_vnorm, mlp_ws, mlp_bs, diff_lambda, diff_subln, w_branch, w_out,
              ffn_wg, ffn_wu, ffn_wd, moe_router, moe_wg, moe_wu, moe_wd, g_final):
    n_lat = x.shape[1]
    rows = n_lat // GRID_W
    cos, sin = axial_rope_tables(rows)
    for layer in range(DEPTH):
        last = layer == DEPTH - 1
        mod_l = jnp.split(adaln(c, w_ada[layer], b_ada[layer])[:, None, :], 6, axis=-1)
        mod_c = jnp.split(adaln(c_ctx, w_ada[layer], b_ada[layer]), 6, axis=-1)
        h_l = modulate(rmsnorm(x, g_norm1[layer]), mod_l[0], mod_l[1])
        h_c = modulate(rmsnorm(ctx, g_norm1[layer]), mod_c[0], mod_c[1])
        m_l, m_c = token_mixer(h_l, h_c, layer, cos, sin, w_in[layer], hgrn_lb, hgrn_gnorm[layer],
                               mlp_vnorm[layer], mlp_ws[layer], mlp_bs[layer], diff_lambda[layer],
                               diff_subln[layer], w_branch[layer], w_out[layer], not last)
        x = x + mod_l[2] * m_l
        h_l = modulate(rmsnorm(x, g_norm2[layer]), mod_l[3], mod_l[4])
        x = x + mod_l[5] * channel_mixer(h_l, layer, ffn_wg, ffn_wu, ffn_wd, moe_router, moe_wg, moe_wu, moe_wd)
        if not last:
            ctx = ctx + mod_c[2] * m_c
            h_c = modulate(rmsnorm(ctx, g_norm2[layer]), mod_c[3], mod_c[4])
            ctx = ctx + mod_c[5] * channel_mixer(h_c, layer, ffn_wg, ffn_wu, ffn_wd, moe_router, moe_wg, moe_wu, moe_wd)
    return rmsnorm(x, g_final)
```

```python
import functools
import math

import jax
import jax.numpy as jnp
from jax import lax
from jax.experimental import pallas as pl
from jax.experimental.pallas import tpu as pltpu

F32 = jnp.float32
BF16 = jnp.bfloat16

EPS = 1e-6
GRID_W = 64
ROPE_BASE = 10000.0

A_HEADS = 4
A_CHUNK = 64
A_SUB = 16
B_GROUPS = 4
B_CHUNK = 128
C_HEADS = 4
C_DH = 64
HEAD_W = 128
BRANCH_W = 512
N_BRANCH = 3
N_EXPERTS = 8
TOP_K = 2

ROW_TILE = 256
MOD_ROWS = 16
ATT_TQ = 256
ATT_TK = 512
FF_CHUNK = 512
EXP_CLAMP = 80.0
V7X_VMEM_LIMIT = 56 * 1024 * 1024


def _cparams(sem, vmem=None):
    return pltpu.CompilerParams(dimension_semantics=sem, vmem_limit_bytes=vmem)


def _resident(shape, index_map):
    return pl.BlockSpec(shape, index_map, pipeline_mode=pl.Buffered(1))


def _rms(xf):
    return xf * lax.rsqrt(jnp.mean(xf * xf, axis=-1, keepdims=True) + EPS)


def _sigmoid(x):
    return 1.0 / (1.0 + jnp.exp(-x))


def _dot(a, b):
    return jnp.dot(a, b, preferred_element_type=F32)


def _dot_nt(a, b):
    return lax.dot_general(a, b, (((1,), (1,)), ((), ())), preferred_element_type=F32)


def _dot_tn(a, b):
    return lax.dot_general(a, b, (((0,), (0,)), ((), ())), preferred_element_type=F32)


def _adaln_kernel(c_ref, w_ref, b_ref, o_ref):
    c = c_ref[...]
    a = c * _sigmoid(c)
    o_ref[0] = jnp.dot(a, w_ref[0], preferred_element_type=F32, precision=lax.Precision.HIGHEST) + b_ref[0]


def _adaln(cvec, w_ada, b_ada):
    depth, d, n = w_ada.shape
    tn = 1536
    return pl.pallas_call(
        _adaln_kernel,
        grid=(depth, n // tn),
        in_specs=[pl.BlockSpec((MOD_ROWS, d), lambda l, j: (0, 0)),
                  pl.BlockSpec((1, d, tn), lambda l, j: (l, 0, j)),
                  pl.BlockSpec((1, 1, tn), lambda l, j: (l, 0, j))],
        out_specs=pl.BlockSpec((1, MOD_ROWS, tn), lambda l, j: (l, 0, j)),
        out_shape=jax.ShapeDtypeStruct((depth, MOD_ROWS, n), F32),
        compiler_params=_cparams(("arbitrary", "arbitrary")),
        name="adaln",
    )(cvec, w_ada, b_ada.reshape(depth, 1, n))


_IN_COLS = (("qa", 512, BF16), ("zf", 512, F32), ("zb", 512, F32), ("ia", 512, BF16), ("og", 512, BF16),
            ("uv", 1024, BF16), ("qc", 512, BF16), ("kc", 512, BF16), ("vc", 512, BF16), ("gt", 3072, BF16))


def _inproj_kernel(x_ref, mod_ref, g_ref, w_ref, cos_ref, sin_ref, *out_refs, d):
    x = x_ref[0]
    mod = mod_ref[0]
    h = _rms(x) * g_ref[...]
    h = (h * (1.0 + mod[:, d:2 * d]) + mod[:, 0:d]).astype(BF16)

    tm = x.shape[0]
    lane = lax.broadcasted_iota(jnp.int32, (tm, 512), 1)
    low_half = (lane % 32) < 16

    def rope(p):
        partner = jnp.where(low_half, pltpu.roll(p, 512 - 16, 1), pltpu.roll(p, 16, 1))
        return p * cos_ref[...] + partner * sin_ref[...]

    lo = 0
    for (name, width, dt), o_ref in zip(_IN_COLS, out_refs):
        for c0 in range(0, width, 512):
            p = _dot(h, w_ref[:, lo + c0:lo + c0 + 512])
            if name in ("qc", "kc"):
                p = rope(p)
            o_ref[0, :, c0:c0 + 512] = p.astype(dt)
        lo += width


def _inproj(xc, mod3, g, w_bf, cos_t, sin_t, n_lat_blocks):
    bsz, t, d = xc.shape
    tm = ROW_TILE
    ncols = w_bf.shape[1]
    ctx_row = bsz

    def mod_map(b, j):
        return (jnp.where(j < n_lat_blocks, b, ctx_row), 0, 0)

    out_shape = [jax.ShapeDtypeStruct((bsz, t, w), dt) for _, w, dt in _IN_COLS]
    out_specs = [pl.BlockSpec((1, tm, w), lambda b, j: (b, j, 0)) for _, w, _ in _IN_COLS]
    return pl.pallas_call(
        functools.partial(_inproj_kernel, d=d),
        grid=(bsz, t // tm),
        in_specs=[pl.BlockSpec((1, tm, d), lambda b, j: (b, j, 0)),
                  pl.BlockSpec((1, 1, mod3.shape[2]), mod_map),
                  pl.BlockSpec((1, d), lambda b, j: (0, 0)),
                  _resident((d, ncols), lambda b, j: (0, 0)),
                  pl.BlockSpec((tm, 512), lambda b, j: (j, 0)),
                  pl.BlockSpec((tm, 512), lambda b, j: (j, 0))],
        out_specs=out_specs,
        out_shape=out_shape,
        compiler_params=_cparams(("arbitrary", "arbitrary"), V7X_VMEM_LIMIT),
        name="inproj",
    )(xc, mod3, g.reshape(1, d), w_bf, cos_t, sin_t)


def _hgrn_direction(q_ref, z_ref, v_ref, lb_ref, o_ref, st_ref, reverse):
    c = A_CHUNK
    row = lax.broadcasted_iota(jnp.int32, (c, c), 0)
    col = lax.broadcasted_iota(jnp.int32, (c, c), 1)
    allowed = (col >= row) if reverse else (col <= row)

    z = z_ref[0]
    lb = lb_ref[...]
    sp = jnp.maximum(-z, 0.0) + jnp.log1p(jnp.exp(-jnp.abs(z)))
    la = jnp.log(lb)
    lc = jnp.log1p(-lb) - sp
    logf = jnp.maximum(la, lc) + jnp.log1p(jnp.exp(-jnp.abs(la - lc)))
    kin = (1.0 - lb) * jnp.exp(-z - sp)
    b = jnp.dot(allowed.astype(F32), logf, preferred_element_type=F32, precision=lax.Precision.HIGHEST)
    tot_row = 0 if reverse else c - 1
    tot = b[tot_row:tot_row + 1, :]

    q = q_ref[0].astype(F32)
    v = v_ref[0]
    zero_blk = jnp.zeros((A_SUB, HEAD_W), F32)
    for h in range(A_HEADS):
        hs = slice(h * HEAD_W, (h + 1) * HEAD_W)
        bh, qh, kh, vh = b[:, hs], q[:, hs], kin[:, hs], v[:, hs]
        q_state = qh * jnp.exp(bh)
        k_state = kh * jnp.exp(tot[:, hs] - bh)
        q_rows, k_cols = [], []
        for i in range(c // A_SUB):
            r0 = i * A_SUB
            ref = r0 + A_SUB - 1 if reverse else r0
            r = bh[ref:ref + 1, :]
            qt = qh[r0:r0 + A_SUB] * jnp.exp(bh[r0:r0 + A_SUB] - r)
            q_rows.append(jnp.concatenate([qt if j == i else zero_blk for j in range(c // A_SUB)], axis=1))
            k_cols.append(kh * jnp.exp(jnp.minimum(r - bh, EXP_CLAMP)))
        q_wide = jnp.concatenate(q_rows, axis=0).astype(BF16)
        k_wide = jnp.concatenate(k_cols, axis=1).astype(BF16)
        att = jnp.where(allowed, _dot_nt(q_wide, k_wide), 0.0)
        st = st_ref[h]
        o = _dot(att.astype(BF16), vh) + _dot_nt(q_state.astype(BF16), st.astype(BF16))
        o_ref[0, :, hs] = o.astype(o_ref.dtype)
        st_ref[h] = st * jnp.exp(tot[:, hs]) + _dot_tn(vh, k_state.astype(BF16))


def _hgrn_kernel(qf_ref, zf_ref, vf_ref, qb_ref, zb_ref, vb_ref, lbf_ref, lbb_ref, of_ref, ob_ref, sf_ref, sb_ref):
    @pl.when(pl.program_id(1) == 0)
    def _():
        sf_ref[...] = jnp.zeros_like(sf_ref)
        sb_ref[...] = jnp.zeros_like(sb_ref)

    _hgrn_direction(qf_ref, zf_ref, vf_ref, lbf_ref, of_ref, sf_ref, reverse=False)
    _hgrn_direction(qb_ref, zb_ref, vb_ref, lbb_ref, ob_ref, sb_ref, reverse=True)


def _hgrn(qa, zf, zb, ia, lb_f, lb_b, n_lat):
    bsz, t, w = qa.shape
    c = A_CHUNK
    n_chunks = t // c
    lat_chunks = n_lat // c

    def fwd(b, s):
        return (b, (s + lat_chunks) % n_chunks, 0)

    def bwd(b, s):
        return (b, n_chunks - 1 - s, 0)

    blk = (1, c, w)
    vec = pl.BlockSpec((1, w), lambda b, s: (0, 0))
    return pl.pallas_call(
        _hgrn_kernel,
        grid=(bsz, n_chunks),
        in_specs=[pl.BlockSpec(blk, fwd), pl.BlockSpec(blk, fwd), pl.BlockSpec(blk, fwd),
                  pl.BlockSpec(blk, bwd), pl.BlockSpec(blk, bwd), pl.BlockSpec(blk, bwd), vec, vec],
        out_specs=[pl.BlockSpec(blk, fwd), pl.BlockSpec(blk, bwd)],
        out_shape=[jax.ShapeDtypeStruct((bsz, t, w), BF16)] * 2,
        scratch_shapes=[pltpu.VMEM((A_HEADS, HEAD_W, HEAD_W), F32)] * 2,
        compiler_params=_cparams(("arbitrary", "arbitrary")),
        name="hgrn2",
    )(qa, zf, ia, qa, zb, ia, lb_f.reshape(1, w), lb_b.reshape(1, w))


def _attn_kernel(q_ref, k_ref, v_ref, lam_ref, sub_ref, o_ref, m_ref, l_ref, acc_ref, *, n_lat, n_ctx, lam_init,
                 with_ctx_out):
    tq = q_ref.shape[1]
    lane = lax.broadcasted_iota(jnp.int32, (tq, HEAD_W), 1)
    qf = q_ref[0].astype(F32) * (C_DH ** -0.5)
    q2 = jnp.concatenate([jnp.where(lane < C_DH, qf, 0.0), jnp.where(lane >= C_DH, qf, 0.0)], axis=0).astype(BF16)

    m_ref[...] = jnp.full_like(m_ref, -1e30)
    l_ref[...] = jnp.zeros_like(l_ref)
    acc_ref[...] = jnp.zeros_like(acc_ref)

    def block(kb, vb):
        s = _dot_nt(q2, kb)
        m_old = m_ref[...]
        m_new = jnp.maximum(m_old, jnp.max(s, axis=-1, keepdims=True))
        alpha = jnp.exp(m_old - m_new)
        p = jnp.exp(s - m_new)
        l_ref[...] = alpha * l_ref[...] + jnp.sum(p, axis=-1, keepdims=True)
        acc_ref[...] = alpha * acc_ref[...] + _dot(p.astype(BF16), vb)
        m_ref[...] = m_new

    def latent_blocks():
        def body(i, carry):
            start = pl.multiple_of(i * ATT_TK, ATT_TK)
            block(k_ref[0, pl.ds(start, ATT_TK), :], v_ref[0, pl.ds(start, ATT_TK), :])
            return carry
        lax.fori_loop(0, n_lat // ATT_TK, body, 0)

    def context_block():
        block(k_ref[0, n_lat:n_lat + n_ctx, :], v_ref[0, n_lat:n_lat + n_ctx, :])

    if with_ctx_out:
        pl.when(pl.program_id(2) * tq < n_lat)(latent_blocks)
    else:
        latent_blocks()
    context_block()

    o12 = acc_ref[...] / l_ref[...]
    o = o12[:tq] - lam_ref[...] * o12[tq:]
    o_ref[0] = (_rms(o) * sub_ref[...] * (1.0 - lam_init)).astype(o_ref.dtype)


def _attention(qc, kc, vc, lam, subln, n_lat, with_ctx_out, lam_init):
    bsz, t, w = qc.shape
    n_ctx = t - n_lat
    tq = ATT_TQ
    n_q = (t if with_ctx_out else n_lat) // tq
    return pl.pallas_call(
        functools.partial(_attn_kernel, n_lat=n_lat, n_ctx=n_ctx, lam_init=lam_init, with_ctx_out=with_ctx_out),
        grid=(bsz, C_HEADS, n_q),
        in_specs=[pl.BlockSpec((1, tq, HEAD_W), lambda b, h, i: (b, i, h)),
                  pl.BlockSpec((1, t, HEAD_W), lambda b, h, i: (b, 0, h)),
                  pl.BlockSpec((1, t, HEAD_W), lambda b, h, i: (b, 0, h)),
                  pl.BlockSpec((1, HEAD_W), lambda b, h, i: (0, 0)),
                  pl.BlockSpec((1, HEAD_W), lambda b, h, i: (0, h))],
        out_specs=pl.BlockSpec((1, tq, HEAD_W), lambda b, h, i: (b, i, h)),
        out_shape=jax.ShapeDtypeStruct((bsz, n_q * tq, w), BF16),
        scratch_shapes=[pltpu.VMEM((2 * tq, 1), F32), pltpu.VMEM((2 * tq, 1), F32),
                        pltpu.VMEM((2 * tq, HEAD_W), F32)],
        compiler_params=_cparams(("arbitrary", "arbitrary", "arbitrary")),
        name="diff_attention",
    )(qc, kc, vc, jnp.full((1, HEAD_W), lam, F32), subln.reshape(1, w))


def _gelu_tanh(x):
    return 0.5 * x * (1.0 + jnp.tanh(math.sqrt(2.0 / math.pi) * (x + 0.044715 * (x * x * x))))


def _merge_kernel(x_ref, mod_ref, of_ref, ob_ref, og_ref, gn_ref, uv_ref, vn_ref, ws_ref, bs_ref, yc_ref, gt_ref,
                  wb_ref, wo_ref, g2_ref, wr_ref, x_out_ref, h_out_ref, *maybe_logits_ref, d):
    tm = x_ref.shape[1]
    o = of_ref[0].astype(F32) + ob_ref[0].astype(F32)
    o = jnp.concatenate([_rms(o[:, h * HEAD_W:(h + 1) * HEAD_W]) for h in range(A_HEADS)], axis=1)
    og = og_ref[0].astype(F32)
    ya = o * gn_ref[...] * (og * _sigmoid(og))

    uv = _gelu_tanh(uv_ref[0].astype(F32))
    u = uv[:, :BRANCH_W]
    vv = (_rms(uv[:, BRANCH_W:]) * vn_ref[...]).astype(BF16)
    mixed_rows = []
    for c0 in range(0, tm, B_CHUNK):
        mixed_rows.append(jnp.concatenate(
            [_dot(ws_ref[g], vv[c0:c0 + B_CHUNK, g * HEAD_W:(g + 1) * HEAD_W]) for g in range(B_GROUPS)], axis=1)
            + bs_ref[...])
    yb = u * jnp.concatenate(mixed_rows, axis=0)

    ys = (ya, yb, yc_ref[0])
    merged = None
    for i in range(N_BRANCH):
        gate = _sigmoid(gt_ref[0, :, i * d:(i + 1) * d].astype(F32))
        term = gate * _dot(ys[i].astype(BF16), wb_ref[i])
        merged = term if merged is None else merged + term
    mix = _dot(merged.astype(BF16), wo_ref[...])

    mod = mod_ref[0]
    x_new = x_ref[0] + mod[:, 2 * d:3 * d] * mix
    x_out_ref[0] = x_new
    h = _rms(x_new) * g2_ref[...]
    h = h * (1.0 + mod[:, 4 * d:5 * d]) + mod[:, 3 * d:4 * d]
    h_out_ref[0] = h.astype(BF16)
    if maybe_logits_ref:
        maybe_logits_ref[0][0] = jnp.dot(h, wr_ref[...], preferred_element_type=F32, precision=lax.Precision.HIGHEST)


def _merge(xc, mod3, proj, of, ob, yc, gnorm, vnorm, ws_bf, bs_full, wb_bf, wo_bf, g2, w_router_pad,
           n_lat_blocks, n_blocks, with_logits):
    bsz, _, d = xc.shape
    tm = ROW_TILE
    ctx_row = bsz

    def rows(w):
        return pl.BlockSpec((1, tm, w), lambda b, j: (b, j, 0))

    def const2(shape):
        return pl.BlockSpec(shape, lambda b, j: (0, 0))

    def mod_map(b, j):
        return (jnp.where(j < n_lat_blocks, b, ctx_row), 0, 0)

    n_rows = n_blocks * tm
    out_shape = [jax.ShapeDtypeStruct((bsz, n_rows, d), F32), jax.ShapeDtypeStruct((bsz, n_rows, d), BF16)]
    out_specs = [rows(d), rows(d)]
    if with_logits:
        out_shape.append(jax.ShapeDtypeStruct((bsz, n_rows, HEAD_W), F32))
        out_specs.append(rows(HEAD_W))
    return pl.pallas_call(
        functools.partial(_merge_kernel, d=d),
        grid=(bsz, n_blocks),
        in_specs=[rows(d), pl.BlockSpec((1, 1, mod3.shape[2]), mod_map),
                  rows(BRANCH_W), rows(BRANCH_W), rows(BRANCH_W), const2((1, BRANCH_W)),
                  rows(2 * BRANCH_W), const2((1, BRANCH_W)),
                  _resident((B_GROUPS, B_CHUNK, B_CHUNK), lambda b, j: (0, 0, 0)), const2((B_CHUNK, BRANCH_W)),
                  rows(BRANCH_W), rows(N_BRANCH * d),
                  _resident((N_BRANCH, BRANCH_W, d), lambda b, j: (0, 0, 0)), _resident((d, d), lambda b, j: (0, 0)),
                  const2((1, d)), _resident((d, HEAD_W), lambda b, j: (0, 0))],
        out_specs=out_specs,
        out_shape=out_shape,
        compiler_params=_cparams(("arbitrary", "arbitrary"), V7X_VMEM_LIMIT),
        name="merge",
    )(xc, mod3, of, ob, proj["og"], gnorm.reshape(1, BRANCH_W), proj["uv"], vnorm.reshape(1, BRANCH_W), ws_bf,
      bs_full, yc, proj["gt"], wb_bf, wo_bf, g2.reshape(1, d), w_router_pad)


def _ffn_kernel(x_ref, h_ref, mod_ref, wg_ref, wu_ref, wd_ref, o_ref, *, d):
    h = h_ref[0]
    g = _dot(h, wg_ref[...])
    a = (g * _sigmoid(g) * _dot(h, wu_ref[...])).astype(BF16)
    y = _dot(a, wd_ref[...])
    o_ref[0] = x_ref[0] + mod_ref[0][:, 5 * d:6 * d] * y


def _ffn(xc, h2, mod3, wg_bf, wu_bf, wd_bf, n_lat_blocks):
    bsz, t, d = xc.shape
    tm = ROW_TILE
    dff = wg_bf.shape[1]
    ctx_row = bsz

    def mod_map(b, j):
        return (jnp.where(j < n_lat_blocks, b, ctx_row), 0, 0)

    rows = pl.BlockSpec((1, tm, d), lambda b, j: (b, j, 0))
    return pl.pallas_call(
        functools.partial(_ffn_kernel, d=d),
        grid=(bsz, t // tm),
        in_specs=[rows, rows, pl.BlockSpec((1, 1, mod3.shape[2]), mod_map),
                  _resident((d, dff), lambda b, j: (0, 0)), _resident((d, dff), lambda b, j: (0, 0)),
                  _resident((dff, d), lambda b, j: (0, 0))],
        out_specs=rows,
        out_shape=jax.ShapeDtypeStruct((bsz, t, d), F32),
        compiler_params=_cparams(("arbitrary", "arbitrary"), V7X_VMEM_LIMIT),
        name="dense_ffn",
    )(xc, h2, mod3, wg_bf, wu_bf, wd_bf)


def _moe_kernel(tile_expert_ref, n_tiles_ref, x_ref, wg_ref, wu_ref, wd_ref, o_ref):
    del tile_expert_ref

    @pl.when(pl.program_id(0) < n_tiles_ref[0])
    def _():
        x = x_ref[...]
        dff = wg_ref.shape[2]
        acc = None
        for c0 in range(0, dff, FF_CHUNK):
            g = _dot(x, wg_ref[0, :, c0:c0 + FF_CHUNK])
            a = (g * _sigmoid(g) * _dot(x, wu_ref[0, :, c0:c0 + FF_CHUNK])).astype(BF16)
            y = _dot(a, wd_ref[0, c0:c0 + FF_CHUNK, :])
            acc = y if acc is None else acc + y
        o_ref[...] = acc

    @pl.when(pl.program_id(0) >= n_tiles_ref[0])
    def _():
        o_ref[...] = jnp.zeros_like(o_ref)


def _moe_experts(x_sorted, tile_expert, n_tiles, wg_bf, wu_bf, wd_bf):
    p, d = x_sorted.shape
    tm = ROW_TILE
    dff = wg_bf.shape[2]
    grid_spec = pltpu.PrefetchScalarGridSpec(
        num_scalar_prefetch=2,
        grid=(p // tm,),
        in_specs=[pl.BlockSpec((tm, d), lambda t, te, nt: (t, 0)),
                  pl.BlockSpec((1, d, dff), lambda t, te, nt: (te[t], 0, 0)),
                  pl.BlockSpec((1, d, dff), lambda t, te, nt: (te[t], 0, 0)),
                  pl.BlockSpec((1, dff, d), lambda t, te, nt: (te[t], 0, 0))],
        out_specs=pl.BlockSpec((tm, d), lambda t, te, nt: (t, 0)),
    )
    return pl.pallas_call(
        _moe_kernel,
        grid_spec=grid_spec,
        out_shape=jax.ShapeDtypeStruct((p, d), F32),
        compiler_params=_cparams(("arbitrary",), V7X_VMEM_LIMIT),
        name="moe_experts",
    )(tile_expert, n_tiles, x_sorted, wg_bf, wu_bf, wd_bf)


def _route(logits, tm):
    n = logits.shape[0]
    top_v, top_i = lax.top_k(logits, TOP_K)
    wts = jax.nn.softmax(top_v, axis=-1)
    flat_e = top_i.reshape(-1)
    onehot = (flat_e[:, None] == jnp.arange(N_EXPERTS)[None, :]).astype(jnp.int32)
    rank = jnp.take_along_axis(jnp.cumsum(onehot, axis=0) - onehot, flat_e[:, None], axis=1)[:, 0]
    counts = jnp.sum(onehot, axis=0)
    padded = ((counts + tm - 1) // tm) * tm
    ends = jnp.cumsum(padded)
    dest = (ends - padded)[flat_e] + rank
    p = TOP_K * n + N_EXPERTS * tm
    src_token = jnp.zeros((p,), jnp.int32).at[dest].set(jnp.arange(TOP_K * n, dtype=jnp.int32) // TOP_K)
    tile_start = jnp.arange(p // tm, dtype=jnp.int32) * tm
    tile_expert = jnp.minimum(jnp.sum((tile_start[:, None] >= ends[None, :]).astype(jnp.int32), axis=1),
                              N_EXPERTS - 1).astype(jnp.int32)
    n_tiles = (ends[-1] // tm).astype(jnp.int32).reshape(1)
    return wts, dest.reshape(n, TOP_K), src_token, tile_expert, n_tiles


def _final_kernel(x_ref, y0_ref, y1_ref, w_ref, mod_ref, g_ref, o_ref, *, d):
    w = w_ref[0]
    y = w[:, 0:1] * y0_ref[0] + w[:, 1:2] * y1_ref[0]
    x = x_ref[0] + mod_ref[0][:, 5 * d:6 * d] * y
    o_ref[0] = _rms(x) * g_ref[...]


def _final(x_lat, y0, y1, wts, mod3, g_final):
    bsz, s, d = x_lat.shape
    tm = ROW_TILE
    rows = pl.BlockSpec((1, tm, d), lambda b, j: (b, j, 0))
    return pl.pallas_call(
        functools.partial(_final_kernel, d=d),
        grid=(bsz, s // tm),
        in_specs=[rows, rows, rows, pl.BlockSpec((1, tm, TOP_K), lambda b, j: (b, j, 0)),
                  pl.BlockSpec((1, 1, mod3.shape[2]), lambda b, j: (b, 0, 0)),
                  pl.BlockSpec((1, d), lambda b, j: (0, 0))],
        out_specs=rows,
        out_shape=jax.ShapeDtypeStruct((bsz, s, d), F32),
        compiler_params=_cparams(("arbitrary", "arbitrary")),
        name="moe_combine_final_norm",
    )(x_lat, y0, y1, wts, mod3, g_final.reshape(1, d))


def _rope_tables(n_lat, n_ctx):
    pairs = C_DH // 4
    tpos = jnp.arange(n_lat)
    pos = jnp.stack([(tpos // GRID_W).astype(F32), (tpos % GRID_W).astype(F32)], axis=1)
    freqs = ROPE_BASE ** (-jnp.arange(pairs, dtype=F32) / pairs)
    ang = pos[:, :, None] * freqs[None, None, :]
    cos = jnp.repeat(jnp.cos(ang)[:, :, None, :], 2, axis=2).reshape(n_lat, C_DH)
    sin = jnp.sin(ang)[:, :, None, :] * jnp.array([-1.0, 1.0], F32)[None, None, :, None]
    sin = sin.reshape(n_lat, C_DH)
    cos = jnp.concatenate([cos, jnp.ones((n_ctx, C_DH), F32)], axis=0)
    sin = jnp.concatenate([sin, jnp.zeros((n_ctx, C_DH), F32)], axis=0)
    reps = 512 // C_DH
    return jnp.tile(cos, (1, reps)), jnp.tile(sin, (1, reps))


def _lower_bound(p, layer):
    cs = jnp.cumsum(jax.nn.softmax(p.astype(F32), axis=0), axis=0)
    return cs[layer] - cs[0]


def kernel(x, c, ctx, c_ctx, w_ada, b_ada, g_norm1, g_norm2, w_in, hgrn_lb, hgrn_gnorm, mlp_vnorm, mlp_ws, mlp_bs,
           diff_lambda, diff_subln, w_branch, w_out, ffn_wg, ffn_wu, ffn_wd, moe_router, moe_wg, moe_wu, moe_wd,
           g_final):
    bsz, n_lat, d = x.shape
    n_ctx = ctx.shape[1]
    depth = w_ada.shape[0]
    t = n_lat + n_ctx
    tm = ROW_TILE
    assert depth == 2 and bsz < MOD_ROWS and n_lat % ATT_TK == 0 and n_ctx % tm == 0 and n_lat % GRID_W == 0
    n_lat_blocks = n_lat // tm

    cvec = jnp.zeros((MOD_ROWS, d), F32).at[:bsz].set(c).at[bsz].set(c_ctx)
    mods = _adaln(cvec, w_ada, b_ada)
    cos_t, sin_t = _rope_tables(n_lat, n_ctx)
    xc = jnp.concatenate([x, ctx], axis=1)

    out = None
    for layer in range(depth):
        last = layer == depth - 1
        mod3 = mods[layer].reshape(MOD_ROWS, 1, 6 * d)
        proj = dict(zip([n for n, _, _ in _IN_COLS],
                        _inproj(xc, mod3, g_norm1[layer], w_in[layer].astype(BF16), cos_t, sin_t, n_lat_blocks)))
        lb_f = _lower_bound(hgrn_lb[0], layer)
        lb_b = _lower_bound(hgrn_lb[1], layer)
        of, ob = _hgrn(proj["qa"], proj["zf"], proj["zb"], proj["ia"], lb_f, lb_b, n_lat)

        lam_init = 0.8 - 0.6 * math.exp(-0.3 * layer)
        lam_p = diff_lambda[layer]
        lam = jnp.exp(jnp.sum(lam_p[0] * lam_p[1])) - jnp.exp(jnp.sum(lam_p[2] * lam_p[3])) + lam_init
        yc = _attention(proj["qc"], proj["kc"], proj["vc"], lam.astype(F32), diff_subln[layer], n_lat,
                        not last, lam_init)

        bs_full = jnp.repeat(mlp_bs[layer].T, HEAD_W, axis=1)
        j = layer // 2
        is_moe = layer % 2 == 1
        w_router_pad = jnp.zeros((d, HEAD_W), F32)
        if is_moe:
            w_router_pad = w_router_pad.at[:, :N_EXPERTS].set(moe_router[j])
        n_blocks = n_lat_blocks if last else t // tm
        merged = _merge(xc, mod3, proj, of, ob, yc, hgrn_gnorm[layer], mlp_vnorm[layer], mlp_ws[layer].astype(BF16),
                        bs_full, w_branch[layer].astype(BF16), w_out[layer].astype(BF16), g_norm2[layer],
                        w_router_pad, n_lat_blocks, n_blocks, is_moe)
        x_new, h2 = merged[0], merged[1]

        if not is_moe:
            assert not last
            xc = _ffn(x_new, h2, mod3, ffn_wg[j].astype(BF16), ffn_wu[j].astype(BF16), ffn_wd[j].astype(BF16),
                      n_lat_blocks)
        else:
            assert last
            n = bsz * n_lat
            logits = merged[2].reshape(n, HEAD_W)[:, :N_EXPERTS]
            wts, dest, src_token, tile_expert, n_tiles = _route(logits, tm)
            x_sorted = jnp.take(h2.reshape(n, d), src_token, axis=0)
            y = _moe_experts(x_sorted, tile_expert, n_tiles, moe_wg[j].astype(BF16), moe_wu[j].astype(BF16),
                             moe_wd[j].astype(BF16))
            y0 = jnp.take(y, dest[:, 0], axis=0).reshape(bsz, n_lat, d)
            y1 = jnp.take(y, dest[:, 1], axis=0).reshape(bsz, n_lat, d)
            out = _final(x_new, y0, y1, wts.reshape(bsz, n_lat, TOP_K), mod3, g_final)
    return out
```

```python
import functools
import math

import jax
import jax.numpy as jnp
from jax import lax
from jax.experimental import pallas as pl
from jax.experimental.pallas import tpu as pltpu

F32 = jnp.float32
BF16 = jnp.bfloat16

EPS = 1e-6
GRID_W = 64
ROPE_BASE = 10000.0

A_HEADS = 4
A_CHUNK = 64
A_SUB = 16
B_GROUPS = 4
B_CHUNK = 128
C_HEADS = 4
C_DH = 64
HEAD_W = 128
BRANCH_W = 512
N_BRANCH = 3
N_EXPERTS = 8
TOP_K = 2

ROW_TILE = 256
MOD_ROWS = 16
ATT_TQ = 256
ATT_TK = 512
FF_CHUNK = 512
EXP_CLAMP = 80.0
LOG2_E = 1.4426950408889634
V7X_VMEM_LIMIT = 56 * 1024 * 1024


def _cparams(sem, vmem=None):
    return pltpu.CompilerParams(dimension_semantics=sem, vmem_limit_bytes=vmem)


def _resident(shape, index_map):
    return pl.BlockSpec(shape, index_map, pipeline_mode=pl.Buffered(1))


def _rms(xf):
    return xf * lax.rsqrt(jnp.mean(xf * xf, axis=-1, keepdims=True) + EPS)


def _sigmoid(x):
    return 1.0 / (1.0 + jnp.exp(-x))


def _dot(a, b):
    return jnp.dot(a, b, preferred_element_type=F32)


def _dot_nt(a, b):
    return lax.dot_general(a, b, (((1,), (1,)), ((), ())), preferred_element_type=F32)


def _dot_tn(a, b):
    return lax.dot_general(a, b, (((0,), (0,)), ((), ())), preferred_element_type=F32)


def _adaln_kernel(c_ref, w_ref, b_ref, o_ref):
    c = c_ref[...]
    a = c * _sigmoid(c)
    o_ref[0] = jnp.dot(a, w_ref[0], preferred_element_type=F32, precision=lax.Precision.HIGHEST) + b_ref[0]


def _adaln(cvec, w_ada, b_ada):
    depth, d, n = w_ada.shape
    tn = 1536
    return pl.pallas_call(
        _adaln_kernel,
        grid=(depth, n // tn),
        in_specs=[pl.BlockSpec((MOD_ROWS, d), lambda l, j: (0, 0)),
                  pl.BlockSpec((1, d, tn), lambda l, j: (l, 0, j)),
                  pl.BlockSpec((1, 1, tn), lambda l, j: (l, 0, j))],
        out_specs=pl.BlockSpec((1, MOD_ROWS, tn), lambda l, j: (l, 0, j)),
        out_shape=jax.ShapeDtypeStruct((depth, MOD_ROWS, n), F32),
        compiler_params=_cparams(("arbitrary", "arbitrary")),
        name="adaln",
    )(cvec, w_ada, b_ada.reshape(depth, 1, n))


_IN_COLS = (("qa", 512, BF16), ("zf", 512, F32), ("zb", 512, F32), ("ia", 512, BF16), ("og", 512, BF16),
            ("uv", 1024, BF16), ("qc", 512, BF16), ("kc", 512, BF16), ("vc", 512, BF16), ("gt", 3072, BF16))


def _inproj_kernel(x_ref, mod_ref, g_ref, w_ref, cos_ref, sin_ref, *out_refs, d):
    x = x_ref[0]
    mod = mod_ref[0]
    h = _rms(x) * g_ref[...]
    h = (h * (1.0 + mod[:, d:2 * d]) + mod[:, 0:d]).astype(BF16)

    tm = x.shape[0]
    lane = lax.broadcasted_iota(jnp.int32, (tm, 512), 1)
    low_half = (lane % 32) < 16

    def rope(p):
        partner = jnp.where(low_half, pltpu.roll(p, 512 - 16, 1), pltpu.roll(p, 16, 1))
        return p * cos_ref[...] + partner * sin_ref[...]

    lo = 0
    for (name, width, dt), o_ref in zip(_IN_COLS, out_refs):
        for c0 in range(0, width, 512):
            p = _dot(h, w_ref[:, lo + c0:lo + c0 + 512])
            if name in ("qc", "kc"):
                p = rope(p)
            o_ref[0, :, c0:c0 + 512] = p.astype(dt)
        lo += width


def _inproj(xc, mod3, g, w_bf, cos_t, sin_t, n_lat_blocks):
    bsz, t, d = xc.shape
    tm = ROW_TILE
    ncols = w_bf.shape[1]
    ctx_row = bsz

    def mod_map(b, j):
        return (jnp.where(j < n_lat_blocks, b, ctx_row), 0, 0)

    out_shape = [jax.ShapeDtypeStruct((bsz, t, w), dt) for _, w, dt in _IN_COLS]
    out_specs = [pl.BlockSpec((1, tm, w), lambda b, j: (b, j, 0)) for _, w, _ in _IN_COLS]
    return pl.pallas_call(
        functools.partial(_inproj_kernel, d=d),
        grid=(bsz, t // tm),
        in_specs=[pl.BlockSpec((1, tm, d), lambda b, j: (b, j, 0)),
                  pl.BlockSpec((1, 1, mod3.shape[2]), mod_map),
                  pl.BlockSpec((1, d), lambda b, j: (0, 0)),
                  _resident((d, ncols), lambda b, j: (0, 0)),
                  pl.BlockSpec((tm, 512), lambda b, j: (j, 0)),
                  pl.BlockSpec((tm, 512), lambda b, j: (j, 0))],
        out_specs=out_specs,
        out_shape=out_shape,
        compiler_params=_cparams(("arbitrary", "arbitrary"), V7X_VMEM_LIMIT),
        name="inproj",
    )(xc, mod3, g.reshape(1, d), w_bf, cos_t, sin_t)


def _hgrn_direction(q_ref, z_ref, v_ref, lb_ref, o_ref, st_ref, reverse):
    c = A_CHUNK
    row = lax.broadcasted_iota(jnp.int32, (c, c), 0)
    col = lax.broadcasted_iota(jnp.int32, (c, c), 1)
    allowed = (col >= row) if reverse else (col <= row)

    z = z_ref[0]
    lb = lb_ref[...]
    sp = jnp.maximum(-z, 0.0) + jnp.log1p(jnp.exp(-jnp.abs(z)))
    la = jnp.log(lb)
    lc = jnp.log1p(-lb) - sp
    logf = jnp.maximum(la, lc) + jnp.log1p(jnp.exp(-jnp.abs(la - lc)))
    kin = (1.0 - lb) * jnp.exp(-z - sp)
    b = jnp.dot(allowed.astype(F32), logf, preferred_element_type=F32, precision=lax.Precision.HIGHEST)
    tot_row = 0 if reverse else c - 1
    tot = b[tot_row:tot_row + 1, :]

    q = q_ref[0].astype(F32)
    v = v_ref[0]
    zero_blk = jnp.zeros((A_SUB, HEAD_W), F32)
    for h in range(A_HEADS):
        hs = slice(h * HEAD_W, (h + 1) * HEAD_W)
        bh, qh, kh, vh = b[:, hs], q[:, hs], kin[:, hs], v[:, hs]
        q_state = qh * jnp.exp(bh)
        k_state = kh * jnp.exp(tot[:, hs] - bh)
        q_rows, k_cols = [], []
        for i in range(c // A_SUB):
            r0 = i * A_SUB
            ref = r0 + A_SUB - 1 if reverse else r0
            r = bh[ref:ref + 1, :]
            qt = qh[r0:r0 + A_SUB] * jnp.exp(bh[r0:r0 + A_SUB] - r)
            q_rows.append(jnp.concatenate([qt if j == i else zero_blk for j in range(c // A_SUB)], axis=1))
            k_cols.append(kh * jnp.exp(jnp.minimum(r - bh, EXP_CLAMP)))
        q_wide = jnp.concatenate(q_rows, axis=0).astype(BF16)
        k_wide = jnp.concatenate(k_cols, axis=1).astype(BF16)
        att = jnp.where(allowed, _dot_nt(q_wide, k_wide), 0.0)
        st = st_ref[h]
        o = _dot(att.astype(BF16), vh) + _dot_nt(q_state.astype(BF16), st.astype(BF16))
        o_ref[0, :, hs] = o.astype(o_ref.dtype)
        st_ref[h] = st * jnp.exp(tot[:, hs]) + _dot_tn(vh, k_state.astype(BF16))


def _hgrn_kernel(qf_ref, zf_ref, vf_ref, qb_ref, zb_ref, vb_ref, lbf_ref, lbb_ref, of_ref, ob_ref, sf_ref, sb_ref):
    @pl.when(pl.program_id(1) == 0)
    def _():
        sf_ref[...] = jnp.zeros_like(sf_ref)
        sb_ref[...] = jnp.zeros_like(sb_ref)

    _hgrn_direction(qf_ref, zf_ref, vf_ref, lbf_ref, of_ref, sf_ref, reverse=False)
    _hgrn_direction(qb_ref, zb_ref, vb_ref, lbb_ref, ob_ref, sb_ref, reverse=True)


def _hgrn(qa, zf, zb, ia, lb_f, lb_b, n_lat):
    bsz, t, w = qa.shape
    c = A_CHUNK
    n_chunks = t // c
    lat_chunks = n_lat // c

    def fwd(b, s):
        return (b, (s + lat_chunks) % n_chunks, 0)

    def bwd(b, s):
        return (b, n_chunks - 1 - s, 0)

    blk = (1, c, w)
    vec = pl.BlockSpec((1, w), lambda b, s: (0, 0))
    return pl.pallas_call(
        _hgrn_kernel,
        grid=(bsz, n_chunks),
        in_specs=[pl.BlockSpec(blk, fwd), pl.BlockSpec(blk, fwd), pl.BlockSpec(blk, fwd),
                  pl.BlockSpec(blk, bwd), pl.BlockSpec(blk, bwd), pl.BlockSpec(blk, bwd), vec, vec],
        out_specs=[pl.BlockSpec(blk, fwd), pl.BlockSpec(blk, bwd)],
        out_shape=[jax.ShapeDtypeStruct((bsz, t, w), BF16)] * 2,
        scratch_shapes=[pltpu.VMEM((A_HEADS, HEAD_W, HEAD_W), F32)] * 2,
        compiler_params=_cparams(("arbitrary", "arbitrary")),
        name="hgrn2",
    )(qa, zf, ia, qa, zb, ia, lb_f.reshape(1, w), lb_b.reshape(1, w))


def _attn_kernel(q_ref, k_ref, v_ref, lam_ref, sub_ref, o_ref, v1_ref, m_ref, acc_ref, s_ref, *, n_lat, n_ctx, lam_init,
                 with_ctx_out):
    tq = q_ref.shape[1]

    @pl.when(pl.program_id(2) == 0)
    def _():
        v1_ref[:, :HEAD_W] = v_ref[0]
        v1_ref[:, HEAD_W:] = jnp.ones((v1_ref.shape[0], HEAD_W), BF16)

    lane = lax.broadcasted_iota(jnp.int32, (tq, HEAD_W), 1)
    qf = q_ref[0].astype(F32) * (C_DH ** -0.5 * LOG2_E)
    q2 = jnp.concatenate([jnp.where(lane < C_DH, qf, 0.0), jnp.where(lane >= C_DH, qf, 0.0)], axis=0).astype(BF16)

    m_ref[...] = jnp.full_like(m_ref, -1e30)
    acc_ref[...] = jnp.zeros_like(acc_ref)

    def scores(start, size):
        return _dot_nt(q2, k_ref[0, pl.ds(start, size), :])

    def absorb(s, start, size):
        m_old = m_ref[...]
        m_new = jnp.maximum(m_old, jnp.max(s, axis=-1, keepdims=True))
        alpha = jnp.exp2(m_old - m_new)
        p = jnp.exp2(s - pltpu.repeat(m_new, size // HEAD_W, axis=1))
        acc_ref[...] = (pltpu.repeat(alpha, 2, axis=1) * acc_ref[...]
                        + _dot(p.astype(BF16), v1_ref[pl.ds(start, size), :]))
        m_ref[...] = m_new

    def blk(i):
        return pl.multiple_of(i * ATT_TK, ATT_TK)

    def latent_then_context():
        n_blk = n_lat // ATT_TK
        n_pairs = (n_blk - 1) // 2
        s_ref[...] = scores(0, ATT_TK)

        def body(i, carry):
            s1 = scores(blk(2 * i + 1), ATT_TK)
            absorb(s_ref[...], blk(2 * i), ATT_TK)
            s2 = scores(blk(2 * i + 2), ATT_TK)
            absorb(s1, blk(2 * i + 1), ATT_TK)
            s_ref[...] = s2
            return carry

        lax.fori_loop(0, n_pairs, body, 0)
        cur = s_ref[...]
        for b in range(2 * n_pairs, n_blk):
            nxt = scores((b + 1) * ATT_TK, ATT_TK) if b + 1 < n_blk else scores(n_lat, n_ctx)
            absorb(cur, b * ATT_TK, ATT_TK)
            cur = nxt
        absorb(cur, n_lat, n_ctx)

    def context_only():
        absorb(scores(n_lat, n_ctx), n_lat, n_ctx)

    if with_ctx_out:
        pl.when(pl.program_id(2) * tq < n_lat)(latent_then_context)
        pl.when(pl.program_id(2) * tq >= n_lat)(context_only)
    else:
        latent_then_context()

    o12 = acc_ref[:, :HEAD_W] / acc_ref[:, HEAD_W:]
    o = o12[:tq] - lam_ref[...] * o12[tq:]
    o_ref[0] = (_rms(o) * sub_ref[...] * (1.0 - lam_init)).astype(o_ref.dtype)


def _attention(qc, kc, vc, lam, subln, n_lat, with_ctx_out, lam_init):
    bsz, t, w = qc.shape
    n_ctx = t - n_lat
    tq = ATT_TQ
    n_q = (t if with_ctx_out else n_lat) // tq
    return pl.pallas_call(
        functools.partial(_attn_kernel, n_lat=n_lat, n_ctx=n_ctx, lam_init=lam_init, with_ctx_out=with_ctx_out),
        grid=(bsz, C_HEADS, n_q),
        in_specs=[pl.BlockSpec((1, tq, HEAD_W), lambda b, h, i: (b, i, h)),
                  pl.BlockSpec((1, t, HEAD_W), lambda b, h, i: (b, 0, h)),
                  pl.BlockSpec((1, t, HEAD_W), lambda b, h, i: (b, 0, h)),
                  pl.BlockSpec((1, HEAD_W), lambda b, h, i: (0, 0)),
                  pl.BlockSpec((1, HEAD_W), lambda b, h, i: (0, h))],
        out_specs=pl.BlockSpec((1, tq, HEAD_W), lambda b, h, i: (b, i, h)),
        out_shape=jax.ShapeDtypeStruct((bsz, n_q * tq, w), BF16),
        scratch_shapes=[pltpu.VMEM((t, 2 * HEAD_W), BF16), pltpu.VMEM((2 * tq, HEAD_W), F32),
                        pltpu.VMEM((2 * tq, 2 * HEAD_W), F32), pltpu.VMEM((2 * tq, ATT_TK), F32)],
        compiler_params=_cparams(("arbitrary", "arbitrary", "arbitrary")),
        name="diff_attention",
    )(qc, kc, vc, jnp.full((1, HEAD_W), lam, F32), subln.reshape(1, w))


def _gelu_tanh(x):
    return 0.5 * x * (1.0 + jnp.tanh(math.sqrt(2.0 / math.pi) * (x + 0.044715 * (x * x * x))))


def _merge_kernel(x_ref, mod_ref, of_ref, ob_ref, og_ref, gn_ref, uv_ref, vn_ref, ws_ref, bs_ref, yc_ref, gt_ref,
                  wb_ref, wo_ref, g2_ref, wr_ref, x_out_ref, h_out_ref, *maybe_logits_ref, d):
    tm = x_ref.shape[1]
    o = of_ref[0].astype(F32) + ob_ref[0].astype(F32)
    o = jnp.concatenate([_rms(o[:, h * HEAD_W:(h + 1) * HEAD_W]) for h in range(A_HEADS)], axis=1)
    og = og_ref[0].astype(F32)
    ya = o * gn_ref[...] * (og * _sigmoid(og))

    uv = _gelu_tanh(uv_ref[0].astype(F32))
    u = uv[:, :BRANCH_W]
    vv = (_rms(uv[:, BRANCH_W:]) * vn_ref[...]).astype(BF16)
    mixed_rows = []
    for c0 in range(0, tm, B_CHUNK):
        mixed_rows.append(jnp.concatenate(
            [_dot(ws_ref[g], vv[c0:c0 + B_CHUNK, g * HEAD_W:(g + 1) * HEAD_W]) for g in range(B_GROUPS)], axis=1)
            + bs_ref[...])
    yb = u * jnp.concatenate(mixed_rows, axis=0)

    ys = (ya, yb, yc_ref[0])
    merged = None
    for i in range(N_BRANCH):
        gate = _sigmoid(gt_ref[0, :, i * d:(i + 1) * d].astype(F32))
        term = gate * _dot(ys[i].astype(BF16), wb_ref[i])
        merged = term if merged is None else merged + term
    mix = _dot(merged.astype(BF16), wo_ref[...])

    mod = mod_ref[0]
    x_new = x_ref[0] + mod[:, 2 * d:3 * d] * mix
    x_out_ref[0] = x_new
    h = _rms(x_new) * g2_ref[...]
    h = h * (1.0 + mod[:, 4 * d:5 * d]) + mod[:, 3 * d:4 * d]
    h_out_ref[0] = h.astype(BF16)
    if maybe_logits_ref:
        maybe_logits_ref[0][0] = jnp.dot(h, wr_ref[...], preferred_element_type=F32, precision=lax.Precision.HIGHEST)


def _merge(xc, mod3, proj, of, ob, yc, gnorm, vnorm, ws_bf, bs_full, wb_bf, wo_bf, g2, w_router_pad,
           n_lat_blocks, n_blocks, with_logits):
    bsz, _, d = xc.shape
    tm = ROW_TILE
    ctx_row = bsz

    def rows(w):
        return pl.BlockSpec((1, tm, w), lambda b, j: (b, j, 0))

    def const2(shape):
        return pl.BlockSpec(shape, lambda b, j: (0, 0))

    def mod_map(b, j):
        return (jnp.where(j < n_lat_blocks, b, ctx_row), 0, 0)

    n_rows = n_blocks * tm
    out_shape = [jax.ShapeDtypeStruct((bsz, n_rows, d), F32), jax.ShapeDtypeStruct((bsz, n_rows, d), BF16)]
    out_specs = [rows(d), rows(d)]
    if with_logits:
        out_shape.append(jax.ShapeDtypeStruct((bsz, n_rows, HEAD_W), F32))
        out_specs.append(rows(HEAD_W))
    return pl.pallas_call(
        functools.partial(_merge_kernel, d=d),
        grid=(bsz, n_blocks),
        in_specs=[rows(d), pl.BlockSpec((1, 1, mod3.shape[2]), mod_map),
                  rows(BRANCH_W), rows(BRANCH_W), rows(BRANCH_W), const2((1, BRANCH_W)),
                  rows(2 * BRANCH_W), const2((1, BRANCH_W)),
                  _resident((B_GROUPS, B_CHUNK, B_CHUNK), lambda b, j: (0, 0, 0)), const2((B_CHUNK, BRANCH_W)),
                  rows(BRANCH_W), rows(N_BRANCH * d),
                  _resident((N_BRANCH, BRANCH_W, d), lambda b, j: (0, 0, 0)), _resident((d, d), lambda b, j: (0, 0)),
                  const2((1, d)), _resident((d, HEAD_W), lambda b, j: (0, 0))],
        out_specs=out_specs,
        out_shape=out_shape,
        compiler_params=_cparams(("arbitrary", "arbitrary"), V7X_VMEM_LIMIT),
        name="merge",
    )(xc, mod3, of, ob, proj["og"], gnorm.reshape(1, BRANCH_W), proj["uv"], vnorm.reshape(1, BRANCH_W), ws_bf,
      bs_full, yc, proj["gt"], wb_bf, wo_bf, g2.reshape(1, d), w_router_pad)


def _ffn_kernel(x_ref, h_ref, mod_ref, wg_ref, wu_ref, wd_ref, o_ref, *, d):
    h = h_ref[0]
    g = _dot(h, wg_ref[...])
    a = (g * _sigmoid(g) * _dot(h, wu_ref[...])).astype(BF16)
    y = _dot(a, wd_ref[...])
    o_ref[0] = x_ref[0] + mod_ref[0][:, 5 * d:6 * d] * y


def _ffn(xc, h2, mod3, wg_bf, wu_bf, wd_bf, n_lat_blocks):
    bsz, t, d = xc.shape
    tm = ROW_TILE
    dff = wg_bf.shape[1]
    ctx_row = bsz

    def mod_map(b, j):
        return (jnp.where(j < n_lat_blocks, b, ctx_row), 0, 0)

    rows = pl.BlockSpec((1, tm, d), lambda b, j: (b, j, 0))
    return pl.pallas_call(
        functools.partial(_ffn_kernel, d=d),
        grid=(bsz, t // tm),
        in_specs=[rows, rows, pl.BlockSpec((1, 1, mod3.shape[2]), mod_map),
                  _resident((d, dff), lambda b, j: (0, 0)), _resident((d, dff), lambda b, j: (0, 0)),
                  _resident((dff, d), lambda b, j: (0, 0))],
        out_specs=rows,
        out_shape=jax.ShapeDtypeStruct((bsz, t, d), F32),
        compiler_params=_cparams(("arbitrary", "arbitrary"), V7X_VMEM_LIMIT),
        name="dense_ffn",
    )(xc, h2, mod3, wg_bf, wu_bf, wd_bf)


def _moe_kernel(tile_expert_ref, n_tiles_ref, x_ref, wg_ref, wu_ref, wd_ref, o_ref):
    del tile_expert_ref

    @pl.when(pl.program_id(0) < n_tiles_ref[0])
    def _():
        x = x_ref[...]
        dff = wg_ref.shape[2]
        acc = None
        for c0 in range(0, dff, FF_CHUNK):
            g = _dot(x, wg_ref[0, :, c0:c0 + FF_CHUNK])
            a = (g * _sigmoid(g) * _dot(x, wu_ref[0, :, c0:c0 + FF_CHUNK])).astype(BF16)
            y = _dot(a, wd_ref[0, c0:c0 + FF_CHUNK, :])
            acc = y if acc is None else acc + y
        o_ref[...] = acc

    @pl.when(pl.program_id(0) >= n_tiles_ref[0])
    def _():
        o_ref[...] = jnp.zeros_like(o_ref)


def _moe_experts(x_sorted, tile_expert, n_tiles, wg_bf, wu_bf, wd_bf):
    p, d = x_sorted.shape
    tm = ROW_TILE
    dff = wg_bf.shape[2]
    grid_spec = pltpu.PrefetchScalarGridSpec(
        num_scalar_prefetch=2,
        grid=(p // tm,),
        in_specs=[pl.BlockSpec((tm, d), lambda t, te, nt: (t, 0)),
                  pl.BlockSpec((1, d, dff), lambda t, te, nt: (te[t], 0, 0)),
                  pl.BlockSpec((1, d, dff), lambda t, te, nt: (te[t], 0, 0)),
                  pl.BlockSpec((1, dff, d), lambda t, te, nt: (te[t], 0, 0))],
        out_specs=pl.BlockSpec((tm, d), lambda t, te, nt: (t, 0)),
    )
    return pl.pallas_call(
        _moe_kernel,
        grid_spec=grid_spec,
        out_shape=jax.ShapeDtypeStruct((p, d), F32),
        compiler_params=_cparams(("arbitrary",), V7X_VMEM_LIMIT),
        name="moe_experts",
    )(tile_expert, n_tiles, x_sorted, wg_bf, wu_bf, wd_bf)


def _route(logits, tm):
    n = logits.shape[0]
    top_v, top_i = lax.top_k(logits, TOP_K)
    wts = jax.nn.softmax(top_v, axis=-1)
    flat_e = top_i.reshape(-1)
    onehot = (flat_e[:, None] == jnp.arange(N_EXPERTS)[None, :]).astype(jnp.int32)
    rank = jnp.take_along_axis(jnp.cumsum(onehot, axis=0) - onehot, flat_e[:, None], axis=1)[:, 0]
    counts = jnp.sum(onehot, axis=0)
    padded = ((counts + tm - 1) // tm) * tm
    ends = jnp.cumsum(padded)
    dest = (ends - padded)[flat_e] + rank
    p = TOP_K * n + N_EXPERTS * tm
    src_token = jnp.zeros((p,), jnp.int32).at[dest].set(jnp.arange(TOP_K * n, dtype=jnp.int32) // TOP_K)
    tile_start = jnp.arange(p // tm, dtype=jnp.int32) * tm
    tile_expert = jnp.minimum(jnp.sum((tile_start[:, None] >= ends[None, :]).astype(jnp.int32), axis=1),
                              N_EXPERTS - 1).astype(jnp.int32)
    n_tiles = (ends[-1] // tm).astype(jnp.int32).reshape(1)
    return wts, dest.reshape(n, TOP_K), src_token, tile_expert, n_tiles


def _final_kernel(x_ref, y0_ref, y1_ref, w_ref, mod_ref, g_ref, o_ref, *, d):
    w = w_ref[0]
    y = w[:, 0:1] * y0_ref[0] + w[:, 1:2] * y1_ref[0]
    x = x_ref[0] + mod_ref[0][:, 5 * d:6 * d] * y
    o_ref[0] = _rms(x) * g_ref[...]


def _final(x_lat, y0, y1, wts, mod3, g_final):
    bsz, s, d = x_lat.shape
    tm = ROW_TILE
    rows = pl.BlockSpec((1, tm, d), lambda b, j: (b, j, 0))
    return pl.pallas_call(
        functools.partial(_final_kernel, d=d),
        grid=(bsz, s // tm),
        in_specs=[rows, rows, rows, pl.BlockSpec((1, tm, TOP_K), lambda b, j: (b, j, 0)),
                  pl.BlockSpec((1, 1, mod3.shape[2]), lambda b, j: (b, 0, 0)),
                  pl.BlockSpec((1, d), lambda b, j: (0, 0))],
        out_specs=rows,
        out_shape=jax.ShapeDtypeStruct((bsz, s, d), F32),
        compiler_params=_cparams(("arbitrary", "arbitrary")),
        name="moe_combine_final_norm",
    )(x_lat, y0, y1, wts, mod3, g_final.reshape(1, d))


def _rope_tables(n_lat, n_ctx):
    pairs = C_DH // 4
    tpos = jnp.arange(n_lat)
    pos = jnp.stack([(tpos // GRID_W).astype(F32), (tpos % GRID_W).astype(F32)], axis=1)
    freqs = ROPE_BASE ** (-jnp.arange(pairs, dtype=F32) / pairs)
    ang = pos[:, :, None] * freqs[None, None, :]
    cos = jnp.repeat(jnp.cos(ang)[:, :, None, :], 2, axis=2).reshape(n_lat, C_DH)
    sin = jnp.sin(ang)[:, :, None, :] * jnp.array([-1.0, 1.0], F32)[None, None, :, None]
    sin = sin.reshape(n_lat, C_DH)
    cos = jnp.concatenate([cos, jnp.ones((n_ctx, C_DH), F32)], axis=0)
    sin = jnp.concatenate([sin, jnp.zeros((n_ctx, C_DH), F32)], axis=0)
    reps = 512 // C_DH
    return jnp.tile(cos, (1, reps)), jnp.tile(sin, (1, reps))


def _lower_bound(p, layer):
    cs = jnp.cumsum(jax.nn.softmax(p.astype(F32), axis=0), axis=0)
    return cs[layer] - cs[0]


def kernel(x, c, ctx, c_ctx, w_ada, b_ada, g_norm1, g_norm2, w_in, hgrn_lb, hgrn_gnorm, mlp_vnorm, mlp_ws, mlp_bs,
           diff_lambda, diff_subln, w_branch, w_out, ffn_wg, ffn_wu, ffn_wd, moe_router, moe_wg, moe_wu, moe_wd,
           g_final):
    bsz, n_lat, d = x.shape
    n_ctx = ctx.shape[1]
    depth = w_ada.shape[0]
    t = n_lat + n_ctx
    tm = ROW_TILE
    assert depth == 2 and bsz < MOD_ROWS and n_lat % ATT_TK == 0 and n_ctx % tm == 0 and n_lat % GRID_W == 0
    n_lat_blocks = n_lat // tm

    cvec = jnp.zeros((MOD_ROWS, d), F32).at[:bsz].set(c).at[bsz].set(c_ctx)
    mods = _adaln(cvec, w_ada, b_ada)
    cos_t, sin_t = _rope_tables(n_lat, n_ctx)
    xc = jnp.concatenate([x, ctx], axis=1)

    out = None
    for layer in range(depth):
        last = layer == depth - 1
        mod3 = mods[layer].reshape(MOD_ROWS, 1, 6 * d)
        proj = dict(zip([n for n, _, _ in _IN_COLS],
                        _inproj(xc, mod3, g_norm1[layer], w_in[layer].astype(BF16), cos_t, sin_t, n_lat_blocks)))
        lb_f = _lower_bound(hgrn_lb[0], layer)
        lb_b = _lower_bound(hgrn_lb[1], layer)
        of, ob = _hgrn(proj["qa"], proj["zf"], proj["zb"], proj["ia"], lb_f, lb_b, n_lat)

        lam_init = 0.8 - 0.6 * math.exp(-0.3 * layer)
        lam_p = diff_lambda[layer]
        lam = jnp.exp(jnp.sum(lam_p[0] * lam_p[1])) - jnp.exp(jnp.sum(lam_p[2] * lam_p[3])) + lam_init
        yc = _attention(proj["qc"], proj["kc"], proj["vc"], lam.astype(F32), diff_subln[layer], n_lat,
                        not last, lam_init)

        bs_full = jnp.repeat(mlp_bs[layer].T, HEAD_W, axis=1)
        j = layer // 2
        is_moe = layer % 2 == 1
        w_router_pad = jnp.zeros((d, HEAD_W), F32)
        if is_moe:
            w_router_pad = w_router_pad.at[:, :N_EXPERTS].set(moe_router[j])
        n_blocks = n_lat_blocks if last else t // tm
        merged = _merge(xc, mod3, proj, of, ob, yc, hgrn_gnorm[layer], mlp_vnorm[layer], mlp_ws[layer].astype(BF16),
                        bs_full, w_branch[layer].astype(BF16), w_out[layer].astype(BF16), g_norm2[layer],
                        w_router_pad, n_lat_blocks, n_blocks, is_moe)
        x_new, h2 = merged[0], merged[1]

        if not is_moe:
            assert not last
            xc = _ffn(x_new, h2, mod3, ffn_wg[j].astype(BF16), ffn_wu[j].astype(BF16), ffn_wd[j].astype(BF16),
                      n_lat_blocks)
        else:
            assert last
            n = bsz * n_lat
            logits = merged[2].reshape(n, HEAD_W)[:, :N_EXPERTS]
            wts, dest, src_token, tile_expert, n_tiles = _route(logits, tm)
            x_sorted = jnp.take(h2.reshape(n, d), src_token, axis=0)
            y = _moe_experts(x_sorted, tile_expert, n_tiles, moe_wg[j].astype(BF16), moe_wu[j].astype(BF16),
                             moe_wd[j].astype(BF16))
            y0 = jnp.take(y, dest[:, 0], axis=0).reshape(bsz, n_lat, d)
            y1 = jnp.take(y, dest[:, 1], axis=0).reshape(bsz, n_lat, d)
            out = _final(x_new, y0, y1, wts.reshape(bsz, n_lat, TOP_K), mod3, g_final)
    return out
```

```python
import functools
import math

import jax
import jax.numpy as jnp
from jax import lax
from jax.experimental import pallas as pl
from jax.experimental.pallas import tpu as pltpu

F32 = jnp.float32
BF16 = jnp.bfloat16

EPS = 1e-6
GRID_W = 64
ROPE_BASE = 10000.0

A_HEADS = 4
A_CHUNK = 64
A_SUB = 16
B_GROUPS = 4
B_CHUNK = 128
C_HEADS = 4
C_DH = 64
HEAD_W = 128
BRANCH_W = 512
N_BRANCH = 3
N_EXPERTS = 8
TOP_K = 2

ROW_TILE = 256
MOD_ROWS = 16
ATT_TQ = 256
ATT_TK = 512
FF_CHUNK = 512
EXP_CLAMP = 80.0
LOG2_E = 1.4426950408889634
V7X_VMEM_LIMIT = 56 * 1024 * 1024


def _cparams(sem, vmem=None):
    return pltpu.CompilerParams(dimension_semantics=sem, vmem_limit_bytes=vmem)


def _resident(shape, index_map):
    return pl.BlockSpec(shape, index_map, pipeline_mode=pl.Buffered(1))


def _rms(xf):
    return xf * lax.rsqrt(jnp.mean(xf * xf, axis=-1, keepdims=True) + EPS)


def _sigmoid(x):
    return 1.0 / (1.0 + jnp.exp(-x))


def _dot(a, b):
    return jnp.dot(a, b, preferred_element_type=F32)


def _dot_nt(a, b):
    return lax.dot_general(a, b, (((1,), (1,)), ((), ())), preferred_element_type=F32)


def _dot_tn(a, b):
    return lax.dot_general(a, b, (((0,), (0,)), ((), ())), preferred_element_type=F32)


def _adaln_kernel(c_ref, w_ref, b_ref, o_ref):
    c = c_ref[...]
    a = c * _sigmoid(c)
    o_ref[0] = jnp.dot(a, w_ref[0], preferred_element_type=F32, precision=lax.Precision.HIGHEST) + b_ref[0]


def _adaln(cvec, w_ada, b_ada):
    depth, d, n = w_ada.shape
    tn = 1536
    return pl.pallas_call(
        _adaln_kernel,
        grid=(depth, n // tn),
        in_specs=[pl.BlockSpec((MOD_ROWS, d), lambda l, j: (0, 0)),
                  pl.BlockSpec((1, d, tn), lambda l, j: (l, 0, j)),
                  pl.BlockSpec((1, 1, tn), lambda l, j: (l, 0, j))],
        out_specs=pl.BlockSpec((1, MOD_ROWS, tn), lambda l, j: (l, 0, j)),
        out_shape=jax.ShapeDtypeStruct((depth, MOD_ROWS, n), F32),
        compiler_params=_cparams(("arbitrary", "arbitrary")),
        name="adaln",
    )(cvec, w_ada, b_ada.reshape(depth, 1, n))


_IN_COLS = (("qa", 512), ("zf", 512), ("zb", 512), ("ia", 512), ("og", 512), ("uv", 1024), ("qc", 512), ("kc", 512),
            ("vc", 512), ("gt", 3072))
_IN_OUTS = (("qa", 512, BF16), ("bf", 512, F32), ("kf", 512, BF16), ("bb", 512, F32), ("kb", 512, BF16),
            ("ia", 512, BF16), ("og", 512, BF16), ("uv", 1024, BF16), ("qc", 512, BF16), ("kc", 512, BF16),
            ("vc", 512, BF16), ("gt", 3072, BF16))


def _forget_gate(z, lb):
    sp = jnp.maximum(-z, 0.0) + jnp.log(1.0 + jnp.exp(-jnp.abs(z)))
    la = jnp.log(lb)
    lc = jnp.log1p(-lb) - sp
    logf = jnp.maximum(la, lc) + jnp.log(1.0 + jnp.exp(-jnp.abs(la - lc)))
    kin = (1.0 - lb) * jnp.exp(-z - sp)
    hi = logf.astype(BF16)
    rest = logf - hi.astype(F32)
    mid = rest.astype(BF16)
    low = (rest - mid.astype(F32)).astype(BF16)
    return (hi, mid, low), kin


def _inproj_kernel(x_ref, mod_ref, g_ref, w_ref, cos_ref, sin_ref, lbf_ref, lbb_ref, trif_ref, trib_ref, *out_refs, d):
    x = x_ref[0]
    mod = mod_ref[0]
    h = _rms(x) * g_ref[...]
    h = (h * (1.0 + mod[:, d:2 * d]) + mod[:, 0:d]).astype(BF16)

    tm = x.shape[0]
    lane = lax.broadcasted_iota(jnp.int32, (tm, 512), 1)
    low_half = (lane % 32) < 16

    def rope(p):
        partner = jnp.where(low_half, pltpu.roll(p, 512 - 16, 1), pltpu.roll(p, 16, 1))
        return p * cos_ref[...] + partner * sin_ref[...]

    outs = {name: ref for (name, _, _), ref in zip(_IN_OUTS, out_refs)}
    col0 = {}
    lo = 0
    for name, width in _IN_COLS:
        col0[name] = lo
        lo += width

    gates = {}
    for name, lb_ref in (("zf", lbf_ref), ("zb", lbb_ref)):
        terms, kin = _forget_gate(_dot(h, w_ref[:, col0[name]:col0[name] + 512]), lb_ref[...])
        outs["k" + name[1]][0] = kin.astype(BF16)
        gates[name] = terms
    for name, width in _IN_COLS:
        if name in gates:
            continue
        for c0 in range(0, width, 512):
            p = _dot(h, w_ref[:, col0[name] + c0:col0[name] + c0 + 512])
            if name in ("qc", "kc"):
                p = rope(p)
            outs[name][0, :, c0:c0 + 512] = p.astype(outs[name].dtype)
    for name, tri_ref in (("zf", trif_ref), ("zb", trib_ref)):
        hi, mid, low = gates[name]
        outs["b" + name[1]][0] = _dot(tri_ref[...], hi) + _dot(tri_ref[...], mid) + _dot(tri_ref[...], low)


def _scan_matrices(tm):
    row = jnp.arange(tm)[:, None]
    col = jnp.arange(tm)[None, :]
    same = (row // A_CHUNK) == (col // A_CHUNK)
    return (same & (col <= row)).astype(BF16), (same & (col >= row)).astype(BF16)


def _inproj(xc, mod3, g, w_bf, cos_t, sin_t, lb_f, lb_b, n_lat_blocks):
    bsz, t, d = xc.shape
    tm = ROW_TILE
    ncols = w_bf.shape[1]
    ctx_row = bsz

    def mod_map(b, j):
        return (jnp.where(j < n_lat_blocks, b, ctx_row), 0, 0)

    out_shape = [jax.ShapeDtypeStruct((bsz, t, w), dt) for _, w, dt in _IN_OUTS]
    out_specs = [pl.BlockSpec((1, tm, w), lambda b, j: (b, j, 0)) for _, w, _ in _IN_OUTS]
    tri_f, tri_b = _scan_matrices(tm)
    vec = pl.BlockSpec((1, 512), lambda b, j: (0, 0))
    outs = pl.pallas_call(
        functools.partial(_inproj_kernel, d=d),
        grid=(bsz, t // tm),
        in_specs=[pl.BlockSpec((1, tm, d), lambda b, j: (b, j, 0)),
                  pl.BlockSpec((1, 1, mod3.shape[2]), mod_map),
                  pl.BlockSpec((1, d), lambda b, j: (0, 0)),
                  _resident((d, ncols), lambda b, j: (0, 0)),
                  pl.BlockSpec((tm, 512), lambda b, j: (j, 0)),
                  pl.BlockSpec((tm, 512), lambda b, j: (j, 0)),
                  vec, vec, _resident((tm, tm), lambda b, j: (0, 0)), _resident((tm, tm), lambda b, j: (0, 0))],
        out_specs=out_specs,
        out_shape=out_shape,
        compiler_params=_cparams(("arbitrary", "arbitrary"), V7X_VMEM_LIMIT),
        name="inproj",
    )(xc, mod3, g.reshape(1, d), w_bf, cos_t, sin_t, lb_f.reshape(1, 512), lb_b.reshape(1, 512), tri_f, tri_b)
    return {name: o for (name, _, _), o in zip(_IN_OUTS, outs)}


def _hgrn_direction(q_ref, b_ref, k_ref, v_ref, o_ref, st_ref, reverse):
    c = A_CHUNK
    row = lax.broadcasted_iota(jnp.int32, (c, c), 0)
    col = lax.broadcasted_iota(jnp.int32, (c, c), 1)
    allowed = (col >= row) if reverse else (col <= row)

    b = b_ref[0]
    tot_row = 0 if reverse else c - 1
    tot = b[tot_row:tot_row + 1, :]

    q = q_ref[0].astype(F32)
    kin = k_ref[0].astype(F32)
    v = v_ref[0]
    zero_blk = jnp.zeros((A_SUB, HEAD_W), F32)
    heads = []
    for h in range(A_HEADS):
        hs = slice(h * HEAD_W, (h + 1) * HEAD_W)
        bh, qh, kh = b[:, hs], q[:, hs], kin[:, hs]
        q_state = qh * jnp.exp(bh)
        k_state = kh * jnp.exp(tot[:, hs] - bh)
        q_rows, k_cols = [], []
        for i in range(c // A_SUB):
            r0 = i * A_SUB
            ref = r0 + A_SUB - 1 if reverse else r0
            r = bh[ref:ref + 1, :]
            qt = qh[r0:r0 + A_SUB] * jnp.exp(bh[r0:r0 + A_SUB] - r)
            q_rows.append(jnp.concatenate([qt if j == i else zero_blk for j in range(c // A_SUB)], axis=1))
            k_cols.append(kh * jnp.exp(jnp.minimum(r - bh, EXP_CLAMP)))
        heads.append(dict(
            hs=hs, h=h, allowed=allowed, vh=v[:, hs], decay=jnp.exp(tot[:, hs]), o_ref=o_ref, st_ref=st_ref,
            q_wide=jnp.concatenate(q_rows, axis=0).astype(BF16),
            k_wide=jnp.concatenate(k_cols, axis=1).astype(BF16),
            q_state=q_state.astype(BF16), k_state=k_state.astype(BF16)))
    return heads


def _hgrn_kernel(qf_ref, bf_ref, kf_ref, vf_ref, qb_ref, bb_ref, kb_ref, vb_ref, of_ref, ob_ref, sf_ref, sb_ref):
    @pl.when(pl.program_id(1) == 0)
    def _():
        sf_ref[...] = jnp.zeros_like(sf_ref)
        sb_ref[...] = jnp.zeros_like(sb_ref)

    chains = (_hgrn_direction(qf_ref, bf_ref, kf_ref, vf_ref, of_ref, sf_ref, reverse=False)
              + _hgrn_direction(qb_ref, bb_ref, kb_ref, vb_ref, ob_ref, sb_ref, reverse=True))
    for ch in chains:
        ch["st"] = ch["st_ref"][ch["h"]]
        ch["att"] = _dot_nt(ch["q_wide"], ch["k_wide"])
    for ch in chains:
        ch["inter"] = _dot_nt(ch["q_state"], ch["st"].astype(BF16))
        ch["upd"] = _dot_tn(ch["vh"], ch["k_state"])
    for ch in chains:
        att = jnp.where(ch["allowed"], ch["att"], 0.0).astype(BF16)
        o = _dot(att, ch["vh"]) + ch["inter"]
        ch["o_ref"][0, :, ch["hs"]] = o.astype(ch["o_ref"].dtype)
        ch["st_ref"][ch["h"]] = ch["st"] * ch["decay"] + ch["upd"]


def _hgrn(qa, bf, kf, bb, kb, ia, n_lat):
    bsz, t, w = qa.shape
    c = A_CHUNK
    n_chunks = t // c
    lat_chunks = n_lat // c

    def fwd(b, s):
        return (b, (s + lat_chunks) % n_chunks, 0)

    def bwd(b, s):
        return (b, n_chunks - 1 - s, 0)

    blk = (1, c, w)
    return pl.pallas_call(
        _hgrn_kernel,
        grid=(bsz, n_chunks),
        in_specs=[pl.BlockSpec(blk, fwd)] * 4 + [pl.BlockSpec(blk, bwd)] * 4,
        out_specs=[pl.BlockSpec(blk, fwd), pl.BlockSpec(blk, bwd)],
        out_shape=[jax.ShapeDtypeStruct((bsz, t, w), BF16)] * 2,
        scratch_shapes=[pltpu.VMEM((A_HEADS, HEAD_W, HEAD_W), F32)] * 2,
        compiler_params=_cparams(("arbitrary", "arbitrary")),
        name="hgrn2",
    )(qa, bf, kf, ia, qa, bb, kb, ia)


def _attn_kernel(q_ref, k_ref, v_ref, lam_ref, sub_ref, o_ref, v1_ref, m_ref, acc_ref, *, n_lat, n_ctx, lam_init,
                 with_ctx_out):
    tq = q_ref.shape[1]

    @pl.when(pl.program_id(2) == 0)
    def _():
        v1_ref[:, :HEAD_W] = v_ref[0]
        v1_ref[:, HEAD_W:] = jnp.ones((v1_ref.shape[0], HEAD_W), BF16)

    lane = lax.broadcasted_iota(jnp.int32, (tq, HEAD_W), 1)
    qf = q_ref[0].astype(F32) * (C_DH ** -0.5 * LOG2_E)
    q2 = jnp.concatenate([jnp.where(lane < C_DH, qf, 0.0), jnp.where(lane >= C_DH, qf, 0.0)], axis=0).astype(BF16)

    m_ref[...] = jnp.full_like(m_ref, -1e30)
    acc_ref[...] = jnp.zeros_like(acc_ref)

    def scores(start, size):
        return _dot_nt(q2, k_ref[0, pl.ds(start, size), :])

    def absorb(s, start, size):
        m_old = m_ref[...]
        m_new = jnp.maximum(m_old, jnp.max(s, axis=-1, keepdims=True))
        alpha = jnp.exp2(m_old - m_new)
        p = jnp.exp2(s - pltpu.repeat(m_new, size // HEAD_W, axis=1))
        acc_ref[...] = (pltpu.repeat(alpha, 2, axis=1) * acc_ref[...]
                        + _dot(p.astype(BF16), v1_ref[pl.ds(start, size), :]))
        m_ref[...] = m_new

    def latent_then_context():
        n_blk = n_lat // ATT_TK
        cur = scores(0, ATT_TK)
        for b in range(n_blk):
            nxt = scores((b + 1) * ATT_TK, ATT_TK) if b + 1 < n_blk else scores(n_lat, n_ctx)
            absorb(cur, b * ATT_TK, ATT_TK)
            cur = nxt
        absorb(cur, n_lat, n_ctx)

    def context_only():
        absorb(scores(n_lat, n_ctx), n_lat, n_ctx)

    if with_ctx_out:
        pl.when(pl.program_id(2) * tq < n_lat)(latent_then_context)
        pl.when(pl.program_id(2) * tq >= n_lat)(context_only)
    else:
        latent_then_context()

    o12 = acc_ref[:, :HEAD_W] / acc_ref[:, HEAD_W:]
    o = o12[:tq] - lam_ref[...] * o12[tq:]
    o_ref[0] = (_rms(o) * sub_ref[...] * (1.0 - lam_init)).astype(o_ref.dtype)


def _attention(qc, kc, vc, lam, subln, n_lat, with_ctx_out, lam_init):
    bsz, t, w = qc.shape
    n_ctx = t - n_lat
    tq = ATT_TQ
    n_q = (t if with_ctx_out else n_lat) // tq
    return pl.pallas_call(
        functools.partial(_attn_kernel, n_lat=n_lat, n_ctx=n_ctx, lam_init=lam_init, with_ctx_out=with_ctx_out),
        grid=(bsz, C_HEADS, n_q),
        in_specs=[pl.BlockSpec((1, tq, HEAD_W), lambda b, h, i: (b, i, h)),
                  pl.BlockSpec((1, t, HEAD_W), lambda b, h, i: (b, 0, h)),
                  pl.BlockSpec((1, t, HEAD_W), lambda b, h, i: (b, 0, h)),
                  pl.BlockSpec((1, HEAD_W), lambda b, h, i: (0, 0)),
                  pl.BlockSpec((1, HEAD_W), lambda b, h, i: (0, h))],
        out_specs=pl.BlockSpec((1, tq, HEAD_W), lambda b, h, i: (b, i, h)),
        out_shape=jax.ShapeDtypeStruct((bsz, n_q * tq, w), BF16),
        scratch_shapes=[pltpu.VMEM((t, 2 * HEAD_W), BF16), pltpu.VMEM((2 * tq, HEAD_W), F32),
                        pltpu.VMEM((2 * tq, 2 * HEAD_W), F32)],
        compiler_params=_cparams(("arbitrary", "arbitrary", "arbitrary")),
        name="diff_attention",
    )(qc, kc, vc, jnp.full((1, HEAD_W), lam, F32), subln.reshape(1, w))


def _gelu_tanh(x):
    return 0.5 * x * (1.0 + jnp.tanh(math.sqrt(2.0 / math.pi) * (x + 0.044715 * (x * x * x))))


def _merge_kernel(x_ref, mod_ref, of_ref, ob_ref, og_ref, gn_ref, uv_ref, vn_ref, ws_ref, bs_ref, yc_ref, gt_ref,
                  wb_ref, wo_ref, g2_ref, wr_ref, x_out_ref, h_out_ref, *maybe_logits_ref, d):
    tm = x_ref.shape[1]
    o = of_ref[0].astype(F32) + ob_ref[0].astype(F32)
    o = jnp.concatenate([_rms(o[:, h * HEAD_W:(h + 1) * HEAD_W]) for h in range(A_HEADS)], axis=1)
    og = og_ref[0].astype(F32)
    ya = o * gn_ref[...] * (og * _sigmoid(og))

    uv = _gelu_tanh(uv_ref[0].astype(F32))
    u = uv[:, :BRANCH_W]
    vv = (_rms(uv[:, BRANCH_W:]) * vn_ref[...]).astype(BF16)
    mixed_rows = []
    for c0 in range(0, tm, B_CHUNK):
        mixed_rows.append(jnp.concatenate(
            [_dot(ws_ref[g], vv[c0:c0 + B_CHUNK, g * HEAD_W:(g + 1) * HEAD_W]) for g in range(B_GROUPS)], axis=1)
            + bs_ref[...])
    yb = u * jnp.concatenate(mixed_rows, axis=0)

    ys = (ya, yb, yc_ref[0])
    merged = None
    for i in range(N_BRANCH):
        gate = _sigmoid(gt_ref[0, :, i * d:(i + 1) * d].astype(F32))
        term = gate * _dot(ys[i].astype(BF16), wb_ref[i])
        merged = term if merged is None else merged + term
    mix = _dot(merged.astype(BF16), wo_ref[...])

    mod = mod_ref[0]
    x_new = x_ref[0] + mod[:, 2 * d:3 * d] * mix
    x_out_ref[0] = x_new
    h = _rms(x_new) * g2_ref[...]
    h = h * (1.0 + mod[:, 4 * d:5 * d]) + mod[:, 3 * d:4 * d]
    h_out_ref[0] = h.astype(BF16)
    if maybe_logits_ref:
        maybe_logits_ref[0][0] = jnp.dot(h, wr_ref[...], preferred_element_type=F32, precision=lax.Precision.HIGHEST)


def _merge(xc, mod3, proj, of, ob, yc, gnorm, vnorm, ws_bf, bs_full, wb_bf, wo_bf, g2, w_router_pad,
           n_lat_blocks, n_blocks, with_logits):
    bsz, _, d = xc.shape
    tm = ROW_TILE
    ctx_row = bsz

    def rows(w):
        return pl.BlockSpec((1, tm, w), lambda b, j: (b, j, 0))

    def const2(shape):
        return pl.BlockSpec(shape, lambda b, j: (0, 0))

    def mod_map(b, j):
        return (jnp.where(j < n_lat_blocks, b, ctx_row), 0, 0)

    n_rows = n_blocks * tm
    out_shape = [jax.ShapeDtypeStruct((bsz, n_rows, d), F32), jax.ShapeDtypeStruct((bsz, n_rows, d), BF16)]
    out_specs = [rows(d), rows(d)]
    if with_logits:
        out_shape.append(jax.ShapeDtypeStruct((bsz, n_rows, HEAD_W), F32))
        out_specs.append(rows(HEAD_W))
    return pl.pallas_call(
        functools.partial(_merge_kernel, d=d),
        grid=(bsz, n_blocks),
        in_specs=[rows(d), pl.BlockSpec((1, 1, mod3.shape[2]), mod_map),
                  rows(BRANCH_W), rows(BRANCH_W), rows(BRANCH_W), const2((1, BRANCH_W)),
                  rows(2 * BRANCH_W), const2((1, BRANCH_W)),
                  _resident((B_GROUPS, B_CHUNK, B_CHUNK), lambda b, j: (0, 0, 0)), const2((B_CHUNK, BRANCH_W)),
                  rows(BRANCH_W), rows(N_BRANCH * d),
                  _resident((N_BRANCH, BRANCH_W, d), lambda b, j: (0, 0, 0)), _resident((d, d), lambda b, j: (0, 0)),
                  const2((1, d)), _resident((d, HEAD_W), lambda b, j: (0, 0))],
        out_specs=out_specs,
        out_shape=out_shape,
        compiler_params=_cparams(("arbitrary", "arbitrary"), V7X_VMEM_LIMIT),
        name="merge",
    )(xc, mod3, of, ob, proj["og"], gnorm.reshape(1, BRANCH_W), proj["uv"], vnorm.reshape(1, BRANCH_W), ws_bf,
      bs_full, yc, proj["gt"], wb_bf, wo_bf, g2.reshape(1, d), w_router_pad)


def _ffn_kernel(x_ref, h_ref, mod_ref, wg_ref, wu_ref, wd_ref, o_ref, *, d):
    h = h_ref[0]
    g = _dot(h, wg_ref[...])
    a = (g * _sigmoid(g) * _dot(h, wu_ref[...])).astype(BF16)
    y = _dot(a, wd_ref[...])
    o_ref[0] = x_ref[0] + mod_ref[0][:, 5 * d:6 * d] * y


def _ffn(xc, h2, mod3, wg_bf, wu_bf, wd_bf, n_lat_blocks):
    bsz, t, d = xc.shape
    tm = ROW_TILE
    dff = wg_bf.shape[1]
    ctx_row = bsz

    def mod_map(b, j):
        return (jnp.where(j < n_lat_blocks, b, ctx_row), 0, 0)

    rows = pl.BlockSpec((1, tm, d), lambda b, j: (b, j, 0))
    return pl.pallas_call(
        functools.partial(_ffn_kernel, d=d),
        grid=(bsz, t // tm),
        in_specs=[rows, rows, pl.BlockSpec((1, 1, mod3.shape[2]), mod_map),
                  _resident((d, dff), lambda b, j: (0, 0)), _resident((d, dff), lambda b, j: (0, 0)),
                  _resident((dff, d), lambda b, j: (0, 0))],
        out_specs=rows,
        out_shape=jax.ShapeDtypeStruct((bsz, t, d), F32),
        compiler_params=_cparams(("arbitrary", "arbitrary"), V7X_VMEM_LIMIT),
        name="dense_ffn",
    )(xc, h2, mod3, wg_bf, wu_bf, wd_bf)


def _moe_kernel(tile_expert_ref, n_tiles_ref, x_ref, wg_ref, wu_ref, wd_ref, o_ref):
    del tile_expert_ref

    @pl.when(pl.program_id(0) < n_tiles_ref[0])
    def _():
        x = x_ref[...]
        dff = wg_ref.shape[2]
        acc = None
        for c0 in range(0, dff, FF_CHUNK):
            g = _dot(x, wg_ref[0, :, c0:c0 + FF_CHUNK])
            a = (g * _sigmoid(g) * _dot(x, wu_ref[0, :, c0:c0 + FF_CHUNK])).astype(BF16)
            y = _dot(a, wd_ref[0, c0:c0 + FF_CHUNK, :])
            acc = y if acc is None else acc + y
        o_ref[...] = acc.astype(o_ref.dtype)

    @pl.when(pl.program_id(0) >= n_tiles_ref[0])
    def _():
        o_ref[...] = jnp.zeros_like(o_ref)


def _moe_experts(x_sorted, tile_expert, n_tiles, wg_bf, wu_bf, wd_bf):
    p, d = x_sorted.shape
    tm = ROW_TILE
    dff = wg_bf.shape[2]
    grid_spec = pltpu.PrefetchScalarGridSpec(
        num_scalar_prefetch=2,
        grid=(p // tm,),
        in_specs=[pl.BlockSpec((tm, d), lambda t, te, nt: (t, 0)),
                  pl.BlockSpec((1, d, dff), lambda t, te, nt: (te[t], 0, 0)),
                  pl.BlockSpec((1, d, dff), lambda t, te, nt: (te[t], 0, 0)),
                  pl.BlockSpec((1, dff, d), lambda t, te, nt: (te[t], 0, 0))],
        out_specs=pl.BlockSpec((tm, d), lambda t, te, nt: (t, 0)),
    )
    return pl.pallas_call(
        _moe_kernel,
        grid_spec=grid_spec,
        out_shape=jax.ShapeDtypeStruct((p, d), BF16),
        compiler_params=_cparams(("arbitrary",), V7X_VMEM_LIMIT),
        name="moe_experts",
    )(tile_expert, n_tiles, x_sorted, wg_bf, wu_bf, wd_bf)


def _route(logits, tm):
    n = logits.shape[0]
    top_v, top_i = lax.top_k(logits, TOP_K)
    wts = jax.nn.softmax(top_v, axis=-1)
    flat_e = top_i.reshape(-1)
    onehot = (flat_e[:, None] == jnp.arange(N_EXPERTS)[None, :]).astype(jnp.int32)
    rank = jnp.take_along_axis(jnp.cumsum(onehot, axis=0) - onehot, flat_e[:, None], axis=1)[:, 0]
    counts = jnp.sum(onehot, axis=0)
    padded = ((counts + tm - 1) // tm) * tm
    ends = jnp.cumsum(padded)
    starts = ends - padded
    dest = starts[flat_e] + rank
    m = TOP_K * n
    p = m + N_EXPERTS * tm
    tile_start = jnp.arange(p // tm, dtype=jnp.int32) * tm
    tile_expert = jnp.minimum(jnp.sum((tile_start[:, None] >= ends[None, :]).astype(jnp.int32), axis=1),
                              N_EXPERTS - 1).astype(jnp.int32)
    n_tiles = (ends[-1] // tm).astype(jnp.int32).reshape(1)
    by_expert = jnp.sort(flat_e * m + jnp.arange(m, dtype=jnp.int32)) % m
    first = jnp.cumsum(counts) - counts
    row_expert = jnp.repeat(tile_expert, tm)
    pos = jnp.arange(p, dtype=jnp.int32)
    entry = jnp.take(by_expert, jnp.clip(first[row_expert] + pos - starts[row_expert], 0, m - 1))
    src_token = entry // TOP_K
    return wts, dest.reshape(n, TOP_K), src_token, tile_expert, n_tiles


def _final_kernel(x_ref, y0_ref, y1_ref, w_ref, mod_ref, g_ref, o_ref, *, d):
    w = w_ref[0]
    y = w[:, 0:1] * y0_ref[0].astype(F32) + w[:, 1:2] * y1_ref[0].astype(F32)
    x = x_ref[0] + mod_ref[0][:, 5 * d:6 * d] * y
    o_ref[0] = _rms(x) * g_ref[...]


def _final(x_lat, y0, y1, wts, mod3, g_final):
    bsz, s, d = x_lat.shape
    tm = ROW_TILE
    rows = pl.BlockSpec((1, tm, d), lambda b, j: (b, j, 0))
    return pl.pallas_call(
        functools.partial(_final_kernel, d=d),
        grid=(bsz, s // tm),
        in_specs=[rows, rows, rows, pl.BlockSpec((1, tm, TOP_K), lambda b, j: (b, j, 0)),
                  pl.BlockSpec((1, 1, mod3.shape[2]), lambda b, j: (b, 0, 0)),
                  pl.BlockSpec((1, d), lambda b, j: (0, 0))],
        out_specs=rows,
        out_shape=jax.ShapeDtypeStruct((bsz, s, d), F32),
        compiler_params=_cparams(("arbitrary", "arbitrary")),
        name="moe_combine_final_norm",
    )(x_lat, y0, y1, wts, mod3, g_final.reshape(1, d))


def _rope_tables(n_lat, n_ctx):
    pairs = C_DH // 4
    tpos = jnp.arange(n_lat)
    pos = jnp.stack([(tpos // GRID_W).astype(F32), (tpos % GRID_W).astype(F32)], axis=1)
    freqs = ROPE_BASE ** (-jnp.arange(pairs, dtype=F32) / pairs)
    ang = pos[:, :, None] * freqs[None, None, :]
    cos = jnp.repeat(jnp.cos(ang)[:, :, None, :], 2, axis=2).reshape(n_lat, C_DH)
    sin = jnp.sin(ang)[:, :, None, :] * jnp.array([-1.0, 1.0], F32)[None, None, :, None]
    sin = sin.reshape(n_lat, C_DH)
    cos = jnp.concatenate([cos, jnp.ones((n_ctx, C_DH), F32)], axis=0)
    sin = jnp.concatenate([sin, jnp.zeros((n_ctx, C_DH), F32)], axis=0)
    reps = 512 // C_DH
    return jnp.tile(cos, (1, reps)), jnp.tile(sin, (1, reps))


def _lower_bound(p, layer):
    cs = jnp.cumsum(jax.nn.softmax(p.astype(F32), axis=0), axis=0)
    return cs[layer] - cs[0]


def kernel(x, c, ctx, c_ctx, w_ada, b_ada, g_norm1, g_norm2, w_in, hgrn_lb, hgrn_gnorm, mlp_vnorm, mlp_ws, mlp_bs,
           diff_lambda, diff_subln, w_branch, w_out, ffn_wg, ffn_wu, ffn_wd, moe_router, moe_wg, moe_wu, moe_wd,
           g_final):
    bsz, n_lat, d = x.shape
    n_ctx = ctx.shape[1]
    depth = w_ada.shape[0]
    t = n_lat + n_ctx
    tm = ROW_TILE
    assert depth == 2 and bsz < MOD_ROWS and n_lat % ATT_TK == 0 and n_ctx % tm == 0 and n_lat % GRID_W == 0
    n_lat_blocks = n_lat // tm

    cvec = jnp.zeros((MOD_ROWS, d), F32).at[:bsz].set(c).at[bsz].set(c_ctx)
    mods = _adaln(cvec, w_ada, b_ada)
    cos_t, sin_t = _rope_tables(n_lat, n_ctx)
    xc = jnp.concatenate([x, ctx], axis=1)

    out = None
    for layer in range(depth):
        last = layer == depth - 1
        mod3 = mods[layer].reshape(MOD_ROWS, 1, 6 * d)
        lb_f = _lower_bound(hgrn_lb[0], layer)
        lb_b = _lower_bound(hgrn_lb[1], layer)
        proj = _inproj(xc, mod3, g_norm1[layer], w_in[layer].astype(BF16), cos_t, sin_t, lb_f, lb_b, n_lat_blocks)
        of, ob = _hgrn(proj["qa"], proj["bf"], proj["kf"], proj["bb"], proj["kb"], proj["ia"], n_lat)

        lam_init = 0.8 - 0.6 * math.exp(-0.3 * layer)
        lam_p = diff_lambda[layer]
        lam = jnp.exp(jnp.sum(lam_p[0] * lam_p[1])) - jnp.exp(jnp.sum(lam_p[2] * lam_p[3])) + lam_init
        yc = _attention(proj["qc"], proj["kc"], proj["vc"], lam.astype(F32), diff_subln[layer], n_lat,
                        not last, lam_init)

        bs_full = jnp.repeat(mlp_bs[layer].T, HEAD_W, axis=1)
        j = layer // 2
        is_moe = layer % 2 == 1
        w_router_pad = jnp.zeros((d, HEAD_W), F32)
        if is_moe:
            w_router_pad = w_router_pad.at[:, :N_EXPERTS].set(moe_router[j])
        n_blocks = n_lat_blocks if last else t // tm
        merged = _merge(xc, mod3, proj, of, ob, yc, hgrn_gnorm[layer], mlp_vnorm[layer], mlp_ws[layer].astype(BF16),
                        bs_full, w_branch[layer].astype(BF16), w_out[layer].astype(BF16), g_norm2[layer],
                        w_router_pad, n_lat_blocks, n_blocks, is_moe)
        x_new, h2 = merged[0], merged[1]

        if not is_moe:
            assert not last
            xc = _ffn(x_new, h2, mod3, ffn_wg[j].astype(BF16), ffn_wu[j].astype(BF16), ffn_wd[j].astype(BF16),
                      n_lat_blocks)
        else:
            assert last
            n = bsz * n_lat
            logits = merged[2].reshape(n, HEAD_W)[:, :N_EXPERTS]
            wts, dest, src_token, tile_expert, n_tiles = _route(logits, tm)
            x_sorted = jnp.take(h2.reshape(n, d), src_token, axis=0, mode="clip")
            y = _moe_experts(x_sorted, tile_expert, n_tiles, moe_wg[j].astype(BF16), moe_wu[j].astype(BF16),
                             moe_wd[j].astype(BF16))
            y0 = jnp.take(y, dest[:, 0], axis=0, mode="clip").reshape(bsz, n_lat, d)
            y1 = jnp.take(y, dest[:, 1], axis=0, mode="clip").reshape(bsz, n_lat, d)
            out = _final(x_new, y0, y1, wts.reshape(bsz, n_lat, TOP_K), mod3, g_final)
    return out
```

```python
import functools
import math

import jax
import jax.numpy as jnp
from jax import lax
from jax.experimental import pallas as pl
from jax.experimental.pallas import tpu as pltpu

F32 = jnp.float32
BF16 = jnp.bfloat16

EPS = 1e-6
GRID_W = 64
ROPE_BASE = 10000.0

A_HEADS = 4
A_CHUNK = 64
A_SUB = 16
HGRN_ROWS = 128
B_GROUPS = 4
B_CHUNK = 128
C_HEADS = 4
C_DH = 64
HEAD_W = 128
BRANCH_W = 512
N_BRANCH = 3
N_EXPERTS = 8
TOP_K = 2

ROW_TILE = 256
MOD_ROWS = 16
ATT_TQ = 512
ATT_TK = 512
FF_CHUNK = 512
EXP_CLAMP = 80.0
LOG2_E = 1.4426950408889634
NEG_BIG = -1e30
V7X_VMEM_LIMIT = 56 * 1024 * 1024


def _cparams(sem, vmem=None):
    return pltpu.CompilerParams(dimension_semantics=sem, vmem_limit_bytes=vmem)


def _resident(shape, index_map):
    return pl.BlockSpec(shape, index_map, pipeline_mode=pl.Buffered(1))


def _lat_ctx_specs(tm, d, n_lat_blocks, ctx_block0):
    lat = pl.BlockSpec((1, tm, d), lambda b, j: (b, jnp.minimum(j, n_lat_blocks - 1), 0))
    ctx = pl.BlockSpec((1, tm, d), lambda b, j: (b, ctx_block0 + jnp.maximum(j - n_lat_blocks, 0), 0))
    return lat, ctx


def _rms(xf):
    return xf * lax.rsqrt(jnp.mean(xf * xf, axis=-1, keepdims=True) + EPS)


def _sigmoid(x):
    return 1.0 / (1.0 + jnp.exp(-x))


def _dot(a, b):
    return jnp.dot(a, b, preferred_element_type=F32)


def _dot_nt(a, b):
    return lax.dot_general(a, b, (((1,), (1,)), ((), ())), preferred_element_type=F32)


def _dot_tn(a, b):
    return lax.dot_general(a, b, (((0,), (0,)), ((), ())), preferred_element_type=F32)


def _adaln_kernel(c_ref, w_ref, b_ref, o_ref):
    c = c_ref[...]
    a = c * _sigmoid(c)
    o_ref[0] = jnp.dot(a, w_ref[0], preferred_element_type=F32, precision=lax.Precision.HIGHEST) + b_ref[0]


def _adaln(cvec, w_ada, b_ada):
    depth, d, n = w_ada.shape
    tn = 1536
    return pl.pallas_call(
        _adaln_kernel,
        grid=(depth, n // tn),
        in_specs=[pl.BlockSpec((MOD_ROWS, d), lambda l, j: (0, 0)),
                  pl.BlockSpec((1, d, tn), lambda l, j: (l, 0, j)),
                  pl.BlockSpec((1, 1, tn), lambda l, j: (l, 0, j))],
        out_specs=pl.BlockSpec((1, MOD_ROWS, tn), lambda l, j: (l, 0, j)),
        out_shape=jax.ShapeDtypeStruct((depth, MOD_ROWS, n), F32),
        compiler_params=_cparams(("arbitrary", "arbitrary")),
        name="adaln",
    )(cvec, w_ada, b_ada.reshape(depth, 1, n))


_IN_COLS = (("qa", 512), ("zf", 512), ("zb", 512), ("ia", 512), ("og", 512), ("uv", 1024), ("qc", 512), ("kc", 512),
            ("vc", 512), ("gt", 3072))
_IN_OUTS = (("qa", 512, BF16), ("bf", 512, F32), ("kf", 512, BF16), ("bb", 512, F32), ("kb", 512, BF16),
            ("ia", 512, BF16), ("og", 512, BF16), ("uv", 1024, BF16), ("qc", 512, BF16), ("kc", 512, BF16),
            ("vc", 512, BF16), ("gt", 3072, BF16))


def _forget_gate(z, lb):
    sp = jnp.maximum(-z, 0.0) + jnp.log(1.0 + jnp.exp(-jnp.abs(z)))
    la = jnp.log(lb)
    lc = jnp.log1p(-lb) - sp
    logf = jnp.maximum(la, lc) + jnp.log(1.0 + jnp.exp(-jnp.abs(la - lc)))
    kin = (1.0 - lb) * jnp.exp(-z - sp)
    hi = logf.astype(BF16)
    rest = logf - hi.astype(F32)
    mid = rest.astype(BF16)
    low = (rest - mid.astype(F32)).astype(BF16)
    return (hi, mid, low), kin


def _inproj_kernel(xl_ref, xc_ref, mod_ref, g_ref, w_ref, cos_ref, sin_ref, lbf_ref, lbb_ref, trif_ref, trib_ref,
                   *out_refs, d, n_lat_blocks):
    x = jnp.where(pl.program_id(1) < n_lat_blocks, xl_ref[0], xc_ref[0])
    mod = mod_ref[0]
    h = _rms(x) * g_ref[...]
    h = (h * (1.0 + mod[:, d:2 * d]) + mod[:, 0:d]).astype(BF16)

    tm = x.shape[0]
    lane = lax.broadcasted_iota(jnp.int32, (tm, 512), 1)
    low_half = (lane % 32) < 16

    def rope(p):
        partner = jnp.where(low_half, pltpu.roll(p, 512 - 16, 1), pltpu.roll(p, 16, 1))
        return p * cos_ref[...] + partner * sin_ref[...]

    outs = {name: ref for (name, _, _), ref in zip(_IN_OUTS, out_refs)}
    col0 = {}
    lo = 0
    for name, width in _IN_COLS:
        col0[name] = lo
        lo += width

    gates = {}
    for name, lb_ref in (("zf", lbf_ref), ("zb", lbb_ref)):
        terms, kin = _forget_gate(_dot(h, w_ref[:, col0[name]:col0[name] + 512]), lb_ref[...])
        outs["k" + name[1]][0] = kin.astype(BF16)
        gates[name] = terms
    for name, width in _IN_COLS:
        if name in gates:
            continue
        for c0 in range(0, width, 512):
            p = _dot(h, w_ref[:, col0[name] + c0:col0[name] + c0 + 512])
            if name in ("qc", "kc"):
                p = rope(p)
            outs[name][0, :, c0:c0 + 512] = p.astype(outs[name].dtype)
    for name, tri_ref in (("zf", trif_ref), ("zb", trib_ref)):
        hi, mid, low = gates[name]
        outs["b" + name[1]][0] = _dot(tri_ref[...], hi) + _dot(tri_ref[...], mid) + _dot(tri_ref[...], low)


def _scan_matrices(tm):
    row = jnp.arange(tm)[:, None]
    col = jnp.arange(tm)[None, :]
    same = (row // A_CHUNK) == (col // A_CHUNK)
    return (same & (col <= row)).astype(BF16), (same & (col >= row)).astype(BF16)


def _inproj(x_lat, x_ctx, ctx_block0, t, mod3, g, w_bf, cos_t, sin_t, lb_f, lb_b, n_lat_blocks):
    bsz, _, d = x_lat.shape
    tm = ROW_TILE
    ncols = w_bf.shape[1]
    ctx_row = bsz

    def mod_map(b, j):
        return (jnp.where(j < n_lat_blocks, b, ctx_row), 0, 0)

    out_shape = [jax.ShapeDtypeStruct((bsz, t, w), dt) for _, w, dt in _IN_OUTS]
    out_specs = [pl.BlockSpec((1, tm, w), lambda b, j: (b, j, 0)) for _, w, _ in _IN_OUTS]
    tri_f, tri_b = _scan_matrices(tm)
    vec = pl.BlockSpec((1, 512), lambda b, j: (0, 0))
    lat_spec, ctx_spec = _lat_ctx_specs(tm, d, n_lat_blocks, ctx_block0)
    outs = pl.pallas_call(
        functools.partial(_inproj_kernel, d=d, n_lat_blocks=n_lat_blocks),
        grid=(bsz, t // tm),
        in_specs=[lat_spec, ctx_spec,
                  pl.BlockSpec((1, 1, mod3.shape[2]), mod_map),
                  pl.BlockSpec((1, d), lambda b, j: (0, 0)),
                  _resident((d, ncols), lambda b, j: (0, 0)),
                  pl.BlockSpec((tm, 512), lambda b, j: (j, 0)),
                  pl.BlockSpec((tm, 512), lambda b, j: (j, 0)),
                  vec, vec, _resident((tm, tm), lambda b, j: (0, 0)), _resident((tm, tm), lambda b, j: (0, 0))],
        out_specs=out_specs,
        out_shape=out_shape,
        compiler_params=_cparams(("arbitrary", "arbitrary"), V7X_VMEM_LIMIT),
        name="inproj",
    )(x_lat, x_ctx, mod3, g.reshape(1, d), w_bf, cos_t, sin_t, lb_f.reshape(1, 512), lb_b.reshape(1, 512), tri_f, tri_b)
    return {name: o for (name, _, _), o in zip(_IN_OUTS, outs)}


def _hgrn_direction(q_ref, b_ref, k_ref, v_ref, o_ref, st_ref, reverse, rows):
    c = A_CHUNK
    row = lax.broadcasted_iota(jnp.int32, (c, c), 0)
    col = lax.broadcasted_iota(jnp.int32, (c, c), 1)
    allowed = (col >= row) if reverse else (col <= row)

    b = b_ref[0, rows, :]
    tot_row = 0 if reverse else c - 1
    tot = b[tot_row:tot_row + 1, :]

    q = q_ref[0, rows, :].astype(F32)
    kin = k_ref[0, rows, :].astype(F32)
    v = v_ref[0, rows, :]
    zero_blk = jnp.zeros((A_SUB, HEAD_W), F32)
    heads = []
    for h in range(A_HEADS):
        hs = slice(h * HEAD_W, (h + 1) * HEAD_W)
        bh, qh, kh = b[:, hs], q[:, hs], kin[:, hs]
        q_state = qh * jnp.exp(bh)
        k_state = kh * jnp.exp(tot[:, hs] - bh)
        q_rows, k_cols = [], []
        for i in range(c // A_SUB):
            r0 = i * A_SUB
            ref = r0 + A_SUB - 1 if reverse else r0
            r = bh[ref:ref + 1, :]
            qt = qh[r0:r0 + A_SUB] * jnp.exp(bh[r0:r0 + A_SUB] - r)
            q_rows.append(jnp.concatenate([qt if j == i else zero_blk for j in range(c // A_SUB)], axis=1))
            k_cols.append(kh * jnp.exp(jnp.minimum(r - bh, EXP_CLAMP)))
        heads.append(dict(
            hs=hs, h=h, rows=rows, allowed=allowed, vh=v[:, hs], decay=jnp.exp(tot[:, hs]), o_ref=o_ref, st_ref=st_ref,
            q_wide=jnp.concatenate(q_rows, axis=0).astype(BF16),
            k_wide=jnp.concatenate(k_cols, axis=1).astype(BF16),
            q_state=q_state.astype(BF16), k_state=k_state.astype(BF16)))
    return heads


def _hgrn_kernel(qf_ref, bf_ref, kf_ref, vf_ref, qb_ref, bb_ref, kb_ref, vb_ref, of_ref, ob_ref, sf_ref, sb_ref):
    @pl.when(pl.program_id(1) == 0)
    def _():
        sf_ref[...] = jnp.zeros_like(sf_ref)
        sb_ref[...] = jnp.zeros_like(sb_ref)

    n_sub = qf_ref.shape[1] // A_CHUNK
    seq = []
    for i in range(n_sub):
        up = slice(i * A_CHUNK, (i + 1) * A_CHUNK)
        down = slice((n_sub - 1 - i) * A_CHUNK, (n_sub - i) * A_CHUNK)
        seq.append(_hgrn_direction(qf_ref, bf_ref, kf_ref, vf_ref, of_ref, sf_ref, reverse=False, rows=up)
                   + _hgrn_direction(qb_ref, bb_ref, kb_ref, vb_ref, ob_ref, sb_ref, reverse=True, rows=down))
    for chains in seq:
        for ch in chains:
            ch["att"] = _dot_nt(ch["q_wide"], ch["k_wide"])
    state = [ch["st_ref"][ch["h"]] for ch in seq[0]]
    for chains in seq:
        for ch, st in zip(chains, state):
            ch["inter"] = _dot_nt(ch["q_state"], st.astype(BF16))
            ch["upd"] = _dot_tn(ch["vh"], ch["k_state"])
        for n, ch in enumerate(chains):
            att = jnp.where(ch["allowed"], ch["att"], 0.0).astype(BF16)
            o = _dot(att, ch["vh"]) + ch["inter"]
            ch["o_ref"][0, ch["rows"], ch["hs"]] = o.astype(ch["o_ref"].dtype)
            state[n] = state[n] * ch["decay"] + ch["upd"]
    for ch, st in zip(seq[0], state):
        ch["st_ref"][ch["h"]] = st


def _hgrn(qa, bf, kf, bb, kb, ia, n_lat):
    bsz, t, w = qa.shape
    c = HGRN_ROWS
    n_chunks = t // c
    lat_chunks = n_lat // c

    def fwd(b, s):
        return (b, (s + lat_chunks) % n_chunks, 0)

    def bwd(b, s):
        return (b, n_chunks - 1 - s, 0)

    blk = (1, c, w)
    return pl.pallas_call(
        _hgrn_kernel,
        grid=(bsz, n_chunks),
        in_specs=[pl.BlockSpec(blk, fwd)] * 4 + [pl.BlockSpec(blk, bwd)] * 4,
        out_specs=[pl.BlockSpec(blk, fwd), pl.BlockSpec(blk, bwd)],
        out_shape=[jax.ShapeDtypeStruct((bsz, t, w), BF16)] * 2,
        scratch_shapes=[pltpu.VMEM((A_HEADS, HEAD_W, HEAD_W), F32)] * 2,
        compiler_params=_cparams(("arbitrary", "arbitrary")),
        name="hgrn2",
    )(qa, bf, kf, ia, qa, bb, kb, ia)


def _attn_kernel(q_ref, k_ref, v_ref, lam_ref, sub_ref, o_ref, v1_ref, m_ref, acc_ref, *, key_blocks, lam_init):
    tq = q_ref.shape[1]

    @pl.when(pl.program_id(2) == 0)
    def _():
        v1_ref[:, :HEAD_W] = v_ref[0]
        v1_ref[:, HEAD_W:] = jnp.ones((v1_ref.shape[0], HEAD_W), BF16)

    lane = lax.broadcasted_iota(jnp.int32, (tq, HEAD_W), 1)
    qf = q_ref[0].astype(F32) * (C_DH ** -0.5 * LOG2_E)
    q2 = jnp.concatenate([jnp.where(lane < C_DH, qf, 0.0), jnp.where(lane >= C_DH, qf, 0.0)], axis=0).astype(BF16)

    m_ref[...] = jnp.full_like(m_ref, NEG_BIG)
    acc_ref[...] = jnp.zeros_like(acc_ref)

    def scores(start, size):
        return _dot_nt(q2, k_ref[0, pl.ds(start, size), :])

    def absorb(s, start, size):
        m_old = m_ref[...]
        m_new = jnp.maximum(m_old, jnp.max(s, axis=-1, keepdims=True))
        alpha = jnp.exp2(m_old - m_new)
        p = jnp.exp2(s - jnp.concatenate([m_new] * (size // HEAD_W), axis=1))
        acc_ref[...] = (jnp.concatenate([alpha, alpha], axis=1) * acc_ref[...]
                        + _dot(p.astype(BF16), v1_ref[pl.ds(start, size), :]))
        m_ref[...] = m_new

    cur = scores(*key_blocks[0])
    for n, blk in enumerate(key_blocks):
        nxt = scores(*key_blocks[n + 1]) if n + 1 < len(key_blocks) else None
        absorb(cur, *blk)
        cur = nxt

    o12 = acc_ref[:, :HEAD_W] / acc_ref[:, HEAD_W:]
    o = o12[:tq] - lam_ref[...] * o12[tq:]
    o_ref[0] = (_rms(o) * sub_ref[...] * (1.0 - lam_init)).astype(o_ref.dtype)


def _attention(qc, kc, vc, lam, subln, lam_init, tq, q_block0, n_q, kv_block, key_blocks):
    bsz, _, w = qc.shape
    kv_rows, kv_idx = kv_block
    return pl.pallas_call(
        functools.partial(_attn_kernel, key_blocks=key_blocks, lam_init=lam_init),
        grid=(bsz, C_HEADS, n_q),
        in_specs=[pl.BlockSpec((1, tq, HEAD_W), lambda b, h, i: (b, q_block0 + i, h)),
                  pl.BlockSpec((1, kv_rows, HEAD_W), lambda b, h, i: (b, kv_idx, h)),
                  pl.BlockSpec((1, kv_rows, HEAD_W), lambda b, h, i: (b, kv_idx, h)),
                  pl.BlockSpec((1, HEAD_W), lambda b, h, i: (0, 0)),
                  pl.BlockSpec((1, HEAD_W), lambda b, h, i: (0, h))],
        out_specs=pl.BlockSpec((1, tq, HEAD_W), lambda b, h, i: (b, i, h)),
        out_shape=jax.ShapeDtypeStruct((bsz, n_q * tq, w), BF16),
        scratch_shapes=[pltpu.VMEM((kv_rows, 2 * HEAD_W), BF16), pltpu.VMEM((2 * tq, HEAD_W), F32),
                        pltpu.VMEM((2 * tq, 2 * HEAD_W), F32)],
        compiler_params=_cparams(("arbitrary", "arbitrary", "arbitrary")),
        name="diff_attention",
    )(qc, kc, vc, jnp.full((1, HEAD_W), lam, F32), subln.reshape(1, w))


def _gelu_tanh(x):
    return 0.5 * x * (1.0 + jnp.tanh(math.sqrt(2.0 / math.pi) * (x + 0.044715 * (x * x * x))))


def _merge_kernel(xl_ref, xc_ref, mod_ref, of_ref, ob_ref, og_ref, gn_ref, uv_ref, vn_ref, ws_ref, bs_ref, ycl_ref,
                  ycc_ref, gt_ref, wb_ref, wo_ref, g2_ref, wr_ref, x_out_ref, h_out_ref, *maybe_logits_ref, d, n_lat_blocks):
    tm = xl_ref.shape[1]
    is_lat = pl.program_id(1) < n_lat_blocks
    mod = mod_ref[0]
    groups = [slice(r0, r0 + B_CHUNK) for r0 in range(0, tm, B_CHUNK)]
    st = [dict() for _ in groups]

    for s, rows in zip(st, groups):
        o = of_ref[0, rows, :].astype(F32) + ob_ref[0, rows, :].astype(F32)
        o = jnp.concatenate([_rms(o[:, h * HEAD_W:(h + 1) * HEAD_W]) for h in range(A_HEADS)], axis=1)
        og = og_ref[0, rows, :].astype(F32)
        s["ya"] = (o * gn_ref[...] * (og * _sigmoid(og))).astype(BF16)
        uv = _gelu_tanh(uv_ref[0, rows, :].astype(F32))
        s["u"] = uv[:, :BRANCH_W]
        s["vv"] = (_rms(uv[:, BRANCH_W:]) * vn_ref[...]).astype(BF16)
    for s in st:
        s["mixed"] = jnp.concatenate(
            [_dot(ws_ref[g], s["vv"][:, g * HEAD_W:(g + 1) * HEAD_W]) for g in range(B_GROUPS)], axis=1)
    for s, rows in zip(st, groups):
        yb = (s["u"] * (s["mixed"] + bs_ref[...])).astype(BF16)
        yc = jnp.where(is_lat, ycl_ref[0, rows, :], ycc_ref[0, rows, :])
        s["proj"] = [_dot(y, wb_ref[i]) for i, y in enumerate((s["ya"], yb, yc))]
    for s, rows in zip(st, groups):
        merged = None
        for i in range(N_BRANCH):
            term = _sigmoid(gt_ref[0, rows, i * d:(i + 1) * d].astype(F32)) * s["proj"][i]
            merged = term if merged is None else merged + term
        s["mix"] = _dot(merged.astype(BF16), wo_ref[...])
    for s, rows in zip(st, groups):
        x_old = jnp.where(is_lat, xl_ref[0, rows, :], xc_ref[0, rows, :])
        x_new = x_old + mod[:, 2 * d:3 * d] * s["mix"]
        x_out_ref[0, rows, :] = x_new
        h = _rms(x_new) * g2_ref[...]
        h = h * (1.0 + mod[:, 4 * d:5 * d]) + mod[:, 3 * d:4 * d]
        h_hi = h.astype(BF16)
        h_out_ref[0, rows, :] = h_hi
        if maybe_logits_ref:
            h_lo = (h - h_hi.astype(F32)).astype(BF16)
            maybe_logits_ref[0][0, rows, :] = (_dot(h_hi, wr_ref[0]) + _dot(h_lo, wr_ref[0])) + _dot(h_hi, wr_ref[1])


def _merge(x_lat, x_ctx, ctx_block0, mod3, proj, of, ob, yc_lat, yc_ctx, gnorm, vnorm, ws_bf, bs_full, wb_bf, wo_bf, g2, wr_split,
           n_lat_blocks, n_blocks, with_logits):
    bsz, _, d = x_lat.shape
    tm = ROW_TILE
    ctx_row = bsz

    def rows(w):
        return pl.BlockSpec((1, tm, w), lambda b, j: (b, j, 0))

    def const2(shape):
        return pl.BlockSpec(shape, lambda b, j: (0, 0))

    def mod_map(b, j):
        return (jnp.where(j < n_lat_blocks, b, ctx_row), 0, 0)

    n_rows = n_blocks * tm
    out_shape = [jax.ShapeDtypeStruct((bsz, n_rows, d), F32), jax.ShapeDtypeStruct((bsz, n_rows, d), BF16)]
    out_specs = [rows(d), rows(d)]
    if with_logits:
        out_shape.append(jax.ShapeDtypeStruct((bsz, n_rows, HEAD_W), F32))
        out_specs.append(rows(HEAD_W))
    lat_spec, ctx_spec = _lat_ctx_specs(tm, d, n_lat_blocks, ctx_block0)
    return pl.pallas_call(
        functools.partial(_merge_kernel, d=d, n_lat_blocks=n_lat_blocks),
        grid=(bsz, n_blocks),
        in_specs=[lat_spec, ctx_spec, pl.BlockSpec((1, 1, mod3.shape[2]), mod_map),
                  rows(BRANCH_W), rows(BRANCH_W), rows(BRANCH_W), const2((1, BRANCH_W)),
                  rows(2 * BRANCH_W), const2((1, BRANCH_W)),
                  _resident((B_GROUPS, B_CHUNK, B_CHUNK), lambda b, j: (0, 0, 0)), const2((B_CHUNK, BRANCH_W)),
                  *_lat_ctx_specs(tm, BRANCH_W, n_lat_blocks, 0), rows(N_BRANCH * d),
                  _resident((N_BRANCH, BRANCH_W, d), lambda b, j: (0, 0, 0)), _resident((d, d), lambda b, j: (0, 0)),
                  const2((1, d)), _resident((2, d, HEAD_W), lambda b, j: (0, 0, 0))],
        out_specs=out_specs,
        out_shape=out_shape,
        compiler_params=_cparams(("arbitrary", "arbitrary"), V7X_VMEM_LIMIT),
        name="merge",
    )(x_lat, x_ctx, mod3, of, ob, proj["og"], gnorm.reshape(1, BRANCH_W), proj["uv"], vnorm.reshape(1, BRANCH_W),
      ws_bf, bs_full, yc_lat, yc_ctx, proj["gt"], wb_bf, wo_bf, g2.reshape(1, d), wr_split)


def _ffn_kernel(x_ref, h_ref, mod_ref, wg_ref, wu_ref, wd_ref, o_ref, *, d):
    h = h_ref[0]
    g = _dot(h, wg_ref[...])
    a = (g * _sigmoid(g) * _dot(h, wu_ref[...])).astype(BF16)
    y = _dot(a, wd_ref[...])
    o_ref[0] = x_ref[0] + mod_ref[0][:, 5 * d:6 * d] * y


def _ffn(xc, h2, mod3, wg_bf, wu_bf, wd_bf, n_lat_blocks):
    bsz, t, d = xc.shape
    tm = ROW_TILE
    dff = wg_bf.shape[1]
    ctx_row = bsz

    def mod_map(b, j):
        return (jnp.where(j < n_lat_blocks, b, ctx_row), 0, 0)

    rows = pl.BlockSpec((1, tm, d), lambda b, j: (b, j, 0))
    return pl.pallas_call(
        functools.partial(_ffn_kernel, d=d),
        grid=(bsz, t // tm),
        in_specs=[rows, rows, pl.BlockSpec((1, 1, mod3.shape[2]), mod_map),
                  _resident((d, dff), lambda b, j: (0, 0)), _resident((d, dff), lambda b, j: (0, 0)),
                  _resident((dff, d), lambda b, j: (0, 0))],
        out_specs=rows,
        out_shape=jax.ShapeDtypeStruct((bsz, t, d), F32),
        compiler_params=_cparams(("arbitrary", "arbitrary"), V7X_VMEM_LIMIT),
        name="dense_ffn",
    )(xc, h2, mod3, wg_bf, wu_bf, wd_bf)


def _moe_kernel(tile_expert_ref, n_tiles_ref, x_ref, wg_ref, wu_ref, wd_ref, o_ref):
    del tile_expert_ref

    @pl.when(pl.program_id(0) < n_tiles_ref[0])
    def _():
        x = x_ref[...]
        dff = wg_ref.shape[2]
        acc = None
        for c0 in range(0, dff, FF_CHUNK):
            g = _dot(x, wg_ref[0, :, c0:c0 + FF_CHUNK])
            a = (g * _sigmoid(g) * _dot(x, wu_ref[0, :, c0:c0 + FF_CHUNK])).astype(BF16)
            y = _dot(a, wd_ref[0, c0:c0 + FF_CHUNK, :])
            acc = y if acc is None else acc + y
        o_ref[...] = acc.astype(o_ref.dtype)

    @pl.when(pl.program_id(0) >= n_tiles_ref[0])
    def _():
        o_ref[...] = jnp.zeros_like(o_ref)


def _moe_experts(x_sorted, tile_expert, n_tiles, wg_bf, wu_bf, wd_bf):
    p, d = x_sorted.shape
    tm = ROW_TILE
    dff = wg_bf.shape[2]
    grid_spec = pltpu.PrefetchScalarGridSpec(
        num_scalar_prefetch=2,
        grid=(p // tm,),
        in_specs=[pl.BlockSpec((tm, d), lambda t, te, nt: (t, 0)),
                  pl.BlockSpec((1, d, dff), lambda t, te, nt: (te[t], 0, 0)),
                  pl.BlockSpec((1, d, dff), lambda t, te, nt: (te[t], 0, 0)),
                  pl.BlockSpec((1, dff, d), lambda t, te, nt: (te[t], 0, 0))],
        out_specs=pl.BlockSpec((tm, d), lambda t, te, nt: (t, 0)),
    )
    return pl.pallas_call(
        _moe_kernel,
        grid_spec=grid_spec,
        out_shape=jax.ShapeDtypeStruct((p, d), BF16),
        compiler_params=_cparams(("arbitrary",), V7X_VMEM_LIMIT),
        name="moe_experts",
    )(tile_expert, n_tiles, x_sorted, wg_bf, wu_bf, wd_bf)


def _route(logits, tm):
    n = logits.shape[0]
    _, top_i = lax.top_k(logits, TOP_K)
    flat_e = top_i.reshape(-1)
    onehot = (flat_e[:, None] == jnp.arange(N_EXPERTS)[None, :]).astype(jnp.int32)
    rank = jnp.take_along_axis(jnp.cumsum(onehot, axis=0) - onehot, flat_e[:, None], axis=1)[:, 0]
    counts = jnp.sum(onehot, axis=0)
    padded = ((counts + tm - 1) // tm) * tm
    ends = jnp.cumsum(padded)
    starts = ends - padded
    dest = starts[flat_e] + rank
    m = TOP_K * n
    p = m + N_EXPERTS * tm
    tile_start = jnp.arange(p // tm, dtype=jnp.int32) * tm
    tile_expert = jnp.minimum(jnp.sum((tile_start[:, None] >= ends[None, :]).astype(jnp.int32), axis=1),
                              N_EXPERTS - 1).astype(jnp.int32)
    n_tiles = (ends[-1] // tm).astype(jnp.int32).reshape(1)
    by_expert = jnp.sort(flat_e * m + jnp.arange(m, dtype=jnp.int32)) % m
    first = jnp.cumsum(counts) - counts
    row_expert = jnp.repeat(tile_expert, tm)
    pos = jnp.arange(p, dtype=jnp.int32)
    entry = jnp.take(by_expert, jnp.clip(first[row_expert] + pos - starts[row_expert], 0, m - 1))
    src_token = entry // TOP_K
    return dest.reshape(n, TOP_K), src_token, tile_expert, n_tiles


def _final_kernel(x_ref, y0_ref, y1_ref, lg_ref, mod_ref, g_ref, o_ref, *, d):
    lane = lax.broadcasted_iota(jnp.int32, lg_ref.shape[1:], 1)
    lg = jnp.where(lane < N_EXPERTS, lg_ref[0], NEG_BIG)
    m1 = jnp.max(lg, axis=-1, keepdims=True)
    first = jnp.min(jnp.where(lg == m1, lane, HEAD_W), axis=-1, keepdims=True)
    m2 = jnp.max(jnp.where(lane == first, NEG_BIG, lg), axis=-1, keepdims=True)
    e = jnp.exp(m2 - m1)
    w1 = 1.0 / (1.0 + e)
    y = w1 * y0_ref[0].astype(F32) + (e * w1) * y1_ref[0].astype(F32)
    x = x_ref[0] + mod_ref[0][:, 5 * d:6 * d] * y
    o_ref[0] = _rms(x) * g_ref[...]


def _final(x_lat, y0, y1, logits, mod3, g_final):
    bsz, s, d = x_lat.shape
    tm = ROW_TILE
    rows = pl.BlockSpec((1, tm, d), lambda b, j: (b, j, 0))
    return pl.pallas_call(
        functools.partial(_final_kernel, d=d),
        grid=(bsz, s // tm),
        in_specs=[rows, rows, rows, pl.BlockSpec((1, tm, HEAD_W), lambda b, j: (b, j, 0)),
                  pl.BlockSpec((1, 1, mod3.shape[2]), lambda b, j: (b, 0, 0)),
                  pl.BlockSpec((1, d), lambda b, j: (0, 0))],
        out_specs=rows,
        out_shape=jax.ShapeDtypeStruct((bsz, s, d), F32),
        compiler_params=_cparams(("arbitrary", "arbitrary")),
        name="moe_combine_final_norm",
    )(x_lat, y0, y1, logits, mod3, g_final.reshape(1, d))


def _rope_tables(n_lat, n_ctx):
    pairs = C_DH // 4
    tpos = jnp.arange(n_lat)
    pos = jnp.stack([(tpos // GRID_W).astype(F32), (tpos % GRID_W).astype(F32)], axis=1)
    freqs = ROPE_BASE ** (-jnp.arange(pairs, dtype=F32) / pairs)
    ang = pos[:, :, None] * freqs[None, None, :]
    cos = jnp.repeat(jnp.cos(ang)[:, :, None, :], 2, axis=2).reshape(n_lat, C_DH)
    sin = jnp.sin(ang)[:, :, None, :] * jnp.array([-1.0, 1.0], F32)[None, None, :, None]
    sin = sin.reshape(n_lat, C_DH)
    cos = jnp.concatenate([cos, jnp.ones((n_ctx, C_DH), F32)], axis=0)
    sin = jnp.concatenate([sin, jnp.zeros((n_ctx, C_DH), F32)], axis=0)
    reps = 512 // C_DH
    return jnp.tile(cos, (1, reps)), jnp.tile(sin, (1, reps))


def _lower_bound(p, layer):
    cs = jnp.cumsum(jax.nn.softmax(p.astype(F32), axis=0), axis=0)
    return cs[layer] - cs[0]


def kernel(x, c, ctx, c_ctx, w_ada, b_ada, g_norm1, g_norm2, w_in, hgrn_lb, hgrn_gnorm, mlp_vnorm, mlp_ws, mlp_bs,
           diff_lambda, diff_subln, w_branch, w_out, ffn_wg, ffn_wu, ffn_wd, moe_router, moe_wg, moe_wu, moe_wd,
           g_final):
    bsz, n_lat, d = x.shape
    n_ctx = ctx.shape[1]
    depth = w_ada.shape[0]
    t = n_lat + n_ctx
    tm = ROW_TILE
    assert depth == 2 and bsz < MOD_ROWS and n_lat % GRID_W == 0
    assert n_lat % ATT_TQ == 0 and n_lat % ATT_TK == 0 and n_lat % n_ctx == 0 and n_ctx % tm == 0
    n_lat_blocks = n_lat // tm

    cvec = jnp.zeros((MOD_ROWS, d), F32).at[:bsz].set(c).at[bsz].set(c_ctx)
    mods = _adaln(cvec, w_ada, b_ada)
    cos_t, sin_t = _rope_tables(n_lat, n_ctx)
    x_lat, x_ctx, ctx_block0 = x, ctx, 0

    out = None
    for layer in range(depth):
        last = layer == depth - 1
        mod3 = mods[layer].reshape(MOD_ROWS, 1, 6 * d)
        lb_f = _lower_bound(hgrn_lb[0], layer)
        lb_b = _lower_bound(hgrn_lb[1], layer)
        proj = _inproj(x_lat, x_ctx, ctx_block0, t, mod3, g_norm1[layer], w_in[layer].astype(BF16), cos_t, sin_t,
                       lb_f, lb_b, n_lat_blocks)
        of, ob = _hgrn(proj["qa"], proj["bf"], proj["kf"], proj["bb"], proj["kb"], proj["ia"], n_lat)

        lam_init = 0.8 - 0.6 * math.exp(-0.3 * layer)
        lam_p = diff_lambda[layer]
        lam = jnp.exp(jnp.sum(lam_p[0] * lam_p[1])) - jnp.exp(jnp.sum(lam_p[2] * lam_p[3])) + lam_init
        lat_keys = tuple((i * ATT_TK, ATT_TK) for i in range(n_lat // ATT_TK)) + ((n_lat, n_ctx),)
        qkv = (proj["qc"], proj["kc"], proj["vc"], lam.astype(F32), diff_subln[layer], lam_init)
        yc_lat = _attention(*qkv, ATT_TQ, 0, n_lat // ATT_TQ, (t, 0), lat_keys)
        yc_ctx = yc_lat if last else _attention(*qkv, n_ctx, n_lat // n_ctx, 1, (n_ctx, n_lat // n_ctx),
                                                ((0, n_ctx),))

        bs_full = jnp.repeat(mlp_bs[layer].T, HEAD_W, axis=1)
        j = layer // 2
        is_moe = layer % 2 == 1
        wr_split = jnp.zeros((2, d, HEAD_W), BF16)
        if is_moe:
            wr_hi = moe_router[j].astype(BF16)
            wr_lo = (moe_router[j] - wr_hi.astype(F32)).astype(BF16)
            wr_split = wr_split.at[:, :, :N_EXPERTS].set(jnp.stack([wr_hi, wr_lo]))
        n_blocks = n_lat_blocks if last else t // tm
        merged = _merge(x_lat, x_ctx, ctx_block0, mod3, proj, of, ob, yc_lat, yc_ctx, hgrn_gnorm[layer], mlp_vnorm[layer],
                        mlp_ws[layer].astype(BF16), bs_full, w_branch[layer].astype(BF16), w_out[layer].astype(BF16),
                        g_norm2[layer], wr_split, n_lat_blocks, n_blocks, is_moe)
        x_new, h2 = merged[0], merged[1]

        if not is_moe:
            assert not last
            xc = _ffn(x_new, h2, mod3, ffn_wg[j].astype(BF16), ffn_wu[j].astype(BF16), ffn_wd[j].astype(BF16),
                      n_lat_blocks)
            x_lat, x_ctx, ctx_block0 = xc, xc, n_lat_blocks
        else:
            assert last
            n = bsz * n_lat
            logits = merged[2]
            dest, src_token, tile_expert, n_tiles = _route(logits.reshape(n, HEAD_W)[:, :N_EXPERTS], tm)
            x_sorted = jnp.take(h2.reshape(n, d), src_token, axis=0, mode="clip")
            y = _moe_experts(x_sorted, tile_expert, n_tiles, moe_wg[j].astype(BF16), moe_wu[j].astype(BF16),
                             moe_wd[j].astype(BF16))
            y0 = jnp.take(y, dest[:, 0], axis=0, mode="clip").reshape(bsz, n_lat, d)
            y1 = jnp.take(y, dest[:, 1], axis=0, mode="clip").reshape(bsz, n_lat, d)
            out = _final(x_new, y0, y1, logits, mod3, g_final)
    return out
```

```python
import functools
import math

import jax
import jax.numpy as jnp
from jax import lax
from jax.experimental import pallas as pl
from jax.experimental.pallas import tpu as pltpu

F32 = jnp.float32
BF16 = jnp.bfloat16

EPS = 1e-6
GRID_W = 64
ROPE_BASE = 10000.0

A_HEADS = 4
A_CHUNK = 64
A_SUB = 16
HGRN_ROWS = 256
MOE_TILE = 512
B_GROUPS = 4
B_CHUNK = 128
C_HEADS = 4
C_DH = 64
HEAD_W = 128
BRANCH_W = 512
N_BRANCH = 3
N_EXPERTS = 8
TOP_K = 2

ROW_TILE = 256
MOD_ROWS = 16
ATT_TQ = 512
ATT_TK = 512
FF_CHUNK = 512
EXP_CLAMP = 80.0
LOG2_E = 1.4426950408889634
NEG_BIG = -1e30
V7X_VMEM_LIMIT = 56 * 1024 * 1024


def _cparams(sem, vmem=None):
    return pltpu.CompilerParams(dimension_semantics=sem, vmem_limit_bytes=vmem)


def _resident(shape, index_map):
    return pl.BlockSpec(shape, index_map, pipeline_mode=pl.Buffered(1))


def _lat_ctx_specs(tm, d, n_lat_blocks, ctx_block0):
    lat = pl.BlockSpec((1, tm, d), lambda b, j: (b, jnp.minimum(j, n_lat_blocks - 1), 0))
    ctx = pl.BlockSpec((1, tm, d), lambda b, j: (b, ctx_block0 + jnp.maximum(j - n_lat_blocks, 0), 0))
    return lat, ctx


def _rms(xf):
    return xf * lax.rsqrt(jnp.mean(xf * xf, axis=-1, keepdims=True) + EPS)


def _sigmoid(x):
    return 1.0 / (1.0 + jnp.exp(-x))


def _dot(a, b):
    return jnp.dot(a, b, preferred_element_type=F32)


def _dot_nt(a, b):
    return lax.dot_general(a, b, (((1,), (1,)), ((), ())), preferred_element_type=F32)


def _dot_tn(a, b):
    return lax.dot_general(a, b, (((0,), (0,)), ((), ())), preferred_element_type=F32)


def _adaln_kernel(c_ref, w_ref, b_ref, o_ref):
    c = c_ref[...]
    a = c * _sigmoid(c)
    o_ref[0] = jnp.dot(a, w_ref[0], preferred_element_type=F32, precision=lax.Precision.HIGHEST) + b_ref[0]


def _adaln(cvec, w_ada, b_ada):
    depth, d, n = w_ada.shape
    tn = 1536
    return pl.pallas_call(
        _adaln_kernel,
        grid=(depth, n // tn),
        in_specs=[pl.BlockSpec((MOD_ROWS, d), lambda l, j: (0, 0)),
                  pl.BlockSpec((1, d, tn), lambda l, j: (l, 0, j)),
                  pl.BlockSpec((1, 1, tn), lambda l, j: (l, 0, j))],
        out_specs=pl.BlockSpec((1, MOD_ROWS, tn), lambda l, j: (l, 0, j)),
        out_shape=jax.ShapeDtypeStruct((depth, MOD_ROWS, n), F32),
        compiler_params=_cparams(("arbitrary", "arbitrary")),
        name="adaln",
    )(cvec, w_ada, b_ada.reshape(depth, 1, n))


_IN_COLS = (("qa", 512), ("zf", 512), ("zb", 512), ("ia", 512), ("og", 512), ("uv", 1024), ("qc", 512), ("kc", 512),
            ("vc", 512), ("gt", 3072))
_IN_OUTS = (("qa", 512, BF16), ("bf", 512, F32), ("kf", 512, BF16), ("bb", 512, F32), ("kb", 512, BF16),
            ("ia", 512, BF16), ("og", 512, BF16), ("uv", 1024, BF16), ("qc", 512, BF16), ("kc", 512, BF16),
            ("vc", 512, BF16), ("gt", 3072, BF16))


def _gelu_tanh(x):
    return 0.5 * x * (1.0 + jnp.tanh(math.sqrt(2.0 / math.pi) * (x + 0.044715 * (x * x * x))))


_IN_ACTIVATIONS = {"og": lambda p: p * _sigmoid(p), "uv": _gelu_tanh, "gt": _sigmoid}


def _forget_gate(z, lb):
    sp = jnp.maximum(-z, 0.0) + jnp.log(1.0 + jnp.exp(-jnp.abs(z)))
    la = jnp.log(lb)
    lc = jnp.log1p(-lb) - sp
    logf = jnp.maximum(la, lc) + jnp.log(1.0 + jnp.exp(-jnp.abs(la - lc)))
    kin = (1.0 - lb) * jnp.exp(-z - sp)
    hi = logf.astype(BF16)
    rest = logf - hi.astype(F32)
    mid = rest.astype(BF16)
    low = (rest - mid.astype(F32)).astype(BF16)
    return (hi, mid, low), kin


def _inproj_kernel(xl_ref, xc_ref, mod_ref, g_ref, w_ref, cos_ref, sin_ref, lbf_ref, lbb_ref, trif_ref, trib_ref,
                   *out_refs, d, n_lat_blocks):
    x = jnp.where(pl.program_id(1) < n_lat_blocks, xl_ref[0], xc_ref[0])
    mod = mod_ref[0]
    h = _rms(x) * g_ref[...]
    h = (h * (1.0 + mod[:, d:2 * d]) + mod[:, 0:d]).astype(BF16)

    tm = x.shape[0]
    lane = lax.broadcasted_iota(jnp.int32, (tm, 512), 1)
    low_half = (lane % 32) < 16

    def rope(p):
        partner = jnp.where(low_half, pltpu.roll(p, 512 - 16, 1), pltpu.roll(p, 16, 1))
        return p * cos_ref[...] + partner * sin_ref[...]

    outs = {name: ref for (name, _, _), ref in zip(_IN_OUTS, out_refs)}
    col0 = {}
    lo = 0
    for name, width in _IN_COLS:
        col0[name] = lo
        lo += width

    gates = {}
    for name, lb_ref in (("zf", lbf_ref), ("zb", lbb_ref)):
        terms, kin = _forget_gate(_dot(h, w_ref[:, col0[name]:col0[name] + 512]), lb_ref[...])
        outs["k" + name[1]][0] = kin.astype(BF16)
        gates[name] = terms
    for name, width in _IN_COLS:
        if name in gates:
            continue
        for c0 in range(0, width, 512):
            p = _dot(h, w_ref[:, col0[name] + c0:col0[name] + c0 + 512])
            if name in ("qc", "kc"):
                p = rope(p)
            elif name in _IN_ACTIVATIONS:
                p = _IN_ACTIVATIONS[name](p)
            outs[name][0, :, c0:c0 + 512] = p.astype(outs[name].dtype)
    for name, tri_ref in (("zf", trif_ref), ("zb", trib_ref)):
        hi, mid, low = gates[name]
        outs["b" + name[1]][0] = _dot(tri_ref[...], hi) + _dot(tri_ref[...], mid) + _dot(tri_ref[...], low)


def _scan_matrices(tm):
    row = jnp.arange(tm)[:, None]
    col = jnp.arange(tm)[None, :]
    same = (row // A_CHUNK) == (col // A_CHUNK)
    return (same & (col <= row)).astype(BF16), (same & (col >= row)).astype(BF16)


def _inproj(x_lat, x_ctx, ctx_block0, t, mod3, g, w_bf, cos_t, sin_t, lb_f, lb_b, n_lat_blocks):
    bsz, _, d = x_lat.shape
    tm = ROW_TILE
    ncols = w_bf.shape[1]
    ctx_row = bsz

    def mod_map(b, j):
        return (jnp.where(j < n_lat_blocks, b, ctx_row), 0, 0)

    out_shape = [jax.ShapeDtypeStruct((bsz, t, w), dt) for _, w, dt in _IN_OUTS]
    out_specs = [pl.BlockSpec((1, tm, w), lambda b, j: (b, j, 0)) for _, w, _ in _IN_OUTS]
    tri_f, tri_b = _scan_matrices(tm)
    vec = pl.BlockSpec((1, 512), lambda b, j: (0, 0))
    lat_spec, ctx_spec = _lat_ctx_specs(tm, d, n_lat_blocks, ctx_block0)
    outs = pl.pallas_call(
        functools.partial(_inproj_kernel, d=d, n_lat_blocks=n_lat_blocks),
        grid=(bsz, t // tm),
        in_specs=[lat_spec, ctx_spec,
                  pl.BlockSpec((1, 1, mod3.shape[2]), mod_map),
                  pl.BlockSpec((1, d), lambda b, j: (0, 0)),
                  _resident((d, ncols), lambda b, j: (0, 0)),
                  pl.BlockSpec((tm, 512), lambda b, j: (j, 0)),
                  pl.BlockSpec((tm, 512), lambda b, j: (j, 0)),
                  vec, vec, _resident((tm, tm), lambda b, j: (0, 0)), _resident((tm, tm), lambda b, j: (0, 0))],
        out_specs=out_specs,
        out_shape=out_shape,
        compiler_params=_cparams(("arbitrary", "arbitrary"), V7X_VMEM_LIMIT),
        name="inproj",
    )(x_lat, x_ctx, mod3, g.reshape(1, d), w_bf, cos_t, sin_t, lb_f.reshape(1, 512), lb_b.reshape(1, 512), tri_f, tri_b)
    return {name: o for (name, _, _), o in zip(_IN_OUTS, outs)}


def _hgrn_direction(q_ref, b_ref, k_ref, v_ref, o_ref, st_ref, reverse, rows):
    c = A_CHUNK
    row = lax.broadcasted_iota(jnp.int32, (c, c), 0)
    col = lax.broadcasted_iota(jnp.int32, (c, c), 1)
    allowed = (col >= row) if reverse else (col <= row)

    b = b_ref[0, rows, :]
    tot_row = 0 if reverse else c - 1
    tot = b[tot_row:tot_row + 1, :]

    q = q_ref[0, rows, :].astype(F32)
    kin = k_ref[0, rows, :].astype(F32)
    v = v_ref[0, rows, :]
    zero_blk = jnp.zeros((A_SUB, HEAD_W), F32)
    heads = []
    for h in range(A_HEADS):
        hs = slice(h * HEAD_W, (h + 1) * HEAD_W)
        bh, qh, kh = b[:, hs], q[:, hs], kin[:, hs]
        q_state = qh * jnp.exp(bh)
        k_state = kh * jnp.exp(tot[:, hs] - bh)
        q_rows, k_cols = [], []
        for i in range(c // A_SUB):
            r0 = i * A_SUB
            ref = r0 + A_SUB - 1 if reverse else r0
            r = bh[ref:ref + 1, :]
            qt = qh[r0:r0 + A_SUB] * jnp.exp(bh[r0:r0 + A_SUB] - r)
            q_rows.append(jnp.concatenate([qt if j == i else zero_blk for j in range(c // A_SUB)], axis=1))
            live = slice(r0, c) if reverse else slice(0, r0 + A_SUB)
            kt = kh[live] * jnp.exp(jnp.minimum(r - bh[live], EXP_CLAMP))
            dead = jnp.zeros((c - kt.shape[0], HEAD_W), F32)
            k_cols.append(jnp.concatenate([dead, kt] if reverse else [kt, dead], axis=0) if dead.shape[0] else kt)
        heads.append(dict(
            hs=hs, h=h, rows=rows, allowed=allowed, vh=v[:, hs], decay=jnp.exp(tot[:, hs]), o_ref=o_ref, st_ref=st_ref,
            q_wide=jnp.concatenate(q_rows, axis=0).astype(BF16),
            k_wide=jnp.concatenate(k_cols, axis=1).astype(BF16),
            q_state=q_state.astype(BF16), k_state=k_state.astype(BF16)))
    return heads


def _hgrn_kernel(qf_ref, bf_ref, kf_ref, vf_ref, qb_ref, bb_ref, kb_ref, vb_ref, of_ref, ob_ref, sf_ref, sb_ref):
    @pl.when(pl.program_id(1) == 0)
    def _():
        sf_ref[...] = jnp.zeros_like(sf_ref)
        sb_ref[...] = jnp.zeros_like(sb_ref)

    n_sub = qf_ref.shape[1] // A_CHUNK
    seq = []
    for i in range(n_sub):
        up = slice(i * A_CHUNK, (i + 1) * A_CHUNK)
        down = slice((n_sub - 1 - i) * A_CHUNK, (n_sub - i) * A_CHUNK)
        seq.append(_hgrn_direction(qf_ref, bf_ref, kf_ref, vf_ref, of_ref, sf_ref, reverse=False, rows=up)
                   + _hgrn_direction(qb_ref, bb_ref, kb_ref, vb_ref, ob_ref, sb_ref, reverse=True, rows=down))
    for chains in seq:
        for ch in chains:
            ch["att"] = _dot_nt(ch["q_wide"], ch["k_wide"])
    state = [ch["st_ref"][ch["h"]] for ch in seq[0]]
    for chains in seq:
        for ch, st in zip(chains, state):
            ch["inter"] = _dot_nt(ch["q_state"], st.astype(BF16))
            ch["upd"] = _dot_tn(ch["vh"], ch["k_state"])
        for n, ch in enumerate(chains):
            att = jnp.where(ch["allowed"], ch["att"], 0.0).astype(BF16)
            o = _dot(att, ch["vh"]) + ch["inter"]
            ch["o_ref"][0, ch["rows"], ch["hs"]] = o.astype(ch["o_ref"].dtype)
            state[n] = state[n] * ch["decay"] + ch["upd"]
    for ch, st in zip(seq[0], state):
        ch["st_ref"][ch["h"]] = st


def _hgrn(qa, bf, kf, bb, kb, ia, n_lat):
    bsz, t, w = qa.shape
    c = HGRN_ROWS
    n_chunks = t // c
    lat_chunks = n_lat // c

    def fwd(b, s):
        return (b, (s + lat_chunks) % n_chunks, 0)

    def bwd(b, s):
        return (b, n_chunks - 1 - s, 0)

    blk = (1, c, w)
    return pl.pallas_call(
        _hgrn_kernel,
        grid=(bsz, n_chunks),
        in_specs=[pl.BlockSpec(blk, fwd)] * 4 + [pl.BlockSpec(blk, bwd)] * 4,
        out_specs=[pl.BlockSpec(blk, fwd), pl.BlockSpec(blk, bwd)],
        out_shape=[jax.ShapeDtypeStruct((bsz, t, w), BF16)] * 2,
        scratch_shapes=[pltpu.VMEM((A_HEADS, HEAD_W, HEAD_W), F32)] * 2,
        compiler_params=_cparams(("arbitrary", "arbitrary")),
        name="hgrn2",
    )(qa, bf, kf, ia, qa, bb, kb, ia)


def _attn_kernel(q_ref, k_ref, v_ref, lam_ref, sub_ref, o_ref, v1_ref, m_ref, acc_ref, *, key_blocks, lam_init):
    tq = q_ref.shape[1]

    @pl.when(pl.program_id(2) == 0)
    def _():
        v1_ref[:, :HEAD_W] = v_ref[0]
        v1_ref[:, HEAD_W:] = jnp.ones((v1_ref.shape[0], HEAD_W), BF16)

    lane = lax.broadcasted_iota(jnp.int32, (tq, HEAD_W), 1)
    qf = q_ref[0].astype(F32) * (C_DH ** -0.5 * LOG2_E)
    q2 = jnp.concatenate([jnp.where(lane < C_DH, qf, 0.0), jnp.where(lane >= C_DH, qf, 0.0)], axis=0).astype(BF16)

    m_ref[...] = jnp.full_like(m_ref, NEG_BIG)
    acc_ref[...] = jnp.zeros_like(acc_ref)

    def scores(start, size):
        return _dot_nt(q2, k_ref[0, pl.ds(start, size), :])

    def absorb(s, start, size):
        m_old = m_ref[...]
        m_new = jnp.maximum(m_old, jnp.max(s, axis=-1, keepdims=True))
        alpha = jnp.exp2(m_old - m_new)
        p = jnp.exp2(s - jnp.concatenate([m_new] * (size // HEAD_W), axis=1))
        acc_ref[...] = (jnp.concatenate([alpha, alpha], axis=1) * acc_ref[...]
                        + _dot(p.astype(BF16), v1_ref[pl.ds(start, size), :]))
        m_ref[...] = m_new

    cur = scores(*key_blocks[0])
    for n, blk in enumerate(key_blocks):
        nxt = scores(*key_blocks[n + 1]) if n + 1 < len(key_blocks) else None
        absorb(cur, *blk)
        cur = nxt

    o12 = acc_ref[:, :HEAD_W] / acc_ref[:, HEAD_W:]
    o = o12[:tq] - lam_ref[...] * o12[tq:]
    o_ref[0] = (_rms(o) * sub_ref[...] * (1.0 - lam_init)).astype(o_ref.dtype)


def _attention(qc, kc, vc, lam, subln, lam_init, tq, q_block0, n_q, kv_block, key_blocks):
    bsz, _, w = qc.shape
    kv_rows, kv_idx = kv_block
    return pl.pallas_call(
        functools.partial(_attn_kernel, key_blocks=key_blocks, lam_init=lam_init),
        grid=(bsz, C_HEADS, n_q),
        in_specs=[pl.BlockSpec((1, tq, HEAD_W), lambda b, h, i: (b, q_block0 + i, h)),
                  pl.BlockSpec((1, kv_rows, HEAD_W), lambda b, h, i: (b, kv_idx, h)),
                  pl.BlockSpec((1, kv_rows, HEAD_W), lambda b, h, i: (b, kv_idx, h)),
                  pl.BlockSpec((1, HEAD_W), lambda b, h, i: (0, 0)),
                  pl.BlockSpec((1, HEAD_W), lambda b, h, i: (0, h))],
        out_specs=pl.BlockSpec((1, tq, HEAD_W), lambda b, h, i: (b, i, h)),
        out_shape=jax.ShapeDtypeStruct((bsz, n_q * tq, w), BF16),
        scratch_shapes=[pltpu.VMEM((kv_rows, 2 * HEAD_W), BF16), pltpu.VMEM((2 * tq, HEAD_W), F32),
                        pltpu.VMEM((2 * tq, 2 * HEAD_W), F32)],
        compiler_params=_cparams(("arbitrary", "arbitrary", "arbitrary")),
        name="diff_attention",
    )(qc, kc, vc, jnp.full((1, HEAD_W), lam, F32), subln.reshape(1, w))


def _merge_kernel(xl_ref, xc_ref, mod_ref, of_ref, ob_ref, og_ref, gn_ref, uv_ref, vn_ref, ws_ref, bs_ref, ycl_ref,
                  ycc_ref, gt_ref, wb_ref, wo_ref, g2_ref, wr_ref, x_out_ref, h_out_ref, *maybe_logits_ref, d, n_lat_blocks):
    tm = xl_ref.shape[1]
    is_lat = pl.program_id(1) < n_lat_blocks
    mod = mod_ref[0]
    groups = [slice(r0, r0 + B_CHUNK) for r0 in range(0, tm, B_CHUNK)]
    st = [dict() for _ in groups]

    for s, rows in zip(st, groups):
        o = of_ref[0, rows, :].astype(F32) + ob_ref[0, rows, :].astype(F32)
        o = jnp.concatenate([_rms(o[:, h * HEAD_W:(h + 1) * HEAD_W]) for h in range(A_HEADS)], axis=1)
        s["ya"] = (o * gn_ref[...] * og_ref[0, rows, :].astype(F32)).astype(BF16)
        uv = uv_ref[0, rows, :].astype(F32)
        s["u"] = uv[:, :BRANCH_W]
        s["vv"] = (_rms(uv[:, BRANCH_W:]) * vn_ref[...]).astype(BF16)
    for s in st:
        s["mixed"] = jnp.concatenate(
            [_dot(ws_ref[g], s["vv"][:, g * HEAD_W:(g + 1) * HEAD_W]) for g in range(B_GROUPS)], axis=1)
    for s, rows in zip(st, groups):
        yb = (s["u"] * (s["mixed"] + bs_ref[...])).astype(BF16)
        yc = jnp.where(is_lat, ycl_ref[0, rows, :], ycc_ref[0, rows, :])
        s["proj"] = [_dot(y, wb_ref[i]) for i, y in enumerate((s["ya"], yb, yc))]
    for s, rows in zip(st, groups):
        merged = None
        for i in range(N_BRANCH):
            term = gt_ref[0, rows, i * d:(i + 1) * d].astype(F32) * s["proj"][i]
            merged = term if merged is None else merged + term
        s["mix"] = _dot(merged.astype(BF16), wo_ref[...])
    for s, rows in zip(st, groups):
        x_old = jnp.where(is_lat, xl_ref[0, rows, :], xc_ref[0, rows, :])
        x_new = x_old + mod[:, 2 * d:3 * d] * s["mix"]
        x_out_ref[0, rows, :] = x_new
        h = _rms(x_new) * g2_ref[...]
        h = h * (1.0 + mod[:, 4 * d:5 * d]) + mod[:, 3 * d:4 * d]
        h_hi = h.astype(BF16)
        h_out_ref[0, rows, :] = h_hi
        if maybe_logits_ref:
            h_lo = (h - h_hi.astype(F32)).astype(BF16)
            maybe_logits_ref[0][0, rows, :] = (_dot(h_hi, wr_ref[0]) + _dot(h_lo, wr_ref[0])) + _dot(h_hi, wr_ref[1])


def _merge(x_lat, x_ctx, ctx_block0, mod3, proj, of, ob, yc_lat, yc_ctx, gnorm, vnorm, ws_bf, bs_full, wb_bf, wo_bf, g2, wr_split,
           n_lat_blocks, n_blocks, with_logits):
    bsz, _, d = x_lat.shape
    tm = ROW_TILE
    ctx_row = bsz

    def rows(w):
        return pl.BlockSpec((1, tm, w), lambda b, j: (b, j, 0))

    def const2(shape):
        return pl.BlockSpec(shape, lambda b, j: (0, 0))

    def mod_map(b, j):
        return (jnp.where(j < n_lat_blocks, b, ctx_row), 0, 0)

    n_rows = n_blocks * tm
    out_shape = [jax.ShapeDtypeStruct((bsz, n_rows, d), F32), jax.ShapeDtypeStruct((bsz, n_rows, d), BF16)]
    out_specs = [rows(d), rows(d)]
    if with_logits:
        out_shape.append(jax.ShapeDtypeStruct((bsz, n_rows, HEAD_W), F32))
        out_specs.append(rows(HEAD_W))
    lat_spec, ctx_spec = _lat_ctx_specs(tm, d, n_lat_blocks, ctx_block0)
    return pl.pallas_call(
        functools.partial(_merge_kernel, d=d, n_lat_blocks=n_lat_blocks),
        grid=(bsz, n_blocks),
        in_specs=[lat_spec, ctx_spec, pl.BlockSpec((1, 1, mod3.shape[2]), mod_map),
                  rows(BRANCH_W), rows(BRANCH_W), rows(BRANCH_W), const2((1, BRANCH_W)),
                  rows(2 * BRANCH_W), const2((1, BRANCH_W)),
                  _resident((B_GROUPS, B_CHUNK, B_CHUNK), lambda b, j: (0, 0, 0)), const2((B_CHUNK, BRANCH_W)),
                  *_lat_ctx_specs(tm, BRANCH_W, n_lat_blocks, 0), rows(N_BRANCH * d),
                  _resident((N_BRANCH, BRANCH_W, d), lambda b, j: (0, 0, 0)), _resident((d, d), lambda b, j: (0, 0)),
                  const2((1, d)), _resident((2, d, HEAD_W), lambda b, j: (0, 0, 0))],
        out_specs=out_specs,
        out_shape=out_shape,
        compiler_params=_cparams(("arbitrary", "arbitrary"), V7X_VMEM_LIMIT),
        name="merge",
    )(x_lat, x_ctx, mod3, of, ob, proj["og"], gnorm.reshape(1, BRANCH_W), proj["uv"], vnorm.reshape(1, BRANCH_W),
      ws_bf, bs_full, yc_lat, yc_ctx, proj["gt"], wb_bf, wo_bf, g2.reshape(1, d), wr_split)


def _ffn_kernel(x_ref, h_ref, mod_ref, wg_ref, wu_ref, wd_ref, o_ref, *, d):
    h = h_ref[0]
    g = _dot(h, wg_ref[...])
    a = (g * _sigmoid(g) * _dot(h, wu_ref[...])).astype(BF16)
    y = _dot(a, wd_ref[...])
    o_ref[0] = x_ref[0] + mod_ref[0][:, 5 * d:6 * d] * y


def _ffn(xc, h2, mod3, wg_bf, wu_bf, wd_bf, n_lat_blocks):
    bsz, t, d = xc.shape
    tm = ROW_TILE
    dff = wg_bf.shape[1]
    ctx_row = bsz

    def mod_map(b, j):
        return (jnp.where(j < n_lat_blocks, b, ctx_row), 0, 0)

    rows = pl.BlockSpec((1, tm, d), lambda b, j: (b, j, 0))
    return pl.pallas_call(
        functools.partial(_ffn_kernel, d=d),
        grid=(bsz, t // tm),
        in_specs=[rows, rows, pl.BlockSpec((1, 1, mod3.shape[2]), mod_map),
                  _resident((d, dff), lambda b, j: (0, 0)), _resident((d, dff), lambda b, j: (0, 0)),
                  _resident((dff, d), lambda b, j: (0, 0))],
        out_specs=rows,
        out_shape=jax.ShapeDtypeStruct((bsz, t, d), F32),
        compiler_params=_cparams(("arbitrary", "arbitrary"), V7X_VMEM_LIMIT),
        name="dense_ffn",
    )(xc, h2, mod3, wg_bf, wu_bf, wd_bf)


def _moe_kernel(tile_expert_ref, n_tiles_ref, x_ref, wg_ref, wu_ref, wd_ref, o_ref):
    del tile_expert_ref

    @pl.when(pl.program_id(0) < n_tiles_ref[0])
    def _():
        x = x_ref[...]
        dff = wg_ref.shape[2]
        acc = None
        for c0 in range(0, dff, FF_CHUNK):
            g = _dot(x, wg_ref[0, :, c0:c0 + FF_CHUNK])
            a = (g * _sigmoid(g) * _dot(x, wu_ref[0, :, c0:c0 + FF_CHUNK])).astype(BF16)
            y = _dot(a, wd_ref[0, c0:c0 + FF_CHUNK, :])
            acc = y if acc is None else acc + y
        o_ref[...] = acc.astype(o_ref.dtype)

    @pl.when(pl.program_id(0) >= n_tiles_ref[0])
    def _():
        o_ref[...] = jnp.zeros_like(o_ref)


def _moe_experts(x_sorted, tile_expert, n_tiles, wg_bf, wu_bf, wd_bf):
    p, d = x_sorted.shape
    tm = MOE_TILE
    dff = wg_bf.shape[2]
    grid_spec = pltpu.PrefetchScalarGridSpec(
        num_scalar_prefetch=2,
        grid=(p // tm,),
        in_specs=[pl.BlockSpec((tm, d), lambda t, te, nt: (t, 0)),
                  pl.BlockSpec((1, d, dff), lambda t, te, nt: (te[t], 0, 0)),
                  pl.BlockSpec((1, d, dff), lambda t, te, nt: (te[t], 0, 0)),
                  pl.BlockSpec((1, dff, d), lambda t, te, nt: (te[t], 0, 0))],
        out_specs=pl.BlockSpec((tm, d), lambda t, te, nt: (t, 0)),
    )
    return pl.pallas_call(
        _moe_kernel,
        grid_spec=grid_spec,
        out_shape=jax.ShapeDtypeStruct((p, d), BF16),
        compiler_params=_cparams(("arbitrary",), V7X_VMEM_LIMIT),
        name="moe_experts",
    )(tile_expert, n_tiles, x_sorted, wg_bf, wu_bf, wd_bf)


def _route(logits, tm):
    n = logits.shape[0]
    _, top_i = lax.top_k(logits, TOP_K)
    flat_e = top_i.reshape(-1)
    onehot = (flat_e[:, None] == jnp.arange(N_EXPERTS)[None, :]).astype(jnp.int32)
    rank = jnp.take_along_axis(jnp.cumsum(onehot, axis=0) - onehot, flat_e[:, None], axis=1)[:, 0]
    counts = jnp.sum(onehot, axis=0)
    padded = ((counts + tm - 1) // tm) * tm
    ends = jnp.cumsum(padded)
    starts = ends - padded
    dest = starts[flat_e] + rank
    m = TOP_K * n
    p = m + N_EXPERTS * tm
    tile_start = jnp.arange(p // tm, dtype=jnp.int32) * tm
    tile_expert = jnp.minimum(jnp.sum((tile_start[:, None] >= ends[None, :]).astype(jnp.int32), axis=1),
                              N_EXPERTS - 1).astype(jnp.int32)
    n_tiles = (ends[-1] // tm).astype(jnp.int32).reshape(1)
    by_expert = jnp.sort(flat_e * m + jnp.arange(m, dtype=jnp.int32)) % m
    first = jnp.cumsum(counts) - counts
    row_expert = jnp.repeat(tile_expert, tm)
    pos = jnp.arange(p, dtype=jnp.int32)
    entry = jnp.take(by_expert, jnp.clip(first[row_expert] + pos - starts[row_expert], 0, m - 1))
    src_token = entry // TOP_K
    return dest.reshape(n, TOP_K), src_token, tile_expert, n_tiles


def _final_kernel(x_ref, y0_ref, y1_ref, lg_ref, mod_ref, g_ref, o_ref, *, d):
    lane = lax.broadcasted_iota(jnp.int32, lg_ref.shape[1:], 1)
    lg = jnp.where(lane < N_EXPERTS, lg_ref[0], NEG_BIG)
    m1 = jnp.max(lg, axis=-1, keepdims=True)
    first = jnp.min(jnp.where(lg == m1, lane, HEAD_W), axis=-1, keepdims=True)
    m2 = jnp.max(jnp.where(lane == first, NEG_BIG, lg), axis=-1, keepdims=True)
    e = jnp.exp(m2 - m1)
    w1 = 1.0 / (1.0 + e)
    y = w1 * y0_ref[0].astype(F32) + (e * w1) * y1_ref[0].astype(F32)
    x = x_ref[0] + mod_ref[0][:, 5 * d:6 * d] * y
    o_ref[0] = _rms(x) * g_ref[...]


def _final(x_lat, y0, y1, logits, mod3, g_final):
    bsz, s, d = x_lat.shape
    tm = ROW_TILE
    rows = pl.BlockSpec((1, tm, d), lambda b, j: (b, j, 0))
    return pl.pallas_call(
        functools.partial(_final_kernel, d=d),
        grid=(bsz, s // tm),
        in_specs=[rows, rows, rows, pl.BlockSpec((1, tm, HEAD_W), lambda b, j: (b, j, 0)),
                  pl.BlockSpec((1, 1, mod3.shape[2]), lambda b, j: (b, 0, 0)),
                  pl.BlockSpec((1, d), lambda b, j: (0, 0))],
        out_specs=rows,
        out_shape=jax.ShapeDtypeStruct((bsz, s, d), F32),
        compiler_params=_cparams(("arbitrary", "arbitrary")),
        name="moe_combine_final_norm",
    )(x_lat, y0, y1, logits, mod3, g_final.reshape(1, d))


def _rope_tables(n_lat, n_ctx):
    pairs = C_DH // 4
    tpos = jnp.arange(n_lat)
    pos = jnp.stack([(tpos // GRID_W).astype(F32), (tpos % GRID_W).astype(F32)], axis=1)
    freqs = ROPE_BASE ** (-jnp.arange(pairs, dtype=F32) / pairs)
    ang = pos[:, :, None] * freqs[None, None, :]
    cos = jnp.repeat(jnp.cos(ang)[:, :, None, :], 2, axis=2).reshape(n_lat, C_DH)
    sin = jnp.sin(ang)[:, :, None, :] * jnp.array([-1.0, 1.0], F32)[None, None, :, None]
    sin = sin.reshape(n_lat, C_DH)
    cos = jnp.concatenate([cos, jnp.ones((n_ctx, C_DH), F32)], axis=0)
    sin = jnp.concatenate([sin, jnp.zeros((n_ctx, C_DH), F32)], axis=0)
    reps = 512 // C_DH
    return jnp.tile(cos, (1, reps)), jnp.tile(sin, (1, reps))


def _lower_bound(p, layer):
    cs = jnp.cumsum(jax.nn.softmax(p.astype(F32), axis=0), axis=0)
    return cs[layer] - cs[0]


def kernel(x, c, ctx, c_ctx, w_ada, b_ada, g_norm1, g_norm2, w_in, hgrn_lb, hgrn_gnorm, mlp_vnorm, mlp_ws, mlp_bs,
           diff_lambda, diff_subln, w_branch, w_out, ffn_wg, ffn_wu, ffn_wd, moe_router, moe_wg, moe_wu, moe_wd,
           g_final):
    bsz, n_lat, d = x.shape
    n_ctx = ctx.shape[1]
    depth = w_ada.shape[0]
    t = n_lat + n_ctx
    tm = ROW_TILE
    assert depth == 2 and bsz < MOD_ROWS and n_lat % GRID_W == 0
    assert n_lat % ATT_TQ == 0 and n_lat % ATT_TK == 0 and n_lat % n_ctx == 0 and n_ctx % tm == 0
    n_lat_blocks = n_lat // tm

    cvec = jnp.zeros((MOD_ROWS, d), F32).at[:bsz].set(c).at[bsz].set(c_ctx)
    mods = _adaln(cvec, w_ada, b_ada)
    cos_t, sin_t = _rope_tables(n_lat, n_ctx)
    x_lat, x_ctx, ctx_block0 = x, ctx, 0

    out = None
    for layer in range(depth):
        last = layer == depth - 1
        mod3 = mods[layer].reshape(MOD_ROWS, 1, 6 * d)
        lb_f = _lower_bound(hgrn_lb[0], layer)
        lb_b = _lower_bound(hgrn_lb[1], layer)
        proj = _inproj(x_lat, x_ctx, ctx_block0, t, mod3, g_norm1[layer], w_in[layer].astype(BF16), cos_t, sin_t,
                       lb_f, lb_b, n_lat_blocks)
        of, ob = _hgrn(proj["qa"], proj["bf"], proj["kf"], proj["bb"], proj["kb"], proj["ia"], n_lat)

        lam_init = 0.8 - 0.6 * math.exp(-0.3 * layer)
        lam_p = diff_lambda[layer]
        lam = jnp.exp(jnp.sum(lam_p[0] * lam_p[1])) - jnp.exp(jnp.sum(lam_p[2] * lam_p[3])) + lam_init
        lat_keys = tuple((i * ATT_TK, ATT_TK) for i in range(n_lat // ATT_TK)) + ((n_lat, n_ctx),)
        qkv = (proj["qc"], proj["kc"], proj["vc"], lam.astype(F32), diff_subln[layer], lam_init)
        yc_lat = _attention(*qkv, ATT_TQ, 0, n_lat // ATT_TQ, (t, 0), lat_keys)
        yc_ctx = yc_lat if last else _attention(*qkv, n_ctx, n_lat // n_ctx, 1, (n_ctx, n_lat // n_ctx),
                                                ((0, n_ctx),))

        bs_full = jnp.repeat(mlp_bs[layer].T, HEAD_W, axis=1)
        j = layer // 2
        is_moe = layer % 2 == 1
        wr_split = jnp.zeros((2, d, HEAD_W), BF16)
        if is_moe:
            wr_hi = moe_router[j].astype(BF16)
            wr_lo = (moe_router[j] - wr_hi.astype(F32)).astype(BF16)
            wr_split = wr_split.at[:, :, :N_EXPERTS].set(jnp.stack([wr_hi, wr_lo]))
        n_blocks = n_lat_blocks if last else t // tm
        merged = _merge(x_lat, x_ctx, ctx_block0, mod3, proj, of, ob, yc_lat, yc_ctx, hgrn_gnorm[layer], mlp_vnorm[layer],
                        mlp_ws[layer].astype(BF16), bs_full, w_branch[layer].astype(BF16), w_out[layer].astype(BF16),
                        g_norm2[layer], wr_split, n_lat_blocks, n_blocks, is_moe)
        x_new, h2 = merged[0], merged[1]

        if not is_moe:
            assert not last
            xc = _ffn(x_new, h2, mod3, ffn_wg[j].astype(BF16), ffn_wu[j].astype(BF16), ffn_wd[j].astype(BF16),
                      n_lat_blocks)
            x_lat, x_ctx, ctx_block0 = xc, xc, n_lat_blocks
        else:
            assert last
            n = bsz * n_lat
            logits = merged[2]
            dest, src_token, tile_expert, n_tiles = _route(logits.reshape(n, HEAD_W)[:, :N_EXPERTS], MOE_TILE)
            x_sorted = jnp.take(h2.reshape(n, d), src_token, axis=0, mode="clip")
            y = _moe_experts(x_sorted, tile_expert, n_tiles, moe_wg[j].astype(BF16), moe_wu[j].astype(BF16),
                             moe_wd[j].astype(BF16))
            y0 = jnp.take(y, dest[:, 0], axis=0, mode="clip").reshape(bsz, n_lat, d)
            y1 = jnp.take(y, dest[:, 1], axis=0, mode="clip").reshape(bsz, n_lat, d)
            out = _final(x_new, y0, y1, logits, mod3, g_final)
    return out
```

```python
import functools
import math

import jax
import jax.numpy as jnp
from jax import lax
from jax.experimental import pallas as pl
from jax.experimental.pallas import tpu as pltpu

F32 = jnp.float32
BF16 = jnp.bfloat16

EPS = 1e-6
GRID_W = 64
ROPE_BASE = 10000.0

A_HEADS = 4
A_CHUNK = 64
A_SUB = 16
HGRN_ROWS = 256
MOE_TILE = 512
B_GROUPS = 4
B_CHUNK = 128
C_HEADS = 4
C_DH = 64
HEAD_W = 128
BRANCH_W = 512
N_BRANCH = 3
N_EXPERTS = 8
TOP_K = 2

ROW_TILE = 256
MOD_ROWS = 16
ATT_TQ = 512
ATT_TK = 512
FF_CHUNK = 512
EXP_CLAMP = 80.0
LOG2_E = 1.4426950408889634
NEG_BIG = -1e30
V7X_VMEM_LIMIT = 56 * 1024 * 1024


def _cparams(sem, vmem=None):
    return pltpu.CompilerParams(dimension_semantics=sem, vmem_limit_bytes=vmem)


def _resident(shape, index_map):
    return pl.BlockSpec(shape, index_map, pipeline_mode=pl.Buffered(1))


def _lat_ctx_specs(tm, d, n_lat_blocks, ctx_block0):
    lat = pl.BlockSpec((1, tm, d), lambda b, j: (b, jnp.minimum(j, n_lat_blocks - 1), 0))
    ctx = pl.BlockSpec((1, tm, d), lambda b, j: (b, ctx_block0 + jnp.maximum(j - n_lat_blocks, 0), 0))
    return lat, ctx


def _rms(xf):
    return xf * lax.rsqrt(jnp.mean(xf * xf, axis=-1, keepdims=True) + EPS)


def _sigmoid(x):
    return 1.0 / (1.0 + jnp.exp(-x))


def _dot(a, b):
    return jnp.dot(a, b, preferred_element_type=F32)


def _dot_nt(a, b):
    return lax.dot_general(a, b, (((1,), (1,)), ((), ())), preferred_element_type=F32)


def _dot_tn(a, b):
    return lax.dot_general(a, b, (((0,), (0,)), ((), ())), preferred_element_type=F32)


def _adaln_kernel(c_ref, w_ref, b_ref, o_ref):
    c = c_ref[...]
    a = c * _sigmoid(c)
    o_ref[0] = jnp.dot(a, w_ref[0], preferred_element_type=F32, precision=lax.Precision.HIGHEST) + b_ref[0]


def _adaln(cvec, w_ada, b_ada):
    depth, d, n = w_ada.shape
    tn = 1536
    return pl.pallas_call(
        _adaln_kernel,
        grid=(depth, n // tn),
        in_specs=[pl.BlockSpec((MOD_ROWS, d), lambda l, j: (0, 0)),
                  pl.BlockSpec((1, d, tn), lambda l, j: (l, 0, j)),
                  pl.BlockSpec((1, 1, tn), lambda l, j: (l, 0, j))],
        out_specs=pl.BlockSpec((1, MOD_ROWS, tn), lambda l, j: (l, 0, j)),
        out_shape=jax.ShapeDtypeStruct((depth, MOD_ROWS, n), F32),
        compiler_params=_cparams(("arbitrary", "arbitrary")),
        name="adaln",
    )(cvec, w_ada, b_ada.reshape(depth, 1, n))


_IN_COLS = (("qa", 512), ("zf", 512), ("zb", 512), ("ia", 512), ("og", 512), ("uv", 1024), ("qc", 512), ("kc", 512),
            ("vc", 512), ("gt", 3072))
_IN_OUTS = (("qa", 512, BF16), ("bf", 512, F32), ("kf", 512, BF16), ("bb", 512, F32), ("kb", 512, BF16),
            ("ia", 512, BF16), ("og", 512, BF16), ("uv", 1024, BF16), ("qc", 512, BF16), ("kc", 512, BF16),
            ("vc", 512, BF16), ("gt", 3072, BF16))


def _gelu_tanh(x):
    return 0.5 * x * (1.0 + jnp.tanh(math.sqrt(2.0 / math.pi) * (x + 0.044715 * (x * x * x))))


_IN_ACTIVATIONS = {"og": lambda p: p * _sigmoid(p), "uv": _gelu_tanh, "gt": _sigmoid}


def _forget_gate(z, lb):
    sp = jnp.maximum(-z, 0.0) + jnp.log(1.0 + jnp.exp(-jnp.abs(z)))
    la = jnp.log(lb)
    lc = jnp.log1p(-lb) - sp
    logf = jnp.maximum(la, lc) + jnp.log(1.0 + jnp.exp(-jnp.abs(la - lc)))
    kin = (1.0 - lb) * jnp.exp(-z - sp)
    hi = logf.astype(BF16)
    rest = logf - hi.astype(F32)
    mid = rest.astype(BF16)
    low = (rest - mid.astype(F32)).astype(BF16)
    return (hi, mid, low), kin


def _inproj_kernel(xl_ref, xc_ref, mod_ref, g_ref, w_ref, cos_ref, sin_ref, lbf_ref, lbb_ref, trif_ref, trib_ref,
                   *out_refs, d, n_lat_blocks):
    x = jnp.where(pl.program_id(1) < n_lat_blocks, xl_ref[0], xc_ref[0])
    mod = mod_ref[0]
    h = _rms(x) * g_ref[...]
    h = (h * (1.0 + mod[:, d:2 * d]) + mod[:, 0:d]).astype(BF16)

    tm = x.shape[0]
    lane = lax.broadcasted_iota(jnp.int32, (tm, 512), 1)
    low_half = (lane % 32) < 16

    def rope(p):
        partner = jnp.where(low_half, pltpu.roll(p, 512 - 16, 1), pltpu.roll(p, 16, 1))
        return p * cos_ref[...] + partner * sin_ref[...]

    outs = {name: ref for (name, _, _), ref in zip(_IN_OUTS, out_refs)}
    col0 = {}
    lo = 0
    for name, width in _IN_COLS:
        col0[name] = lo
        lo += width

    gates = {}
    for name, lb_ref in (("zf", lbf_ref), ("zb", lbb_ref)):
        terms, kin = _forget_gate(_dot(h, w_ref[0, :, col0[name]:col0[name] + 512]), lb_ref[...])
        outs["k" + name[1]][0] = kin.astype(BF16)
        gates[name] = terms
    for name, width in _IN_COLS:
        if name in gates:
            continue
        for c0 in range(0, width, 512):
            p = _dot(h, w_ref[0, :, col0[name] + c0:col0[name] + c0 + 512])
            if name in ("qc", "kc"):
                p = rope(p)
            elif name in _IN_ACTIVATIONS:
                p = _IN_ACTIVATIONS[name](p)
            outs[name][0, :, c0:c0 + 512] = p.astype(outs[name].dtype)
    for name, tri_ref in (("zf", trif_ref), ("zb", trib_ref)):
        hi, mid, low = gates[name]
        outs["b" + name[1]][0] = _dot(tri_ref[...], hi) + _dot(tri_ref[...], mid) + _dot(tri_ref[...], low)


def _scan_matrices(tm):
    row = jnp.arange(tm)[:, None]
    col = jnp.arange(tm)[None, :]
    same = (row // A_CHUNK) == (col // A_CHUNK)
    return (same & (col <= row)).astype(BF16), (same & (col >= row)).astype(BF16)


def _inproj(x_lat, x_ctx, ctx_block0, t, mod3, g, w_bf, layer, cos_t, sin_t, lb_f, lb_b, n_lat_blocks):
    bsz, _, d = x_lat.shape
    tm = ROW_TILE
    ncols = w_bf.shape[2]
    ctx_row = bsz

    def mod_map(b, j):
        return (jnp.where(j < n_lat_blocks, b, ctx_row), 0, 0)

    out_shape = [jax.ShapeDtypeStruct((bsz, t, w), dt) for _, w, dt in _IN_OUTS]
    out_specs = [pl.BlockSpec((1, tm, w), lambda b, j: (b, j, 0)) for _, w, _ in _IN_OUTS]
    tri_f, tri_b = _scan_matrices(tm)
    vec = pl.BlockSpec((1, 512), lambda b, j: (0, 0))
    lat_spec, ctx_spec = _lat_ctx_specs(tm, d, n_lat_blocks, ctx_block0)
    outs = pl.pallas_call(
        functools.partial(_inproj_kernel, d=d, n_lat_blocks=n_lat_blocks),
        grid=(bsz, t // tm),
        in_specs=[lat_spec, ctx_spec,
                  pl.BlockSpec((1, 1, mod3.shape[2]), mod_map),
                  pl.BlockSpec((1, d), lambda b, j: (0, 0)),
                  _resident((1, d, ncols), lambda b, j: (layer, 0, 0)),
                  pl.BlockSpec((tm, 512), lambda b, j: (j, 0)),
                  pl.BlockSpec((tm, 512), lambda b, j: (j, 0)),
                  vec, vec, _resident((tm, tm), lambda b, j: (0, 0)), _resident((tm, tm), lambda b, j: (0, 0))],
        out_specs=out_specs,
        out_shape=out_shape,
        compiler_params=_cparams(("arbitrary", "arbitrary"), V7X_VMEM_LIMIT),
        name="inproj",
    )(x_lat, x_ctx, mod3, g.reshape(1, d), w_bf, cos_t, sin_t, lb_f.reshape(1, 512), lb_b.reshape(1, 512), tri_f, tri_b)
    return {name: o for (name, _, _), o in zip(_IN_OUTS, outs)}


def _hgrn_direction(q_ref, b_ref, k_ref, v_ref, o_ref, st_ref, reverse, rows):
    c = A_CHUNK
    row = lax.broadcasted_iota(jnp.int32, (c, c), 0)
    col = lax.broadcasted_iota(jnp.int32, (c, c), 1)
    allowed = (col >= row) if reverse else (col <= row)

    b = b_ref[0, rows, :]
    tot_row = 0 if reverse else c - 1
    tot = b[tot_row:tot_row + 1, :]

    q = q_ref[0, rows, :].astype(F32)
    kin = k_ref[0, rows, :].astype(F32)
    v = v_ref[0, rows, :]
    zero_blk = jnp.zeros((A_SUB, HEAD_W), F32)
    heads = []
    for h in range(A_HEADS):
        hs = slice(h * HEAD_W, (h + 1) * HEAD_W)
        bh, qh, kh = b[:, hs], q[:, hs], kin[:, hs]
        q_state = qh * jnp.exp(bh)
        k_state = kh * jnp.exp(tot[:, hs] - bh)
        q_rows, k_cols = [], []
        for i in range(c // A_SUB):
            r0 = i * A_SUB
            ref = r0 + A_SUB - 1 if reverse else r0
            r = bh[ref:ref + 1, :]
            qt = qh[r0:r0 + A_SUB] * jnp.exp(bh[r0:r0 + A_SUB] - r)
            q_rows.append(jnp.concatenate([qt if j == i else zero_blk for j in range(c // A_SUB)], axis=1))
            live = slice(r0, c) if reverse else slice(0, r0 + A_SUB)
            kt = kh[live] * jnp.exp(jnp.minimum(r - bh[live], EXP_CLAMP))
            dead = jnp.zeros((c - kt.shape[0], HEAD_W), F32)
            k_cols.append(jnp.concatenate([dead, kt] if reverse else [kt, dead], axis=0) if dead.shape[0] else kt)
        heads.append(dict(
            hs=hs, h=h, rows=rows, allowed=allowed, vh=v[:, hs], decay=jnp.exp(tot[:, hs]), o_ref=o_ref, st_ref=st_ref,
            q_wide=jnp.concatenate(q_rows, axis=0).astype(BF16),
            k_wide=jnp.concatenate(k_cols, axis=1).astype(BF16),
            q_state=q_state.astype(BF16), k_state=k_state.astype(BF16)))
    return heads


def _hgrn_kernel(qf_ref, bf_ref, kf_ref, vf_ref, qb_ref, bb_ref, kb_ref, vb_ref, of_ref, ob_ref, sf_ref, sb_ref):
    @pl.when(pl.program_id(1) == 0)
    def _():
        sf_ref[...] = jnp.zeros_like(sf_ref)
        sb_ref[...] = jnp.zeros_like(sb_ref)

    n_sub = qf_ref.shape[1] // A_CHUNK
    seq = []
    for i in range(n_sub):
        up = slice(i * A_CHUNK, (i + 1) * A_CHUNK)
        down = slice((n_sub - 1 - i) * A_CHUNK, (n_sub - i) * A_CHUNK)
        seq.append(_hgrn_direction(qf_ref, bf_ref, kf_ref, vf_ref, of_ref, sf_ref, reverse=False, rows=up)
                   + _hgrn_direction(qb_ref, bb_ref, kb_ref, vb_ref, ob_ref, sb_ref, reverse=True, rows=down))
    for chains in seq:
        for ch in chains:
            ch["att"] = _dot_nt(ch["q_wide"], ch["k_wide"])
    state = [ch["st_ref"][ch["h"]] for ch in seq[0]]
    for chains in seq:
        for ch, st in zip(chains, state):
            ch["inter"] = _dot_nt(ch["q_state"], st.astype(BF16))
            ch["upd"] = _dot_tn(ch["vh"], ch["k_state"])
        for n, ch in enumerate(chains):
            att = jnp.where(ch["allowed"], ch["att"], 0.0).astype(BF16)
            o = _dot(att, ch["vh"]) + ch["inter"]
            ch["o_ref"][0, ch["rows"], ch["hs"]] = o.astype(ch["o_ref"].dtype)
            state[n] = state[n] * ch["decay"] + ch["upd"]
    for ch, st in zip(seq[0], state):
        ch["st_ref"][ch["h"]] = st


def _hgrn(qa, bf, kf, bb, kb, ia, n_lat):
    bsz, t, w = qa.shape
    c = HGRN_ROWS
    n_chunks = t // c
    lat_chunks = n_lat // c

    def fwd(b, s):
        return (b, (s + lat_chunks) % n_chunks, 0)

    def bwd(b, s):
        return (b, n_chunks - 1 - s, 0)

    blk = (1, c, w)
    return pl.pallas_call(
        _hgrn_kernel,
        grid=(bsz, n_chunks),
        in_specs=[pl.BlockSpec(blk, fwd)] * 4 + [pl.BlockSpec(blk, bwd)] * 4,
        out_specs=[pl.BlockSpec(blk, fwd), pl.BlockSpec(blk, bwd)],
        out_shape=[jax.ShapeDtypeStruct((bsz, t, w), BF16)] * 2,
        scratch_shapes=[pltpu.VMEM((A_HEADS, HEAD_W, HEAD_W), F32)] * 2,
        compiler_params=_cparams(("arbitrary", "arbitrary")),
        name="hgrn2",
    )(qa, bf, kf, ia, qa, bb, kb, ia)


def _attn_kernel(q_ref, k_ref, v_ref, lam_ref, sub_ref, o_ref, v1_ref, m_ref, acc_ref, *, key_blocks, lam_init):
    tq = q_ref.shape[1]

    @pl.when(pl.program_id(2) == 0)
    def _():
        v1_ref[:, :HEAD_W] = v_ref[0]
        v1_ref[:, HEAD_W:] = jnp.ones((v1_ref.shape[0], HEAD_W), BF16)

    lane = lax.broadcasted_iota(jnp.int32, (tq, HEAD_W), 1)
    qf = q_ref[0].astype(F32) * (C_DH ** -0.5 * LOG2_E)
    q2 = jnp.concatenate([jnp.where(lane < C_DH, qf, 0.0), jnp.where(lane >= C_DH, qf, 0.0)], axis=0).astype(BF16)

    m_ref[...] = jnp.full_like(m_ref, NEG_BIG)
    acc_ref[...] = jnp.zeros_like(acc_ref)

    def scores(start, size):
        return _dot_nt(q2, k_ref[0, pl.ds(start, size), :])

    def absorb(s, start, size):
        m_old = m_ref[...]
        m_new = jnp.maximum(m_old, jnp.max(s, axis=-1, keepdims=True))
        alpha = jnp.exp2(m_old - m_new)
        p = jnp.exp2(s - jnp.concatenate([m_new] * (size // HEAD_W), axis=1))
        acc_ref[...] = (jnp.concatenate([alpha, alpha], axis=1) * acc_ref[...]
                        + _dot(p.astype(BF16), v1_ref[pl.ds(start, size), :]))
        m_ref[...] = m_new

    cur = scores(*key_blocks[0])
    for n, blk in enumerate(key_blocks):
        nxt = scores(*key_blocks[n + 1]) if n + 1 < len(key_blocks) else None
        absorb(cur, *blk)
        cur = nxt

    o12 = acc_ref[:, :HEAD_W] / acc_ref[:, HEAD_W:]
    o = o12[:tq] - lam_ref[...] * o12[tq:]
    o_ref[0] = (_rms(o) * sub_ref[...] * (1.0 - lam_init)).astype(o_ref.dtype)


def _attention(qc, kc, vc, lam, subln, lam_init, tq, q_block0, n_q, kv_block, key_blocks):
    bsz, _, w = qc.shape
    kv_rows, kv_idx = kv_block
    return pl.pallas_call(
        functools.partial(_attn_kernel, key_blocks=key_blocks, lam_init=lam_init),
        grid=(bsz, C_HEADS, n_q),
        in_specs=[pl.BlockSpec((1, tq, HEAD_W), lambda b, h, i: (b, q_block0 + i, h)),
                  pl.BlockSpec((1, kv_rows, HEAD_W), lambda b, h, i: (b, kv_idx, h)),
                  pl.BlockSpec((1, kv_rows, HEAD_W), lambda b, h, i: (b, kv_idx, h)),
                  pl.BlockSpec((1, HEAD_W), lambda b, h, i: (0, 0)),
                  pl.BlockSpec((1, HEAD_W), lambda b, h, i: (0, h))],
        out_specs=pl.BlockSpec((1, tq, HEAD_W), lambda b, h, i: (b, i, h)),
        out_shape=jax.ShapeDtypeStruct((bsz, n_q * tq, w), BF16),
        scratch_shapes=[pltpu.VMEM((kv_rows, 2 * HEAD_W), BF16), pltpu.VMEM((2 * tq, HEAD_W), F32),
                        pltpu.VMEM((2 * tq, 2 * HEAD_W), F32)],
        compiler_params=_cparams(("arbitrary", "arbitrary", "arbitrary")),
        name="diff_attention",
    )(qc, kc, vc, jnp.full((1, HEAD_W), lam, F32), subln.reshape(1, w))


def _merge_kernel(xl_ref, xc_ref, mod_ref, of_ref, ob_ref, og_ref, gn_ref, uv_ref, vn_ref, ws_ref, bs_ref, ycl_ref,
                  ycc_ref, gt_ref, wb_ref, wo_ref, g2_ref, wr_ref, x_out_ref, h_out_ref, *maybe_logits_ref, d, n_lat_blocks):
    tm = xl_ref.shape[1]
    is_lat = pl.program_id(1) < n_lat_blocks
    mod = mod_ref[0]
    groups = [slice(r0, r0 + B_CHUNK) for r0 in range(0, tm, B_CHUNK)]
    st = [dict() for _ in groups]

    for s, rows in zip(st, groups):
        o = of_ref[0, rows, :].astype(F32) + ob_ref[0, rows, :].astype(F32)
        o = jnp.concatenate([_rms(o[:, h * HEAD_W:(h + 1) * HEAD_W]) for h in range(A_HEADS)], axis=1)
        s["ya"] = (o * gn_ref[...] * og_ref[0, rows, :].astype(F32)).astype(BF16)
        uv = uv_ref[0, rows, :].astype(F32)
        s["u"] = uv[:, :BRANCH_W]
        s["vv"] = (_rms(uv[:, BRANCH_W:]) * vn_ref[...]).astype(BF16)
    for s in st:
        s["mixed"] = jnp.concatenate(
            [_dot(ws_ref[g], s["vv"][:, g * HEAD_W:(g + 1) * HEAD_W]) for g in range(B_GROUPS)], axis=1)
    for s, rows in zip(st, groups):
        yb = (s["u"] * (s["mixed"] + bs_ref[...])).astype(BF16)
        yc = jnp.where(is_lat, ycl_ref[0, rows, :], ycc_ref[0, rows, :])
        s["proj"] = [_dot(y, wb_ref[i]) for i, y in enumerate((s["ya"], yb, yc))]
    for s, rows in zip(st, groups):
        merged = None
        for i in range(N_BRANCH):
            term = gt_ref[0, rows, i * d:(i + 1) * d].astype(F32) * s["proj"][i]
            merged = term if merged is None else merged + term
        s["mix"] = _dot(merged.astype(BF16), wo_ref[...])
    for s, rows in zip(st, groups):
        x_old = jnp.where(is_lat, xl_ref[0, rows, :], xc_ref[0, rows, :])
        x_new = x_old + mod[:, 2 * d:3 * d] * s["mix"]
        x_out_ref[0, rows, :] = x_new
        h = _rms(x_new) * g2_ref[...]
        h = h * (1.0 + mod[:, 4 * d:5 * d]) + mod[:, 3 * d:4 * d]
        h_hi = h.astype(BF16)
        h_out_ref[0, rows, :] = h_hi
        if maybe_logits_ref:
            h_lo = (h - h_hi.astype(F32)).astype(BF16)
            lg = (_dot(h_hi, wr_ref[0]) + _dot(h_lo, wr_ref[0])) + _dot(h_hi, wr_ref[1])
            maybe_logits_ref[0][0, rows, :] = lg
            maybe_logits_ref[1][:, rows] = lg.T


def _merge(x_lat, x_ctx, ctx_block0, mod3, proj, of, ob, yc_lat, yc_ctx, gnorm, vnorm, ws_bf, bs_full, wb_bf, wo_bf, g2, wr_split,
           n_lat_blocks, n_blocks, with_logits):
    bsz, _, d = x_lat.shape
    tm = ROW_TILE
    ctx_row = bsz

    def rows(w):
        return pl.BlockSpec((1, tm, w), lambda b, j: (b, j, 0))

    def const2(shape):
        return pl.BlockSpec(shape, lambda b, j: (0, 0))

    def mod_map(b, j):
        return (jnp.where(j < n_lat_blocks, b, ctx_row), 0, 0)

    n_rows = n_blocks * tm
    out_shape = [jax.ShapeDtypeStruct((bsz, n_rows, d), F32), jax.ShapeDtypeStruct((bsz, n_rows, d), BF16)]
    out_specs = [rows(d), rows(d)]
    if with_logits:
        out_shape += [jax.ShapeDtypeStruct((bsz, n_rows, HEAD_W), F32),
                      jax.ShapeDtypeStruct((HEAD_W, bsz * n_rows), F32)]
        out_specs += [rows(HEAD_W), pl.BlockSpec((HEAD_W, tm), lambda b, j: (0, b * n_blocks + j))]
    lat_spec, ctx_spec = _lat_ctx_specs(tm, d, n_lat_blocks, ctx_block0)
    return pl.pallas_call(
        functools.partial(_merge_kernel, d=d, n_lat_blocks=n_lat_blocks),
        grid=(bsz, n_blocks),
        in_specs=[lat_spec, ctx_spec, pl.BlockSpec((1, 1, mod3.shape[2]), mod_map),
                  rows(BRANCH_W), rows(BRANCH_W), rows(BRANCH_W), const2((1, BRANCH_W)),
                  rows(2 * BRANCH_W), const2((1, BRANCH_W)),
                  _resident((B_GROUPS, B_CHUNK, B_CHUNK), lambda b, j: (0, 0, 0)), const2((B_CHUNK, BRANCH_W)),
                  *_lat_ctx_specs(tm, BRANCH_W, n_lat_blocks, 0), rows(N_BRANCH * d),
                  _resident((N_BRANCH, BRANCH_W, d), lambda b, j: (0, 0, 0)), _resident((d, d), lambda b, j: (0, 0)),
                  const2((1, d)), _resident((2, d, HEAD_W), lambda b, j: (0, 0, 0))],
        out_specs=out_specs,
        out_shape=out_shape,
        compiler_params=_cparams(("arbitrary", "arbitrary"), V7X_VMEM_LIMIT),
        name="merge",
    )(x_lat, x_ctx, mod3, of, ob, proj["og"], gnorm.reshape(1, BRANCH_W), proj["uv"], vnorm.reshape(1, BRANCH_W),
      ws_bf, bs_full, yc_lat, yc_ctx, proj["gt"], wb_bf, wo_bf, g2.reshape(1, d), wr_split)


def _ffn_kernel(x_ref, h_ref, mod_ref, wg_ref, wu_ref, wd_ref, o_ref, *, d):
    h = h_ref[0]
    g = _dot(h, wg_ref[...])
    a = (g * _sigmoid(g) * _dot(h, wu_ref[...])).astype(BF16)
    y = _dot(a, wd_ref[...])
    o_ref[0] = x_ref[0] + mod_ref[0][:, 5 * d:6 * d] * y


def _ffn(xc, h2, mod3, wg_bf, wu_bf, wd_bf, n_lat_blocks):
    bsz, t, d = xc.shape
    tm = ROW_TILE
    dff = wg_bf.shape[1]
    ctx_row = bsz

    def mod_map(b, j):
        return (jnp.where(j < n_lat_blocks, b, ctx_row), 0, 0)

    rows = pl.BlockSpec((1, tm, d), lambda b, j: (b, j, 0))
    return pl.pallas_call(
        functools.partial(_ffn_kernel, d=d),
        grid=(bsz, t // tm),
        in_specs=[rows, rows, pl.BlockSpec((1, 1, mod3.shape[2]), mod_map),
                  _resident((d, dff), lambda b, j: (0, 0)), _resident((d, dff), lambda b, j: (0, 0)),
                  _resident((dff, d), lambda b, j: (0, 0))],
        out_specs=rows,
        out_shape=jax.ShapeDtypeStruct((bsz, t, d), F32),
        compiler_params=_cparams(("arbitrary", "arbitrary"), V7X_VMEM_LIMIT),
        name="dense_ffn",
    )(xc, h2, mod3, wg_bf, wu_bf, wd_bf)


def _moe_kernel(tile_expert_ref, n_tiles_ref, x_ref, wg_ref, wu_ref, wd_ref, o_ref):
    del tile_expert_ref

    @pl.when(pl.program_id(0) < n_tiles_ref[0])
    def _():
        x = x_ref[...]
        dff = wg_ref.shape[2]
        acc = None
        for c0 in range(0, dff, FF_CHUNK):
            g = _dot(x, wg_ref[0, :, c0:c0 + FF_CHUNK])
            a = (g * _sigmoid(g) * _dot(x, wu_ref[0, :, c0:c0 + FF_CHUNK])).astype(BF16)
            y = _dot(a, wd_ref[0, c0:c0 + FF_CHUNK, :])
            acc = y if acc is None else acc + y
        o_ref[...] = acc.astype(o_ref.dtype)

    @pl.when(pl.program_id(0) >= n_tiles_ref[0])
    def _():
        o_ref[...] = jnp.zeros_like(o_ref)


def _moe_experts(x_sorted, tile_expert, n_tiles, wg_bf, wu_bf, wd_bf):
    p, d = x_sorted.shape
    tm = MOE_TILE
    dff = wg_bf.shape[2]
    grid_spec = pltpu.PrefetchScalarGridSpec(
        num_scalar_prefetch=2,
        grid=(p // tm,),
        in_specs=[pl.BlockSpec((tm, d), lambda t, te, nt: (t, 0)),
                  pl.BlockSpec((1, d, dff), lambda t, te, nt: (te[t], 0, 0)),
                  pl.BlockSpec((1, d, dff), lambda t, te, nt: (te[t], 0, 0)),
                  pl.BlockSpec((1, dff, d), lambda t, te, nt: (te[t], 0, 0))],
        out_specs=pl.BlockSpec((tm, d), lambda t, te, nt: (t, 0)),
    )
    return pl.pallas_call(
        _moe_kernel,
        grid_spec=grid_spec,
        out_shape=jax.ShapeDtypeStruct((p, d), BF16),
        compiler_params=_cparams(("arbitrary",), V7X_VMEM_LIMIT),
        name="moe_experts",
    )(tile_expert, n_tiles, x_sorted, wg_bf, wu_bf, wd_bf)


def _route(logits_t, tm):
    n = logits_t.shape[1]
    experts = jnp.arange(N_EXPERTS, dtype=jnp.int32)[:, None]
    e1 = jnp.argmax(logits_t, axis=0).astype(jnp.int32)
    e2 = jnp.argmax(jnp.where(experts == e1[None, :], -jnp.inf, logits_t), axis=0).astype(jnp.int32)
    oh1 = (experts == e1[None, :]).astype(jnp.int32)
    oh2 = (experts == e2[None, :]).astype(jnp.int32)
    c1 = jnp.cumsum(oh1, axis=1)
    c2 = jnp.cumsum(oh2, axis=1) + c1[:, -1:]
    counts = c2[:, -1]
    padded = ((counts + tm - 1) // tm) * tm
    ends = jnp.cumsum(padded)
    starts = ends - padded
    dest1 = jnp.sum(oh1 * (starts[:, None] + c1 - 1), axis=0)
    dest2 = jnp.sum(oh2 * (starts[:, None] + c2 - 1), axis=0)
    m = TOP_K * n
    p = m + N_EXPERTS * tm
    tile_start = jnp.arange(p // tm, dtype=jnp.int32) * tm
    tile_expert = jnp.minimum(jnp.sum((tile_start[:, None] >= ends[None, :]).astype(jnp.int32), axis=1),
                              N_EXPERTS - 1).astype(jnp.int32)
    n_tiles = (ends[-1] // tm).astype(jnp.int32).reshape(1)
    token = jnp.arange(n, dtype=jnp.int32)
    by_expert = jnp.sort(jnp.concatenate([e1 * m + token, e2 * m + n + token])) % m
    first = jnp.cumsum(counts) - counts
    row_expert = jnp.repeat(tile_expert, tm)
    pos = jnp.arange(p, dtype=jnp.int32)
    entry = jnp.take(by_expert, jnp.clip(first[row_expert] + pos - starts[row_expert], 0, m - 1))
    src_token = jnp.where(entry < n, entry, entry - n)
    return dest1, dest2, src_token, tile_expert, n_tiles


def _final_kernel(x_ref, y0_ref, y1_ref, lg_ref, mod_ref, g_ref, o_ref, *, d):
    lane = lax.broadcasted_iota(jnp.int32, lg_ref.shape[1:], 1)
    lg = jnp.where(lane < N_EXPERTS, lg_ref[0], NEG_BIG)
    m1 = jnp.max(lg, axis=-1, keepdims=True)
    first = jnp.min(jnp.where(lg == m1, lane, HEAD_W), axis=-1, keepdims=True)
    m2 = jnp.max(jnp.where(lane == first, NEG_BIG, lg), axis=-1, keepdims=True)
    e = jnp.exp(m2 - m1)
    w1 = 1.0 / (1.0 + e)
    y = w1 * y0_ref[0].astype(F32) + (e * w1) * y1_ref[0].astype(F32)
    x = x_ref[0] + mod_ref[0][:, 5 * d:6 * d] * y
    o_ref[0] = _rms(x) * g_ref[...]


def _final(x_lat, y0, y1, logits, mod3, g_final):
    bsz, s, d = x_lat.shape
    tm = ROW_TILE
    rows = pl.BlockSpec((1, tm, d), lambda b, j: (b, j, 0))
    return pl.pallas_call(
        functools.partial(_final_kernel, d=d),
        grid=(bsz, s // tm),
        in_specs=[rows, rows, rows, pl.BlockSpec((1, tm, HEAD_W), lambda b, j: (b, j, 0)),
                  pl.BlockSpec((1, 1, mod3.shape[2]), lambda b, j: (b, 0, 0)),
                  pl.BlockSpec((1, d), lambda b, j: (0, 0))],
        out_specs=rows,
        out_shape=jax.ShapeDtypeStruct((bsz, s, d), F32),
        compiler_params=_cparams(("arbitrary", "arbitrary")),
        name="moe_combine_final_norm",
    )(x_lat, y0, y1, logits, mod3, g_final.reshape(1, d))


def _rope_tables(n_lat, n_ctx):
    pairs = C_DH // 4
    tpos = jnp.arange(n_lat)
    pos = jnp.stack([(tpos // GRID_W).astype(F32), (tpos % GRID_W).astype(F32)], axis=1)
    freqs = ROPE_BASE ** (-jnp.arange(pairs, dtype=F32) / pairs)
    ang = pos[:, :, None] * freqs[None, None, :]
    cos = jnp.repeat(jnp.cos(ang)[:, :, None, :], 2, axis=2).reshape(n_lat, C_DH)
    sin = jnp.sin(ang)[:, :, None, :] * jnp.array([-1.0, 1.0], F32)[None, None, :, None]
    sin = sin.reshape(n_lat, C_DH)
    cos = jnp.concatenate([cos, jnp.ones((n_ctx, C_DH), F32)], axis=0)
    sin = jnp.concatenate([sin, jnp.zeros((n_ctx, C_DH), F32)], axis=0)
    reps = 512 // C_DH
    return jnp.tile(cos, (1, reps)), jnp.tile(sin, (1, reps))


def _lower_bound(p, layer):
    cs = jnp.cumsum(jax.nn.softmax(p.astype(F32), axis=0), axis=0)
    return cs[layer] - cs[0]


def kernel(x, c, ctx, c_ctx, w_ada, b_ada, g_norm1, g_norm2, w_in, hgrn_lb, hgrn_gnorm, mlp_vnorm, mlp_ws, mlp_bs,
           diff_lambda, diff_subln, w_branch, w_out, ffn_wg, ffn_wu, ffn_wd, moe_router, moe_wg, moe_wu, moe_wd,
           g_final):
    bsz, n_lat, d = x.shape
    n_ctx = ctx.shape[1]
    depth = w_ada.shape[0]
    t = n_lat + n_ctx
    tm = ROW_TILE
    assert depth == 2 and bsz < MOD_ROWS and n_lat % GRID_W == 0
    assert n_lat % ATT_TQ == 0 and n_lat % ATT_TK == 0 and n_lat % n_ctx == 0 and n_ctx % tm == 0
    n_lat_blocks = n_lat // tm

    cvec = jnp.zeros((MOD_ROWS, d), F32).at[:bsz].set(c).at[bsz].set(c_ctx)
    mods = _adaln(cvec, w_ada, b_ada)
    cos_t, sin_t = _rope_tables(n_lat, n_ctx)
    x_lat, x_ctx, ctx_block0 = x, ctx, 0
    w_in_bf = w_in.astype(BF16)

    out = None
    for layer in range(depth):
        last = layer == depth - 1
        mod3 = mods[layer].reshape(MOD_ROWS, 1, 6 * d)
        lb_f = _lower_bound(hgrn_lb[0], layer)
        lb_b = _lower_bound(hgrn_lb[1], layer)
        proj = _inproj(x_lat, x_ctx, ctx_block0, t, mod3, g_norm1[layer], w_in_bf, layer, cos_t, sin_t,
                       lb_f, lb_b, n_lat_blocks)
        of, ob = _hgrn(proj["qa"], proj["bf"], proj["kf"], proj["bb"], proj["kb"], proj["ia"], n_lat)

        lam_init = 0.8 - 0.6 * math.exp(-0.3 * layer)
        lam_p = diff_lambda[layer]
        lam = jnp.exp(jnp.sum(lam_p[0] * lam_p[1])) - jnp.exp(jnp.sum(lam_p[2] * lam_p[3])) + lam_init
        lat_keys = tuple((i * ATT_TK, ATT_TK) for i in range(n_lat // ATT_TK)) + ((n_lat, n_ctx),)
        qkv = (proj["qc"], proj["kc"], proj["vc"], lam.astype(F32), diff_subln[layer], lam_init)
        yc_lat = _attention(*qkv, ATT_TQ, 0, n_lat // ATT_TQ, (t, 0), lat_keys)
        yc_ctx = yc_lat if last else _attention(*qkv, n_ctx, n_lat // n_ctx, 1, (n_ctx, n_lat // n_ctx),
                                                ((0, n_ctx),))

        bs_full = jnp.repeat(mlp_bs[layer].T, HEAD_W, axis=1)
        j = layer // 2
        is_moe = layer % 2 == 1
        wr_split = jnp.zeros((2, d, HEAD_W), BF16)
        if is_moe:
            wr_hi = moe_router[j].astype(BF16)
            wr_lo = (moe_router[j] - wr_hi.astype(F32)).astype(BF16)
            wr_split = wr_split.at[:, :, :N_EXPERTS].set(jnp.stack([wr_hi, wr_lo]))
        n_blocks = n_lat_blocks if last else t // tm
        merged = _merge(x_lat, x_ctx, ctx_block0, mod3, proj, of, ob, yc_lat, yc_ctx, hgrn_gnorm[layer], mlp_vnorm[layer],
                        mlp_ws[layer].astype(BF16), bs_full, w_branch[layer].astype(BF16), w_out[layer].astype(BF16),
                        g_norm2[layer], wr_split, n_lat_blocks, n_blocks, is_moe)
        x_new, h2 = merged[0], merged[1]

        if not is_moe:
            assert not last
            xc = _ffn(x_new, h2, mod3, ffn_wg[j].astype(BF16), ffn_wu[j].astype(BF16), ffn_wd[j].astype(BF16),
                      n_lat_blocks)
            x_lat, x_ctx, ctx_block0 = xc, xc, n_lat_blocks
        else:
            assert last
            n = bsz * n_lat
            logits, logits_t = merged[2], merged[3]
            dest1, dest2, src_token, tile_expert, n_tiles = _route(logits_t[:N_EXPERTS], MOE_TILE)
            x_sorted = jnp.take(h2.reshape(n, d), src_token, axis=0, mode="clip")
            y = _moe_experts(x_sorted, tile_expert, n_tiles, moe_wg[j].astype(BF16), moe_wu[j].astype(BF16),
                             moe_wd[j].astype(BF16))
            y0 = jnp.take(y, dest1, axis=0, mode="clip").reshape(bsz, n_lat, d)
            y1 = jnp.take(y, dest2, axis=0, mode="clip").reshape(bsz, n_lat, d)
            out = _final(x_new, y0, y1, logits, mod3, g_final)
    return out
```

```python
import functools
import math

import jax
import jax.numpy as jnp
from jax import lax
from jax.experimental import pallas as pl
from jax.experimental.pallas import tpu as pltpu

F32 = jnp.float32
BF16 = jnp.bfloat16

EPS = 1e-6
GRID_W = 64
ROPE_BASE = 10000.0

A_HEADS = 4
A_CHUNK = 64
A_SUB = 16
HGRN_ROWS = 256
MOE_TILE = 512
B_GROUPS = 4
B_CHUNK = 128
C_HEADS = 4
C_DH = 64
HEAD_W = 128
BRANCH_W = 512
N_BRANCH = 3
N_EXPERTS = 8
TOP_K = 2

BF16_ROWS = 16
ROW_TILE = 256
MOD_ROWS = 16
ATT_TQ = 512
ATT_TK = 512
FF_CHUNK = 512
EXP_CLAMP = 80.0
LOG2_E = 1.4426950408889634
NEG_BIG = -1e30
V7X_VMEM_LIMIT = 56 * 1024 * 1024


def _cparams(sem, vmem=None):
    return pltpu.CompilerParams(dimension_semantics=sem, vmem_limit_bytes=vmem)


def _resident(shape, index_map):
    return pl.BlockSpec(shape, index_map, pipeline_mode=pl.Buffered(1))


def _lat_ctx_specs(tm, d, n_lat_blocks, ctx_block0):
    lat = pl.BlockSpec((1, tm, d), lambda b, j: (b, jnp.minimum(j, n_lat_blocks - 1), 0))
    ctx = pl.BlockSpec((1, tm, d), lambda b, j: (b, ctx_block0 + jnp.maximum(j - n_lat_blocks, 0), 0))
    return lat, ctx


def _rms(xf):
    return xf * lax.rsqrt(jnp.mean(xf * xf, axis=-1, keepdims=True) + EPS)


def _sigmoid(x):
    return 1.0 / (1.0 + jnp.exp(-x))


def _dot(a, b):
    return jnp.dot(a, b, preferred_element_type=F32)


def _dot_nt(a, b):
    return lax.dot_general(a, b, (((1,), (1,)), ((), ())), preferred_element_type=F32)


def _dot_tn(a, b):
    return lax.dot_general(a, b, (((0,), (0,)), ((), ())), preferred_element_type=F32)


def _adaln_kernel(c_ref, w_ref, b_ref, o_ref):
    c = c_ref[...]
    a = c * _sigmoid(c)
    o_ref[0] = jnp.dot(a, w_ref[0], preferred_element_type=F32, precision=lax.Precision.HIGHEST) + b_ref[0]


def _adaln(cvec, w_ada, b_ada):
    depth, d, n = w_ada.shape
    tn = 1536
    return pl.pallas_call(
        _adaln_kernel,
        grid=(depth, n // tn),
        in_specs=[pl.BlockSpec((MOD_ROWS, d), lambda l, j: (0, 0)),
                  pl.BlockSpec((1, d, tn), lambda l, j: (l, 0, j)),
                  pl.BlockSpec((1, 1, tn), lambda l, j: (l, 0, j))],
        out_specs=pl.BlockSpec((1, MOD_ROWS, tn), lambda l, j: (l, 0, j)),
        out_shape=jax.ShapeDtypeStruct((depth, MOD_ROWS, n), F32),
        compiler_params=_cparams(("arbitrary", "arbitrary")),
        name="adaln",
    )(cvec, w_ada, b_ada.reshape(depth, 1, n))


_IN_COLS = (("qa", 512), ("zf", 512), ("zb", 512), ("ia", 512), ("og", 512), ("uv", 1024), ("qc", 512), ("kc", 512),
            ("vc", 512), ("gt", 3072))
_IN_OUTS = (("qa", 512, BF16), ("bf", 512, F32), ("kf", 512, BF16), ("bb", 512, F32), ("kb", 512, BF16),
            ("ia", 512, BF16), ("og", 512, BF16), ("uv", 1024, BF16), ("qc", 512, BF16), ("kc", 512, BF16),
            ("vc", 512, BF16), ("gt", 3072, BF16))


def _gelu_tanh(x):
    return 0.5 * x * (1.0 + jnp.tanh(math.sqrt(2.0 / math.pi) * (x + 0.044715 * (x * x * x))))


_IN_ACTIVATIONS = {"og": lambda p: p * _sigmoid(p), "uv": _gelu_tanh, "gt": _sigmoid}


def _forget_gate(z, lb):
    sp = jnp.maximum(-z, 0.0) + jnp.log(1.0 + jnp.exp(-jnp.abs(z)))
    la = jnp.log(lb)
    lc = jnp.log1p(-lb) - sp
    logf = jnp.maximum(la, lc) + jnp.log(1.0 + jnp.exp(-jnp.abs(la - lc)))
    kin = (1.0 - lb) * jnp.exp(-z - sp)
    hi = logf.astype(BF16)
    rest = logf - hi.astype(F32)
    mid = rest.astype(BF16)
    low = (rest - mid.astype(F32)).astype(BF16)
    return (hi, mid, low), kin


def _inproj_kernel(xl_ref, xc_ref, mod_ref, g_ref, w_ref, cos_ref, sin_ref, lbf_ref, lbb_ref, trif_ref, trib_ref,
                   *out_refs, d, n_lat_blocks):
    x = jnp.where(pl.program_id(1) < n_lat_blocks, xl_ref[0], xc_ref[0])
    mod = mod_ref[0]
    h = _rms(x) * g_ref[...]
    h = (h * (1.0 + mod[:, d:2 * d]) + mod[:, 0:d]).astype(BF16)

    tm = x.shape[0]
    lane = lax.broadcasted_iota(jnp.int32, (tm, 512), 1)
    low_half = (lane % 32) < 16

    def rope(p):
        partner = jnp.where(low_half, pltpu.roll(p, 512 - 16, 1), pltpu.roll(p, 16, 1))
        return p * cos_ref[...] + partner * sin_ref[...]

    outs = {name: ref for (name, _, _), ref in zip(_IN_OUTS, out_refs)}
    col0 = {}
    lo = 0
    for name, width in _IN_COLS:
        col0[name] = lo
        lo += width

    gates = {}
    for name, lb_ref in (("zf", lbf_ref), ("zb", lbb_ref)):
        terms, kin = _forget_gate(_dot(h, w_ref[0, :, col0[name]:col0[name] + 512]), lb_ref[...])
        outs["k" + name[1]][0] = kin.astype(BF16)
        gates[name] = terms
    for name, width in _IN_COLS:
        if name in gates:
            continue
        for c0 in range(0, width, 512):
            p = _dot(h, w_ref[0, :, col0[name] + c0:col0[name] + c0 + 512])
            if name in ("qc", "kc"):
                p = rope(p)
            elif name in _IN_ACTIVATIONS:
                p = _IN_ACTIVATIONS[name](p)
            outs[name][0, :, c0:c0 + 512] = p.astype(outs[name].dtype)
    for name, tri_ref in (("zf", trif_ref), ("zb", trib_ref)):
        hi, mid, low = gates[name]
        outs["b" + name[1]][0] = _dot(tri_ref[...], hi) + _dot(tri_ref[...], mid) + _dot(tri_ref[...], low)


def _scan_matrices(tm):
    row = jnp.arange(tm)[:, None]
    col = jnp.arange(tm)[None, :]
    same = (row // A_CHUNK) == (col // A_CHUNK)
    return (same & (col <= row)).astype(BF16), (same & (col >= row)).astype(BF16)


def _inproj(x_lat, x_ctx, ctx_block0, t, mod3, g, w_bf, layer, cos_t, sin_t, lb_f, lb_b, n_lat_blocks):
    bsz, _, d = x_lat.shape
    tm = ROW_TILE
    ncols = w_bf.shape[2]
    ctx_row = bsz

    def mod_map(b, j):
        return (jnp.where(j < n_lat_blocks, b, ctx_row), 0, 0)

    out_shape = [jax.ShapeDtypeStruct((bsz, t, w), dt) for _, w, dt in _IN_OUTS]
    out_specs = [pl.BlockSpec((1, tm, w), lambda b, j: (b, j, 0)) for _, w, _ in _IN_OUTS]
    tri_f, tri_b = _scan_matrices(tm)
    vec = pl.BlockSpec((1, 512), lambda b, j: (0, 0))
    lat_spec, ctx_spec = _lat_ctx_specs(tm, d, n_lat_blocks, ctx_block0)
    outs = pl.pallas_call(
        functools.partial(_inproj_kernel, d=d, n_lat_blocks=n_lat_blocks),
        grid=(bsz, t // tm),
        in_specs=[lat_spec, ctx_spec,
                  pl.BlockSpec((1, 1, mod3.shape[2]), mod_map),
                  pl.BlockSpec((1, d), lambda b, j: (0, 0)),
                  _resident((1, d, ncols), lambda b, j: (layer, 0, 0)),
                  pl.BlockSpec((tm, 512), lambda b, j: (j, 0)),
                  pl.BlockSpec((tm, 512), lambda b, j: (j, 0)),
                  vec, vec, _resident((tm, tm), lambda b, j: (0, 0)), _resident((tm, tm), lambda b, j: (0, 0))],
        out_specs=out_specs,
        out_shape=out_shape,
        compiler_params=_cparams(("arbitrary", "arbitrary"), V7X_VMEM_LIMIT),
        name="inproj",
    )(x_lat, x_ctx, mod3, g.reshape(1, d), w_bf, cos_t, sin_t, lb_f.reshape(1, 512), lb_b.reshape(1, 512), tri_f, tri_b)
    return {name: o for (name, _, _), o in zip(_IN_OUTS, outs)}


def _hgrn_direction(q_ref, b_ref, k_ref, v_ref, o_ref, st_ref, reverse, rows):
    c = A_CHUNK
    row = lax.broadcasted_iota(jnp.int32, (c, c), 0)
    col = lax.broadcasted_iota(jnp.int32, (c, c), 1)
    allowed = (col >= row) if reverse else (col <= row)

    b = b_ref[0, rows, :]
    tot_row = 0 if reverse else c - 1
    tot = b[tot_row:tot_row + 1, :]

    q = q_ref[0, rows, :].astype(F32)
    kin = k_ref[0, rows, :].astype(F32)
    v = v_ref[0, rows, :]
    zero_blk = jnp.zeros((A_SUB, HEAD_W), F32)
    heads = []
    for h in range(A_HEADS):
        hs = slice(h * HEAD_W, (h + 1) * HEAD_W)
        bh, qh, kh = b[:, hs], q[:, hs], kin[:, hs]
        q_state = qh * jnp.exp(bh)
        k_state = kh * jnp.exp(tot[:, hs] - bh)
        q_rows, k_cols = [], []
        for i in range(c // A_SUB):
            r0 = i * A_SUB
            ref = r0 + A_SUB - 1 if reverse else r0
            r = bh[ref:ref + 1, :]
            qt = qh[r0:r0 + A_SUB] * jnp.exp(bh[r0:r0 + A_SUB] - r)
            q_rows.append(jnp.concatenate([qt if j == i else zero_blk for j in range(c // A_SUB)], axis=1))
            live = slice(r0, c) if reverse else slice(0, r0 + A_SUB)
            kt = kh[live] * jnp.exp(jnp.minimum(r - bh[live], EXP_CLAMP))
            dead = jnp.zeros((c - kt.shape[0], HEAD_W), F32)
            k_cols.append(jnp.concatenate([dead, kt] if reverse else [kt, dead], axis=0) if dead.shape[0] else kt)
        heads.append(dict(
            hs=hs, h=h, rows=rows, allowed=allowed, vh=v[:, hs], decay=jnp.exp(tot[:, hs]), o_ref=o_ref, st_ref=st_ref,
            q_wide=jnp.concatenate(q_rows, axis=0).astype(BF16),
            k_wide=jnp.concatenate(k_cols, axis=1).astype(BF16),
            q_state=q_state.astype(BF16), k_state=k_state.astype(BF16)))
    return heads


def _hgrn_kernel(qf_ref, bf_ref, kf_ref, vf_ref, qb_ref, bb_ref, kb_ref, vb_ref, of_ref, ob_ref, sf_ref, sb_ref):
    @pl.when(pl.program_id(1) == 0)
    def _():
        sf_ref[...] = jnp.zeros_like(sf_ref)
        sb_ref[...] = jnp.zeros_like(sb_ref)

    n_sub = qf_ref.shape[1] // A_CHUNK
    seq = []
    for i in range(n_sub):
        up = slice(i * A_CHUNK, (i + 1) * A_CHUNK)
        down = slice((n_sub - 1 - i) * A_CHUNK, (n_sub - i) * A_CHUNK)
        seq.append(_hgrn_direction(qf_ref, bf_ref, kf_ref, vf_ref, of_ref, sf_ref, reverse=False, rows=up)
                   + _hgrn_direction(qb_ref, bb_ref, kb_ref, vb_ref, ob_ref, sb_ref, reverse=True, rows=down))
    for chains in seq:
        for ch in chains:
            ch["att"] = _dot_nt(ch["q_wide"], ch["k_wide"])
    state = [ch["st_ref"][ch["h"]] for ch in seq[0]]
    for chains in seq:
        for ch, st in zip(chains, state):
            ch["inter"] = _dot_nt(ch["q_state"], st.astype(BF16))
            ch["upd"] = _dot_tn(ch["vh"], ch["k_state"])
        for n, ch in enumerate(chains):
            att = jnp.where(ch["allowed"], ch["att"], 0.0).astype(BF16)
            o = _dot(att, ch["vh"]) + ch["inter"]
            ch["o_ref"][0, ch["rows"], ch["hs"]] = o.astype(ch["o_ref"].dtype)
            state[n] = state[n] * ch["decay"] + ch["upd"]
    for ch, st in zip(seq[0], state):
        ch["st_ref"][ch["h"]] = st


def _hgrn(qa, bf, kf, bb, kb, ia, n_lat):
    bsz, t, w = qa.shape
    c = HGRN_ROWS
    n_chunks = t // c
    lat_chunks = n_lat // c

    def fwd(b, s):
        return (b, (s + lat_chunks) % n_chunks, 0)

    def bwd(b, s):
        return (b, n_chunks - 1 - s, 0)

    blk = (1, c, w)
    return pl.pallas_call(
        _hgrn_kernel,
        grid=(bsz, n_chunks),
        in_specs=[pl.BlockSpec(blk, fwd)] * 4 + [pl.BlockSpec(blk, bwd)] * 4,
        out_specs=[pl.BlockSpec(blk, fwd), pl.BlockSpec(blk, bwd)],
        out_shape=[jax.ShapeDtypeStruct((bsz, t, w), BF16)] * 2,
        scratch_shapes=[pltpu.VMEM((A_HEADS, HEAD_W, HEAD_W), F32)] * 2,
        compiler_params=_cparams(("arbitrary", "arbitrary")),
        name="hgrn2",
    )(qa, bf, kf, ia, qa, bb, kb, ia)


def _attn_kernel(q_ref, k_ref, v_ref, lam_ref, sub_ref, *rest, key_blocks, lam_init, n_cast):
    cast_in, (o_ref, *cast_out), (v1_ref, m_ref, acc_ref) = rest[:n_cast], rest[n_cast:2 * n_cast + 1], rest[-3:]
    for src, dst in zip(cast_in, cast_out):
        dst[...] = src[...].astype(BF16)
    tq = q_ref.shape[1]

    @pl.when(pl.program_id(2) == 0)
    def _():
        v1_ref[:, :HEAD_W] = v_ref[0]
        v1_ref[:, HEAD_W:] = jnp.ones((v1_ref.shape[0], HEAD_W), BF16)

    lane = lax.broadcasted_iota(jnp.int32, (tq, HEAD_W), 1)
    qf = q_ref[0].astype(F32) * (C_DH ** -0.5 * LOG2_E)
    q2 = jnp.concatenate([jnp.where(lane < C_DH, qf, 0.0), jnp.where(lane >= C_DH, qf, 0.0)], axis=0).astype(BF16)

    m_ref[...] = jnp.full_like(m_ref, NEG_BIG)
    acc_ref[...] = jnp.zeros_like(acc_ref)

    def scores(start, size):
        return _dot_nt(q2, k_ref[0, pl.ds(start, size), :])

    def absorb(s, start, size):
        m_old = m_ref[...]
        m_new = jnp.maximum(m_old, jnp.max(s, axis=-1, keepdims=True))
        alpha = jnp.exp2(m_old - m_new)
        p = jnp.exp2(s - jnp.concatenate([m_new] * (size // HEAD_W), axis=1))
        acc_ref[...] = (jnp.concatenate([alpha, alpha], axis=1) * acc_ref[...]
                        + _dot(p.astype(BF16), v1_ref[pl.ds(start, size), :]))
        m_ref[...] = m_new

    cur = scores(*key_blocks[0])
    for n, blk in enumerate(key_blocks):
        nxt = scores(*key_blocks[n + 1]) if n + 1 < len(key_blocks) else None
        absorb(cur, *blk)
        cur = nxt

    o12 = acc_ref[:, :HEAD_W] / acc_ref[:, HEAD_W:]
    o = o12[:tq] - lam_ref[...] * o12[tq:]
    o_ref[0] = (_rms(o) * sub_ref[...] * (1.0 - lam_init)).astype(o_ref.dtype)


def _attention(qc, kc, vc, lam, subln, lam_init, tq, q_block0, n_q, kv_block, key_blocks, to_bf16=()):
    bsz, _, w = qc.shape
    kv_rows, kv_idx = kv_block
    n_steps = bsz * C_HEADS * n_q
    cast_specs = []
    for a in to_bf16:
        rows = a.shape[0] // n_steps
        assert a.shape[0] == rows * n_steps and rows % BF16_ROWS == 0
        cast_specs.append(pl.BlockSpec((rows, a.shape[1]), lambda b, h, i: ((b * C_HEADS + h) * n_q + i, 0)))
    outs = pl.pallas_call(
        functools.partial(_attn_kernel, key_blocks=key_blocks, lam_init=lam_init, n_cast=len(to_bf16)),
        grid=(bsz, C_HEADS, n_q),
        in_specs=[pl.BlockSpec((1, tq, HEAD_W), lambda b, h, i: (b, q_block0 + i, h)),
                  pl.BlockSpec((1, kv_rows, HEAD_W), lambda b, h, i: (b, kv_idx, h)),
                  pl.BlockSpec((1, kv_rows, HEAD_W), lambda b, h, i: (b, kv_idx, h)),
                  pl.BlockSpec((1, HEAD_W), lambda b, h, i: (0, 0)),
                  pl.BlockSpec((1, HEAD_W), lambda b, h, i: (0, h))] + cast_specs,
        out_specs=[pl.BlockSpec((1, tq, HEAD_W), lambda b, h, i: (b, i, h))] + cast_specs,
        out_shape=[jax.ShapeDtypeStruct((bsz, n_q * tq, w), BF16)]
        + [jax.ShapeDtypeStruct(a.shape, BF16) for a in to_bf16],
        scratch_shapes=[pltpu.VMEM((kv_rows, 2 * HEAD_W), BF16), pltpu.VMEM((2 * tq, HEAD_W), F32),
                        pltpu.VMEM((2 * tq, 2 * HEAD_W), F32)],
        compiler_params=_cparams(("arbitrary", "arbitrary", "arbitrary")),
        name="diff_attention",
    )(qc, kc, vc, jnp.full((1, HEAD_W), lam, F32), subln.reshape(1, w), *to_bf16)
    return outs[0], outs[1:]


def _merge_kernel(xl_ref, xc_ref, mod_ref, of_ref, ob_ref, og_ref, gn_ref, uv_ref, vn_ref, ws_ref, bs_ref, ycl_ref,
                  ycc_ref, gt_ref, wb_ref, wo_ref, g2_ref, wr_ref, x_out_ref, h_out_ref, *maybe_logits_ref, d, n_lat_blocks):
    tm = xl_ref.shape[1]
    is_lat = pl.program_id(1) < n_lat_blocks
    mod = mod_ref[0]
    groups = [slice(r0, r0 + B_CHUNK) for r0 in range(0, tm, B_CHUNK)]
    st = [dict() for _ in groups]

    for s, rows in zip(st, groups):
        o = of_ref[0, rows, :].astype(F32) + ob_ref[0, rows, :].astype(F32)
        o = jnp.concatenate([_rms(o[:, h * HEAD_W:(h + 1) * HEAD_W]) for h in range(A_HEADS)], axis=1)
        s["ya"] = (o * gn_ref[...] * og_ref[0, rows, :].astype(F32)).astype(BF16)
        uv = uv_ref[0, rows, :].astype(F32)
        s["u"] = uv[:, :BRANCH_W]
        s["vv"] = (_rms(uv[:, BRANCH_W:]) * vn_ref[...]).astype(BF16)
    for s in st:
        s["mixed"] = jnp.concatenate(
            [_dot(ws_ref[g], s["vv"][:, g * HEAD_W:(g + 1) * HEAD_W]) for g in range(B_GROUPS)], axis=1)
    for s, rows in zip(st, groups):
        yb = (s["u"] * (s["mixed"] + bs_ref[...])).astype(BF16)
        yc = jnp.where(is_lat, ycl_ref[0, rows, :], ycc_ref[0, rows, :])
        s["proj"] = [_dot(y, wb_ref[i]) for i, y in enumerate((s["ya"], yb, yc))]
    for s, rows in zip(st, groups):
        merged = None
        for i in range(N_BRANCH):
            term = gt_ref[0, rows, i * d:(i + 1) * d].astype(F32) * s["proj"][i]
            merged = term if merged is None else merged + term
        s["mix"] = _dot(merged.astype(BF16), wo_ref[...])
    for s, rows in zip(st, groups):
        x_old = jnp.where(is_lat, xl_ref[0, rows, :], xc_ref[0, rows, :])
        x_new = x_old + mod[:, 2 * d:3 * d] * s["mix"]
        x_out_ref[0, rows, :] = x_new
        h = _rms(x_new) * g2_ref[...]
        h = h * (1.0 + mod[:, 4 * d:5 * d]) + mod[:, 3 * d:4 * d]
        h_hi = h.astype(BF16)
        h_out_ref[0, rows, :] = h_hi
        if maybe_logits_ref:
            h_lo = (h - h_hi.astype(F32)).astype(BF16)
            lg = (_dot(h_hi, wr_ref[0]) + _dot(h_lo, wr_ref[0])) + _dot(h_hi, wr_ref[1])
            maybe_logits_ref[0][0, rows, :] = lg
            maybe_logits_ref[1][:, rows] = lg.T


def _merge(x_lat, x_ctx, ctx_block0, mod3, proj, of, ob, yc_lat, yc_ctx, gnorm, vnorm, ws_bf, bs_full, wb_bf, wo_bf, g2, wr_split,
           n_lat_blocks, n_blocks, with_logits):
    bsz, _, d = x_lat.shape
    tm = ROW_TILE
    ctx_row = bsz

    def rows(w):
        return pl.BlockSpec((1, tm, w), lambda b, j: (b, j, 0))

    def const2(shape):
        return pl.BlockSpec(shape, lambda b, j: (0, 0))

    def mod_map(b, j):
        return (jnp.where(j < n_lat_blocks, b, ctx_row), 0, 0)

    n_rows = n_blocks * tm
    out_shape = [jax.ShapeDtypeStruct((bsz, n_rows, d), F32), jax.ShapeDtypeStruct((bsz, n_rows, d), BF16)]
    out_specs = [rows(d), rows(d)]
    if with_logits:
        out_shape += [jax.ShapeDtypeStruct((bsz, n_rows, HEAD_W), F32),
                      jax.ShapeDtypeStruct((HEAD_W, bsz * n_rows), F32)]
        out_specs += [rows(HEAD_W), pl.BlockSpec((HEAD_W, tm), lambda b, j: (0, b * n_blocks + j))]
    lat_spec, ctx_spec = _lat_ctx_specs(tm, d, n_lat_blocks, ctx_block0)
    return pl.pallas_call(
        functools.partial(_merge_kernel, d=d, n_lat_blocks=n_lat_blocks),
        grid=(bsz, n_blocks),
        in_specs=[lat_spec, ctx_spec, pl.BlockSpec((1, 1, mod3.shape[2]), mod_map),
                  rows(BRANCH_W), rows(BRANCH_W), rows(BRANCH_W), const2((1, BRANCH_W)),
                  rows(2 * BRANCH_W), const2((1, BRANCH_W)),
                  _resident((B_GROUPS, B_CHUNK, B_CHUNK), lambda b, j: (0, 0, 0)), const2((B_CHUNK, BRANCH_W)),
                  *_lat_ctx_specs(tm, BRANCH_W, n_lat_blocks, 0), rows(N_BRANCH * d),
                  _resident((N_BRANCH, BRANCH_W, d), lambda b, j: (0, 0, 0)), _resident((d, d), lambda b, j: (0, 0)),
                  const2((1, d)), _resident((2, d, HEAD_W), lambda b, j: (0, 0, 0))],
        out_specs=out_specs,
        out_shape=out_shape,
        compiler_params=_cparams(("arbitrary", "arbitrary"), V7X_VMEM_LIMIT),
        name="merge",
    )(x_lat, x_ctx, mod3, of, ob, proj["og"], gnorm.reshape(1, BRANCH_W), proj["uv"], vnorm.reshape(1, BRANCH_W),
      ws_bf, bs_full, yc_lat, yc_ctx, proj["gt"], wb_bf, wo_bf, g2.reshape(1, d), wr_split)


def _ffn_kernel(x_ref, h_ref, mod_ref, wg_ref, wu_ref, wd_ref, o_ref, *, d):
    h = h_ref[0]
    g = _dot(h, wg_ref[...])
    a = (g * _sigmoid(g) * _dot(h, wu_ref[...])).astype(BF16)
    y = _dot(a, wd_ref[...])
    o_ref[0] = x_ref[0] + mod_ref[0][:, 5 * d:6 * d] * y


def _ffn(xc, h2, mod3, wg_bf, wu_bf, wd_bf, n_lat_blocks):
    bsz, t, d = xc.shape
    tm = ROW_TILE
    dff = wg_bf.shape[1]
    ctx_row = bsz

    def mod_map(b, j):
        return (jnp.where(j < n_lat_blocks, b, ctx_row), 0, 0)

    rows = pl.BlockSpec((1, tm, d), lambda b, j: (b, j, 0))
    return pl.pallas_call(
        functools.partial(_ffn_kernel, d=d),
        grid=(bsz, t // tm),
        in_specs=[rows, rows, pl.BlockSpec((1, 1, mod3.shape[2]), mod_map),
                  _resident((d, dff), lambda b, j: (0, 0)), _resident((d, dff), lambda b, j: (0, 0)),
                  _resident((dff, d), lambda b, j: (0, 0))],
        out_specs=rows,
        out_shape=jax.ShapeDtypeStruct((bsz, t, d), F32),
        compiler_params=_cparams(("arbitrary", "arbitrary"), V7X_VMEM_LIMIT),
        name="dense_ffn",
    )(xc, h2, mod3, wg_bf, wu_bf, wd_bf)


def _moe_kernel(tile_expert_ref, n_tiles_ref, x_ref, wg_ref, wu_ref, wd_ref, o_ref):
    del tile_expert_ref

    @pl.when(pl.program_id(0) < n_tiles_ref[0])
    def _():
        x = x_ref[...]
        dff = wg_ref.shape[2]
        acc = None
        for c0 in range(0, dff, FF_CHUNK):
            g = _dot(x, wg_ref[0, :, c0:c0 + FF_CHUNK])
            a = (g * _sigmoid(g) * _dot(x, wu_ref[0, :, c0:c0 + FF_CHUNK])).astype(BF16)
            y = _dot(a, wd_ref[0, c0:c0 + FF_CHUNK, :])
            acc = y if acc is None else acc + y
        o_ref[...] = acc.astype(o_ref.dtype)

    @pl.when(pl.program_id(0) >= n_tiles_ref[0])
    def _():
        o_ref[...] = jnp.zeros_like(o_ref)


def _moe_experts(x_sorted, tile_expert, n_tiles, wg_bf, wu_bf, wd_bf):
    p, d = x_sorted.shape
    tm = MOE_TILE
    dff = wg_bf.shape[2]
    grid_spec = pltpu.PrefetchScalarGridSpec(
        num_scalar_prefetch=2,
        grid=(p // tm,),
        in_specs=[pl.BlockSpec((tm, d), lambda t, te, nt: (t, 0)),
                  pl.BlockSpec((1, d, dff), lambda t, te, nt: (te[t], 0, 0)),
                  pl.BlockSpec((1, d, dff), lambda t, te, nt: (te[t], 0, 0)),
                  pl.BlockSpec((1, dff, d), lambda t, te, nt: (te[t], 0, 0))],
        out_specs=pl.BlockSpec((tm, d), lambda t, te, nt: (t, 0)),
    )
    return pl.pallas_call(
        _moe_kernel,
        grid_spec=grid_spec,
        out_shape=jax.ShapeDtypeStruct((p, d), BF16),
        compiler_params=_cparams(("arbitrary",), V7X_VMEM_LIMIT),
        name="moe_experts",
    )(tile_expert, n_tiles, x_sorted, wg_bf, wu_bf, wd_bf)


def _route(logits_t, tm):
    n = logits_t.shape[1]
    experts = jnp.arange(N_EXPERTS, dtype=jnp.int32)[:, None]
    e1 = jnp.argmax(logits_t, axis=0).astype(jnp.int32)
    e2 = jnp.argmax(jnp.where(experts == e1[None, :], -jnp.inf, logits_t), axis=0).astype(jnp.int32)
    oh1 = (experts == e1[None, :]).astype(jnp.int32)
    oh2 = (experts == e2[None, :]).astype(jnp.int32)
    c1 = jnp.cumsum(oh1, axis=1)
    c2 = jnp.cumsum(oh2, axis=1) + c1[:, -1:]
    counts = c2[:, -1]
    padded = ((counts + tm - 1) // tm) * tm
    ends = jnp.cumsum(padded)
    starts = ends - padded
    dest1 = jnp.sum(oh1 * (starts[:, None] + c1 - 1), axis=0)
    dest2 = jnp.sum(oh2 * (starts[:, None] + c2 - 1), axis=0)
    m = TOP_K * n
    p = m + N_EXPERTS * tm
    tile_start = jnp.arange(p // tm, dtype=jnp.int32) * tm
    tile_expert = jnp.minimum(jnp.sum((tile_start[:, None] >= ends[None, :]).astype(jnp.int32), axis=1),
                              N_EXPERTS - 1).astype(jnp.int32)
    n_tiles = (ends[-1] // tm).astype(jnp.int32).reshape(1)
    token = jnp.arange(n, dtype=jnp.int32)
    by_expert = jnp.sort(jnp.concatenate([e1 * m + token, e2 * m + n + token])) % m
    first = jnp.cumsum(counts) - counts
    row_expert = jnp.repeat(tile_expert, tm)
    pos = jnp.arange(p, dtype=jnp.int32)
    entry = jnp.take(by_expert, jnp.clip(first[row_expert] + pos - starts[row_expert], 0, m - 1))
    src_token = jnp.where(entry < n, entry, entry - n)
    return dest1, dest2, src_token, tile_expert, n_tiles


def _final_kernel(x_ref, y0_ref, y1_ref, lg_ref, mod_ref, g_ref, o_ref, *, d):
    lane = lax.broadcasted_iota(jnp.int32, lg_ref.shape[1:], 1)
    lg = jnp.where(lane < N_EXPERTS, lg_ref[0], NEG_BIG)
    m1 = jnp.max(lg, axis=-1, keepdims=True)
    first = jnp.min(jnp.where(lg == m1, lane, HEAD_W), axis=-1, keepdims=True)
    m2 = jnp.max(jnp.where(lane == first, NEG_BIG, lg), axis=-1, keepdims=True)
    e = jnp.exp(m2 - m1)
    w1 = 1.0 / (1.0 + e)
    y = w1 * y0_ref[0].astype(F32) + (e * w1) * y1_ref[0].astype(F32)
    x = x_ref[0] + mod_ref[0][:, 5 * d:6 * d] * y
    o_ref[0] = _rms(x) * g_ref[...]


def _final(x_lat, y0, y1, logits, mod3, g_final):
    bsz, s, d = x_lat.shape
    tm = ROW_TILE
    rows = pl.BlockSpec((1, tm, d), lambda b, j: (b, j, 0))
    return pl.pallas_call(
        functools.partial(_final_kernel, d=d),
        grid=(bsz, s // tm),
        in_specs=[rows, rows, rows, pl.BlockSpec((1, tm, HEAD_W), lambda b, j: (b, j, 0)),
                  pl.BlockSpec((1, 1, mod3.shape[2]), lambda b, j: (b, 0, 0)),
                  pl.BlockSpec((1, d), lambda b, j: (0, 0))],
        out_specs=rows,
        out_shape=jax.ShapeDtypeStruct((bsz, s, d), F32),
        compiler_params=_cparams(("arbitrary", "arbitrary")),
        name="moe_combine_final_norm",
    )(x_lat, y0, y1, logits, mod3, g_final.reshape(1, d))


def _rope_tables(n_lat, n_ctx):
    pairs = C_DH // 4
    tpos = jnp.arange(n_lat)
    pos = jnp.stack([(tpos // GRID_W).astype(F32), (tpos % GRID_W).astype(F32)], axis=1)
    freqs = ROPE_BASE ** (-jnp.arange(pairs, dtype=F32) / pairs)
    ang = pos[:, :, None] * freqs[None, None, :]
    cos = jnp.repeat(jnp.cos(ang)[:, :, None, :], 2, axis=2).reshape(n_lat, C_DH)
    sin = jnp.sin(ang)[:, :, None, :] * jnp.array([-1.0, 1.0], F32)[None, None, :, None]
    sin = sin.reshape(n_lat, C_DH)
    cos = jnp.concatenate([cos, jnp.ones((n_ctx, C_DH), F32)], axis=0)
    sin = jnp.concatenate([sin, jnp.zeros((n_ctx, C_DH), F32)], axis=0)
    reps = 512 // C_DH
    return jnp.tile(cos, (1, reps)), jnp.tile(sin, (1, reps))


def _lower_bound(p, layer):
    cs = jnp.cumsum(jax.nn.softmax(p.astype(F32), axis=0), axis=0)
    return cs[layer] - cs[0]


def kernel(x, c, ctx, c_ctx, w_ada, b_ada, g_norm1, g_norm2, w_in, hgrn_lb, hgrn_gnorm, mlp_vnorm, mlp_ws, mlp_bs,
           diff_lambda, diff_subln, w_branch, w_out, ffn_wg, ffn_wu, ffn_wd, moe_router, moe_wg, moe_wu, moe_wd,
           g_final):
    bsz, n_lat, d = x.shape
    n_ctx = ctx.shape[1]
    depth = w_ada.shape[0]
    t = n_lat + n_ctx
    tm = ROW_TILE
    assert depth == 2 and bsz < MOD_ROWS and n_lat % GRID_W == 0
    assert n_lat % ATT_TQ == 0 and n_lat % ATT_TK == 0 and n_lat % n_ctx == 0 and n_ctx % tm == 0
    n_lat_blocks = n_lat // tm

    cvec = jnp.zeros((MOD_ROWS, d), F32).at[:bsz].set(c).at[bsz].set(c_ctx)
    mods = _adaln(cvec, w_ada, b_ada)
    cos_t, sin_t = _rope_tables(n_lat, n_ctx)
    x_lat, x_ctx, ctx_block0 = x, ctx, 0
    w_in_bf = w_in.astype(BF16)

    out = None
    for layer in range(depth):
        last = layer == depth - 1
        mod3 = mods[layer].reshape(MOD_ROWS, 1, 6 * d)
        lb_f = _lower_bound(hgrn_lb[0], layer)
        lb_b = _lower_bound(hgrn_lb[1], layer)
        proj = _inproj(x_lat, x_ctx, ctx_block0, t, mod3, g_norm1[layer], w_in_bf, layer, cos_t, sin_t,
                       lb_f, lb_b, n_lat_blocks)
        of, ob = _hgrn(proj["qa"], proj["bf"], proj["kf"], proj["bb"], proj["kb"], proj["ia"], n_lat)

        lam_init = 0.8 - 0.6 * math.exp(-0.3 * layer)
        lam_p = diff_lambda[layer]
        lam = jnp.exp(jnp.sum(lam_p[0] * lam_p[1])) - jnp.exp(jnp.sum(lam_p[2] * lam_p[3])) + lam_init
        lat_keys = tuple((i * ATT_TK, ATT_TK) for i in range(n_lat // ATT_TK)) + ((n_lat, n_ctx),)
        qkv = (proj["qc"], proj["kc"], proj["vc"], lam.astype(F32), diff_subln[layer], lam_init)
        j = layer // 2
        is_moe = layer % 2 == 1
        expert_w = ()
        if is_moe:
            n_e, _, dff = moe_wg[j].shape
            expert_w = (moe_wg[j].reshape(n_e * d, dff), moe_wu[j].reshape(n_e * d, dff),
                        moe_wd[j].reshape(n_e * dff, d))
        yc_lat, expert_w_bf = _attention(*qkv, ATT_TQ, 0, n_lat // ATT_TQ, (t, 0), lat_keys, expert_w)
        yc_ctx = yc_lat if last else _attention(*qkv, n_ctx, n_lat // n_ctx, 1, (n_ctx, n_lat // n_ctx),
                                                ((0, n_ctx),))[0]

        bs_full = jnp.repeat(mlp_bs[layer].T, HEAD_W, axis=1)
        wr_split = jnp.zeros((2, d, HEAD_W), BF16)
        if is_moe:
            wr_hi = moe_router[j].astype(BF16)
            wr_lo = (moe_router[j] - wr_hi.astype(F32)).astype(BF16)
            wr_split = wr_split.at[:, :, :N_EXPERTS].set(jnp.stack([wr_hi, wr_lo]))
        n_blocks = n_lat_blocks if last else t // tm
        merged = _merge(x_lat, x_ctx, ctx_block0, mod3, proj, of, ob, yc_lat, yc_ctx, hgrn_gnorm[layer], mlp_vnorm[layer],
                        mlp_ws[layer].astype(BF16), bs_full, w_branch[layer].astype(BF16), w_out[layer].astype(BF16),
                        g_norm2[layer], wr_split, n_lat_blocks, n_blocks, is_moe)
        x_new, h2 = merged[0], merged[1]

        if not is_moe:
            assert not last
            xc = _ffn(x_new, h2, mod3, ffn_wg[j].astype(BF16), ffn_wu[j].astype(BF16), ffn_wd[j].astype(BF16),
                      n_lat_blocks)
            x_lat, x_ctx, ctx_block0 = xc, xc, n_lat_blocks
        else:
            assert last
            n = bsz * n_lat
            logits, logits_t = merged[2], merged[3]
            dest1, dest2, src_token, tile_expert, n_tiles = _route(logits_t[:N_EXPERTS], MOE_TILE)
            x_sorted = jnp.take(h2.reshape(n, d), src_token, axis=0, mode="clip")
            y = _moe_experts(x_sorted, tile_expert, n_tiles, expert_w_bf[0].reshape(n_e, d, dff),
                             expert_w_bf[1].reshape(n_e, d, dff), expert_w_bf[2].reshape(n_e, dff, d))
            y0 = jnp.take(y, dest1, axis=0, mode="clip").reshape(bsz, n_lat, d)
            y1 = jnp.take(y, dest2, axis=0, mode="clip").reshape(bsz, n_lat, d)
            out = _final(x_new, y0, y1, logits, mod3, g_final)
    return out
```

```python
import functools
import math

import jax
import jax.numpy as jnp
from jax import lax
from jax.experimental import pallas as pl
from jax.experimental.pallas import tpu as pltpu

F32 = jnp.float32
BF16 = jnp.bfloat16

EPS = 1e-6
GRID_W = 64
ROPE_BASE = 10000.0

A_HEADS = 4
A_CHUNK = 64
A_SUB = 16
HGRN_ROWS = 256
MOE_TILE = 512
B_GROUPS = 4
B_CHUNK = 128
C_HEADS = 4
C_DH = 64
HEAD_W = 128
BRANCH_W = 512
N_BRANCH = 3
N_EXPERTS = 8
TOP_K = 2

BF16_ROWS = 16
ROW_TILE = 256
MOD_ROWS = 16
ATT_TQ = 512
ATT_TK = 512
FF_CHUNK = 512
EXP_CLAMP = 80.0
LOG2_E = 1.4426950408889634
NEG_BIG = -1e30
V7X_VMEM_LIMIT = 56 * 1024 * 1024


def _cparams(sem, vmem=None):
    return pltpu.CompilerParams(dimension_semantics=sem, vmem_limit_bytes=vmem)


def _resident(shape, index_map):
    return pl.BlockSpec(shape, index_map, pipeline_mode=pl.Buffered(1))


def _lat_ctx_specs(tm, d, n_lat_blocks, ctx_block0):
    lat = pl.BlockSpec((1, tm, d), lambda b, j: (b, jnp.minimum(j, n_lat_blocks - 1), 0))
    ctx = pl.BlockSpec((1, tm, d), lambda b, j: (b, ctx_block0 + jnp.maximum(j - n_lat_blocks, 0), 0))
    return lat, ctx


def _rms(xf):
    return xf * lax.rsqrt(jnp.mean(xf * xf, axis=-1, keepdims=True) + EPS)


def _sigmoid(x):
    return 1.0 / (1.0 + jnp.exp(-x))


def _dot(a, b):
    return jnp.dot(a, b, preferred_element_type=F32)


def _dot_nt(a, b):
    return lax.dot_general(a, b, (((1,), (1,)), ((), ())), preferred_element_type=F32)


def _dot_tn(a, b):
    return lax.dot_general(a, b, (((0,), (0,)), ((), ())), preferred_element_type=F32)


def _adaln_kernel(c_ref, w_ref, b_ref, o_ref):
    c = c_ref[...]
    a = c * _sigmoid(c)
    o_ref[0] = jnp.dot(a, w_ref[0], preferred_element_type=F32, precision=lax.Precision.HIGHEST) + b_ref[0]


def _adaln(cvec, w_ada, b_ada):
    depth, d, n = w_ada.shape
    tn = 1536
    return pl.pallas_call(
        _adaln_kernel,
        grid=(depth, n // tn),
        in_specs=[pl.BlockSpec((MOD_ROWS, d), lambda l, j: (0, 0)),
                  pl.BlockSpec((1, d, tn), lambda l, j: (l, 0, j)),
                  pl.BlockSpec((1, 1, tn), lambda l, j: (l, 0, j))],
        out_specs=pl.BlockSpec((1, MOD_ROWS, tn), lambda l, j: (l, 0, j)),
        out_shape=jax.ShapeDtypeStruct((depth, MOD_ROWS, n), F32),
        compiler_params=_cparams(("arbitrary", "arbitrary")),
        name="adaln",
    )(cvec, w_ada, b_ada.reshape(depth, 1, n))


_IN_COLS = (("qa", 512), ("zf", 512), ("zb", 512), ("ia", 512), ("og", 512), ("uv", 1024), ("qc", 512), ("kc", 512),
            ("vc", 512), ("gt", 3072))
_IN_OUTS = (("qa", 512, BF16), ("bf", 512, F32), ("kf", 512, BF16), ("bb", 512, F32), ("kb", 512, BF16),
            ("ia", 512, BF16), ("og", 512, BF16), ("uv", 1024, BF16), ("qc", 512, BF16), ("kc", 512, BF16),
            ("vc", 512, BF16), ("gt", 3072, BF16))


def _gelu_tanh(x):
    return 0.5 * x * (1.0 + jnp.tanh(math.sqrt(2.0 / math.pi) * (x + 0.044715 * (x * x * x))))


_IN_ACTIVATIONS = {"og": lambda p: p * _sigmoid(p), "uv": _gelu_tanh, "gt": _sigmoid}


def _forget_gate(z, lb):
    sp = jnp.maximum(-z, 0.0) + jnp.log(1.0 + jnp.exp(-jnp.abs(z)))
    la = jnp.log(lb)
    lc = jnp.log1p(-lb) - sp
    logf = jnp.maximum(la, lc) + jnp.log(1.0 + jnp.exp(-jnp.abs(la - lc)))
    kin = (1.0 - lb) * jnp.exp(-z - sp)
    hi = logf.astype(BF16)
    rest = logf - hi.astype(F32)
    mid = rest.astype(BF16)
    low = (rest - mid.astype(F32)).astype(BF16)
    return (hi, mid, low), kin


def _inproj_kernel(xl_ref, xc_ref, mod_ref, g_ref, w_ref, cos_ref, sin_ref, lbf_ref, lbb_ref, trif_ref, trib_ref,
                   *out_refs, d, n_lat_blocks):
    x = jnp.where(pl.program_id(1) < n_lat_blocks, xl_ref[0], xc_ref[0])
    mod = mod_ref[0]
    h = _rms(x) * g_ref[...]
    h = (h * (1.0 + mod[:, d:2 * d]) + mod[:, 0:d]).astype(BF16)

    tm = x.shape[0]
    lane = lax.broadcasted_iota(jnp.int32, (tm, 512), 1)
    low_half = (lane % 32) < 16

    def rope(p):
        partner = jnp.where(low_half, pltpu.roll(p, 512 - 16, 1), pltpu.roll(p, 16, 1))
        return p * cos_ref[...] + partner * sin_ref[...]

    outs = {name: ref for (name, _, _), ref in zip(_IN_OUTS, out_refs)}
    col0 = {}
    lo = 0
    for name, width in _IN_COLS:
        col0[name] = lo
        lo += width

    gates = {}
    for name, lb_ref in (("zf", lbf_ref), ("zb", lbb_ref)):
        terms, kin = _forget_gate(_dot(h, w_ref[0, :, col0[name]:col0[name] + 512]), lb_ref[...])
        outs["k" + name[1]][0] = kin.astype(BF16)
        gates[name] = terms
    for name, width in _IN_COLS:
        if name in gates:
            continue
        for c0 in range(0, width, 512):
            p = _dot(h, w_ref[0, :, col0[name] + c0:col0[name] + c0 + 512])
            if name in ("qc", "kc"):
                p = rope(p)
            elif name in _IN_ACTIVATIONS:
                p = _IN_ACTIVATIONS[name](p)
            outs[name][0, :, c0:c0 + 512] = p.astype(outs[name].dtype)
    for name, tri_ref in (("zf", trif_ref), ("zb", trib_ref)):
        hi, mid, low = gates[name]
        outs["b" + name[1]][0] = _dot(tri_ref[...], hi) + _dot(tri_ref[...], mid) + _dot(tri_ref[...], low)


def _scan_matrices(tm):
    row = jnp.arange(tm)[:, None]
    col = jnp.arange(tm)[None, :]
    same = (row // A_CHUNK) == (col // A_CHUNK)
    return (same & (col <= row)).astype(BF16), (same & (col >= row)).astype(BF16)


def _inproj(x_lat, x_ctx, ctx_block0, t, mod3, g, w_bf, layer, cos_t, sin_t, lb_f, lb_b, n_lat_blocks):
    bsz, _, d = x_lat.shape
    tm = ROW_TILE
    ncols = w_bf.shape[2]
    ctx_row = bsz

    def mod_map(b, j):
        return (jnp.where(j < n_lat_blocks, b, ctx_row), 0, 0)

    out_shape = [jax.ShapeDtypeStruct((bsz, t, w), dt) for _, w, dt in _IN_OUTS]
    out_specs = [pl.BlockSpec((1, tm, w), lambda b, j: (b, j, 0)) for _, w, _ in _IN_OUTS]
    tri_f, tri_b = _scan_matrices(tm)
    vec = pl.BlockSpec((1, 512), lambda b, j: (0, 0))
    lat_spec, ctx_spec = _lat_ctx_specs(tm, d, n_lat_blocks, ctx_block0)
    outs = pl.pallas_call(
        functools.partial(_inproj_kernel, d=d, n_lat_blocks=n_lat_blocks),
        grid=(bsz, t // tm),
        in_specs=[lat_spec, ctx_spec,
                  pl.BlockSpec((1, 1, mod3.shape[2]), mod_map),
                  pl.BlockSpec((1, d), lambda b, j: (0, 0)),
                  _resident((1, d, ncols), lambda b, j: (layer, 0, 0)),
                  pl.BlockSpec((tm, 512), lambda b, j: (j, 0)),
                  pl.BlockSpec((tm, 512), lambda b, j: (j, 0)),
                  vec, vec, _resident((tm, tm), lambda b, j: (0, 0)), _resident((tm, tm), lambda b, j: (0, 0))],
        out_specs=out_specs,
        out_shape=out_shape,
        compiler_params=_cparams(("arbitrary", "arbitrary"), V7X_VMEM_LIMIT),
        name="inproj",
    )(x_lat, x_ctx, mod3, g.reshape(1, d), w_bf, cos_t, sin_t, lb_f.reshape(1, 512), lb_b.reshape(1, 512), tri_f, tri_b)
    return {name: o for (name, _, _), o in zip(_IN_OUTS, outs)}


def _hgrn_direction(q_ref, b_ref, k_ref, v_ref, o_ref, st_ref, reverse, rows, cross_group_only):
    c = A_CHUNK
    row = lax.broadcasted_iota(jnp.int32, (c, c), 0)
    col = lax.broadcasted_iota(jnp.int32, (c, c), 1)
    allowed = (col >= row) if reverse else (col <= row)
    if cross_group_only:
        allowed = allowed & ((row // A_SUB) != (col // A_SUB))

    b = b_ref[0, rows, :]
    tot_row = 0 if reverse else c - 1
    tot = b[tot_row:tot_row + 1, :]

    q = q_ref[0, rows, :].astype(F32)
    kin = k_ref[0, rows, :].astype(F32)
    v = v_ref[0, rows, :]
    zero_blk = jnp.zeros((A_SUB, HEAD_W), F32)
    heads = []
    for h in range(A_HEADS):
        hs = slice(h * HEAD_W, (h + 1) * HEAD_W)
        bh, qh, kh = b[:, hs], q[:, hs], kin[:, hs]
        q_state = qh * jnp.exp(bh)
        k_state = kh * jnp.exp(tot[:, hs] - bh)
        q_rows, k_cols = [], []
        for i in range(c // A_SUB):
            r0 = i * A_SUB
            ref = r0 + A_SUB - 1 if reverse else r0
            r = bh[ref:ref + 1, :]
            qt = qh[r0:r0 + A_SUB] * jnp.exp(bh[r0:r0 + A_SUB] - r)
            q_rows.append(jnp.concatenate([qt if j == i else zero_blk for j in range(c // A_SUB)], axis=1))
            live = slice(r0, c) if reverse else slice(0, r0 + A_SUB)
            kt = kh[live] * jnp.exp(jnp.minimum(r - bh[live], EXP_CLAMP))
            dead = jnp.zeros((c - kt.shape[0], HEAD_W), F32)
            k_cols.append(jnp.concatenate([dead, kt] if reverse else [kt, dead], axis=0) if dead.shape[0] else kt)
        heads.append(dict(
            hs=hs, h=h, rows=rows, allowed=allowed, vh=v[:, hs], decay=jnp.exp(tot[:, hs]), o_ref=o_ref, st_ref=st_ref,
            q_wide=jnp.concatenate(q_rows, axis=0).astype(BF16),
            k_wide=jnp.concatenate(k_cols, axis=1).astype(BF16),
            q_state=q_state.astype(BF16), k_state=k_state.astype(BF16)))
    return heads


def _hgrn_exact_groups(q_ref, b_ref, k_ref, v_ref, o_ref, reverse):
    sub = lax.broadcasted_iota(jnp.int32, (A_SUB, HEAD_W), 0)

    def chunk(c, carry):
        for g in range(A_CHUNK // A_SUB):
            rows = pl.ds(pl.multiple_of(c * A_CHUNK + g * A_SUB, A_SUB), A_SUB)
            for h in range(A_HEADS):
                hs = slice(h * HEAD_W, (h + 1) * HEAD_W)
                bg, qg = b_ref[0, rows, hs], q_ref[0, rows, hs].astype(F32)
                kg, vg = k_ref[0, rows, hs].astype(F32), v_ref[0, rows, hs].astype(F32)
                acc = jnp.zeros((A_SUB, HEAD_W), F32)
                for t in range(A_SUB):
                    live = (sub >= t) if reverse else (sub <= t)
                    w = jnp.exp(jnp.minimum(bg[t:t + 1] - bg, 0.0))
                    a = jnp.sum(jnp.where(live, qg[t:t + 1] * kg * w, 0.0), axis=-1, keepdims=True)
                    acc = jnp.where(sub == t, jnp.sum(a * vg, axis=0, keepdims=True), acc)
                o_ref[0, rows, hs] = (o_ref[0, rows, hs].astype(F32) + acc).astype(o_ref.dtype)
        return carry

    lax.fori_loop(0, q_ref.shape[1] // A_CHUNK, chunk, 0)


def _hgrn_kernel(qf_ref, bf_ref, kf_ref, vf_ref, qb_ref, bb_ref, kb_ref, vb_ref, of_ref, ob_ref, sf_ref, sb_ref):
    @pl.when(pl.program_id(1) == 0)
    def _():
        sf_ref[...] = jnp.zeros_like(sf_ref)
        sb_ref[...] = jnp.zeros_like(sb_ref)

    drops = None
    for b_ref in (bf_ref, bb_ref):
        for r0 in range(0, qf_ref.shape[1], A_SUB):
            d = jnp.abs(b_ref[0, r0:r0 + 1, :] - b_ref[0, r0 + A_SUB - 1:r0 + A_SUB, :])
            drops = d if drops is None else jnp.maximum(drops, d)
    within_range = jnp.max(drops) <= EXP_CLAMP

    @pl.when(within_range)
    def _():
        _hgrn_block(qf_ref, bf_ref, kf_ref, vf_ref, qb_ref, bb_ref, kb_ref, vb_ref, of_ref, ob_ref, sf_ref, sb_ref,
                    cross_group_only=False)

    @pl.when(jnp.logical_not(within_range))
    def _():
        _hgrn_block(qf_ref, bf_ref, kf_ref, vf_ref, qb_ref, bb_ref, kb_ref, vb_ref, of_ref, ob_ref, sf_ref, sb_ref,
                    cross_group_only=True)
        _hgrn_exact_groups(qf_ref, bf_ref, kf_ref, vf_ref, of_ref, reverse=False)
        _hgrn_exact_groups(qb_ref, bb_ref, kb_ref, vb_ref, ob_ref, reverse=True)


def _hgrn_block(qf_ref, bf_ref, kf_ref, vf_ref, qb_ref, bb_ref, kb_ref, vb_ref, of_ref, ob_ref, sf_ref, sb_ref,
                cross_group_only):
    n_sub = qf_ref.shape[1] // A_CHUNK
    seq = []
    for i in range(n_sub):
        up = slice(i * A_CHUNK, (i + 1) * A_CHUNK)
        down = slice((n_sub - 1 - i) * A_CHUNK, (n_sub - i) * A_CHUNK)
        seq.append(_hgrn_direction(qf_ref, bf_ref, kf_ref, vf_ref, of_ref, sf_ref, False, up, cross_group_only)
                   + _hgrn_direction(qb_ref, bb_ref, kb_ref, vb_ref, ob_ref, sb_ref, True, down, cross_group_only))
    for chains in seq:
        for ch in chains:
            ch["att"] = _dot_nt(ch["q_wide"], ch["k_wide"])
    state = [ch["st_ref"][ch["h"]] for ch in seq[0]]
    for chains in seq:
        for ch, st in zip(chains, state):
            ch["inter"] = _dot_nt(ch["q_state"], st.astype(BF16))
            ch["upd"] = _dot_tn(ch["vh"], ch["k_state"])
        for n, ch in enumerate(chains):
            att = jnp.where(ch["allowed"], ch["att"], 0.0).astype(BF16)
            o = _dot(att, ch["vh"]) + ch["inter"]
            ch["o_ref"][0, ch["rows"], ch["hs"]] = o.astype(ch["o_ref"].dtype)
            state[n] = state[n] * ch["decay"] + ch["upd"]
    for ch, st in zip(seq[0], state):
        ch["st_ref"][ch["h"]] = st


def _hgrn(qa, bf, kf, bb, kb, ia, n_lat):
    bsz, t, w = qa.shape
    c = HGRN_ROWS
    n_chunks = t // c
    lat_chunks = n_lat // c

    def fwd(b, s):
        return (b, (s + lat_chunks) % n_chunks, 0)

    def bwd(b, s):
        return (b, n_chunks - 1 - s, 0)

    blk = (1, c, w)
    return pl.pallas_call(
        _hgrn_kernel,
        grid=(bsz, n_chunks),
        in_specs=[pl.BlockSpec(blk, fwd)] * 4 + [pl.BlockSpec(blk, bwd)] * 4,
        out_specs=[pl.BlockSpec(blk, fwd), pl.BlockSpec(blk, bwd)],
        out_shape=[jax.ShapeDtypeStruct((bsz, t, w), BF16)] * 2,
        scratch_shapes=[pltpu.VMEM((A_HEADS, HEAD_W, HEAD_W), F32)] * 2,
        compiler_params=_cparams(("arbitrary", "arbitrary")),
        name="hgrn2",
    )(qa, bf, kf, ia, qa, bb, kb, ia)


def _attn_kernel(q_ref, k_ref, v_ref, lam_ref, sub_ref, *rest, key_blocks, lam_init, n_cast):
    cast_in, (o_ref, *cast_out), (v1_ref, m_ref, acc_ref) = rest[:n_cast], rest[n_cast:2 * n_cast + 1], rest[-3:]
    for src, dst in zip(cast_in, cast_out):
        dst[...] = src[...].astype(BF16)
    tq = q_ref.shape[1]

    @pl.when(pl.program_id(2) == 0)
    def _():
        v1_ref[:, :HEAD_W] = v_ref[0]
        v1_ref[:, HEAD_W:] = jnp.ones((v1_ref.shape[0], HEAD_W), BF16)

    lane = lax.broadcasted_iota(jnp.int32, (tq, HEAD_W), 1)
    qf = q_ref[0].astype(F32) * (C_DH ** -0.5 * LOG2_E)
    q2 = jnp.concatenate([jnp.where(lane < C_DH, qf, 0.0), jnp.where(lane >= C_DH, qf, 0.0)], axis=0).astype(BF16)

    m_ref[...] = jnp.full_like(m_ref, NEG_BIG)
    acc_ref[...] = jnp.zeros_like(acc_ref)

    def scores(start, size):
        return _dot_nt(q2, k_ref[0, pl.ds(start, size), :])

    def absorb(s, start, size):
        m_old = m_ref[...]
        m_new = jnp.maximum(m_old, jnp.max(s, axis=-1, keepdims=True))
        alpha = jnp.exp2(m_old - m_new)
        p = jnp.exp2(s - jnp.concatenate([m_new] * (size // HEAD_W), axis=1))
        acc_ref[...] = (jnp.concatenate([alpha, alpha], axis=1) * acc_ref[...]
                        + _dot(p.astype(BF16), v1_ref[pl.ds(start, size), :]))
        m_ref[...] = m_new

    cur = scores(*key_blocks[0])
    for n, blk in enumerate(key_blocks):
        nxt = scores(*key_blocks[n + 1]) if n + 1 < len(key_blocks) else None
        absorb(cur, *blk)
        cur = nxt

    o12 = acc_ref[:, :HEAD_W] / acc_ref[:, HEAD_W:]
    o = o12[:tq] - lam_ref[...] * o12[tq:]
    o_ref[0] = (_rms(o) * sub_ref[...] * (1.0 - lam_init)).astype(o_ref.dtype)


def _attention(qc, kc, vc, lam, subln, lam_init, tq, q_block0, n_q, kv_block, key_blocks, to_bf16=()):
    bsz, _, w = qc.shape
    kv_rows, kv_idx = kv_block
    n_steps = bsz * C_HEADS * n_q
    cast_specs = []
    for a in to_bf16:
        rows = a.shape[0] // n_steps
        assert a.shape[0] == rows * n_steps and rows % BF16_ROWS == 0
        cast_specs.append(pl.BlockSpec((rows, a.shape[1]), lambda b, h, i: ((b * C_HEADS + h) * n_q + i, 0)))
    outs = pl.pallas_call(
        functools.partial(_attn_kernel, key_blocks=key_blocks, lam_init=lam_init, n_cast=len(to_bf16)),
        grid=(bsz, C_HEADS, n_q),
        in_specs=[pl.BlockSpec((1, tq, HEAD_W), lambda b, h, i: (b, q_block0 + i, h)),
                  pl.BlockSpec((1, kv_rows, HEAD_W), lambda b, h, i: (b, kv_idx, h)),
                  pl.BlockSpec((1, kv_rows, HEAD_W), lambda b, h, i: (b, kv_idx, h)),
                  pl.BlockSpec((1, HEAD_W), lambda b, h, i: (0, 0)),
                  pl.BlockSpec((1, HEAD_W), lambda b, h, i: (0, h))] + cast_specs,
        out_specs=[pl.BlockSpec((1, tq, HEAD_W), lambda b, h, i: (b, i, h))] + cast_specs,
        out_shape=[jax.ShapeDtypeStruct((bsz, n_q * tq, w), BF16)]
        + [jax.ShapeDtypeStruct(a.shape, BF16) for a in to_bf16],
        scratch_shapes=[pltpu.VMEM((kv_rows, 2 * HEAD_W), BF16), pltpu.VMEM((2 * tq, HEAD_W), F32),
                        pltpu.VMEM((2 * tq, 2 * HEAD_W), F32)],
        compiler_params=_cparams(("arbitrary", "arbitrary", "arbitrary")),
        name="diff_attention",
    )(qc, kc, vc, jnp.full((1, HEAD_W), lam, F32), subln.reshape(1, w), *to_bf16)
    return outs[0], outs[1:]


def _merge_kernel(xl_ref, xc_ref, mod_ref, of_ref, ob_ref, og_ref, gn_ref, uv_ref, vn_ref, ws_ref, bs_ref, ycl_ref,
                  ycc_ref, gt_ref, wb_ref, wo_ref, g2_ref, wr_ref, x_out_ref, h_out_ref, *maybe_logits_ref, d, n_lat_blocks):
    tm = xl_ref.shape[1]
    is_lat = pl.program_id(1) < n_lat_blocks
    mod = mod_ref[0]
    groups = [slice(r0, r0 + B_CHUNK) for r0 in range(0, tm, B_CHUNK)]
    st = [dict() for _ in groups]

    for s, rows in zip(st, groups):
        o = of_ref[0, rows, :].astype(F32) + ob_ref[0, rows, :].astype(F32)
        o = jnp.concatenate([_rms(o[:, h * HEAD_W:(h + 1) * HEAD_W]) for h in range(A_HEADS)], axis=1)
        s["ya"] = (o * gn_ref[...] * og_ref[0, rows, :].astype(F32)).astype(BF16)
        uv = uv_ref[0, rows, :].astype(F32)
        s["u"] = uv[:, :BRANCH_W]
        s["vv"] = (_rms(uv[:, BRANCH_W:]) * vn_ref[...]).astype(BF16)
    for s in st:
        s["mixed"] = jnp.concatenate(
            [_dot(ws_ref[g], s["vv"][:, g * HEAD_W:(g + 1) * HEAD_W]) for g in range(B_GROUPS)], axis=1)
    for s, rows in zip(st, groups):
        yb = (s["u"] * (s["mixed"] + bs_ref[...])).astype(BF16)
        yc = jnp.where(is_lat, ycl_ref[0, rows, :], ycc_ref[0, rows, :])
        s["proj"] = [_dot(y, wb_ref[i]) for i, y in enumerate((s["ya"], yb, yc))]
    for s, rows in zip(st, groups):
        merged = None
        for i in range(N_BRANCH):
            term = gt_ref[0, rows, i * d:(i + 1) * d].astype(F32) * s["proj"][i]
            merged = term if merged is None else merged + term
        s["mix"] = _dot(merged.astype(BF16), wo_ref[...])
    for s, rows in zip(st, groups):
        x_old = jnp.where(is_lat, xl_ref[0, rows, :], xc_ref[0, rows, :])
        x_new = x_old + mod[:, 2 * d:3 * d] * s["mix"]
        x_out_ref[0, rows, :] = x_new
        h = _rms(x_new) * g2_ref[...]
        h = h * (1.0 + mod[:, 4 * d:5 * d]) + mod[:, 3 * d:4 * d]
        h_hi = h.astype(BF16)
        h_out_ref[0, rows, :] = h_hi
        if maybe_logits_ref:
            h_lo = (h - h_hi.astype(F32)).astype(BF16)
            lg = (_dot(h_hi, wr_ref[0]) + _dot(h_lo, wr_ref[0])) + _dot(h_hi, wr_ref[1])
            maybe_logits_ref[0][0, rows, :] = lg
            maybe_logits_ref[1][:, rows] = lg.T


def _merge(x_lat, x_ctx, ctx_block0, mod3, proj, of, ob, yc_lat, yc_ctx, gnorm, vnorm, ws_bf, bs_full, wb_bf, wo_bf, g2, wr_split,
           n_lat_blocks, n_blocks, with_logits):
    bsz, _, d = x_lat.shape
    tm = ROW_TILE
    ctx_row = bsz

    def rows(w):
        return pl.BlockSpec((1, tm, w), lambda b, j: (b, j, 0))

    def const2(shape):
        return pl.BlockSpec(shape, lambda b, j: (0, 0))

    def mod_map(b, j):
        return (jnp.where(j < n_lat_blocks, b, ctx_row), 0, 0)

    n_rows = n_blocks * tm
    out_shape = [jax.ShapeDtypeStruct((bsz, n_rows, d), F32), jax.ShapeDtypeStruct((bsz, n_rows, d), BF16)]
    out_specs = [rows(d), rows(d)]
    if with_logits:
        out_shape += [jax.ShapeDtypeStruct((bsz, n_rows, HEAD_W), F32),
                      jax.ShapeDtypeStruct((HEAD_W, bsz * n_rows), F32)]
        out_specs += [rows(HEAD_W), pl.BlockSpec((HEAD_W, tm), lambda b, j: (0, b * n_blocks + j))]
    lat_spec, ctx_spec = _lat_ctx_specs(tm, d, n_lat_blocks, ctx_block0)
    return pl.pallas_call(
        functools.partial(_merge_kernel, d=d, n_lat_blocks=n_lat_blocks),
        grid=(bsz, n_blocks),
        in_specs=[lat_spec, ctx_spec, pl.BlockSpec((1, 1, mod3.shape[2]), mod_map),
                  rows(BRANCH_W), rows(BRANCH_W), rows(BRANCH_W), const2((1, BRANCH_W)),
                  rows(2 * BRANCH_W), const2((1, BRANCH_W)),
                  _resident((B_GROUPS, B_CHUNK, B_CHUNK), lambda b, j: (0, 0, 0)), const2((B_CHUNK, BRANCH_W)),
                  *_lat_ctx_specs(tm, BRANCH_W, n_lat_blocks, 0), rows(N_BRANCH * d),
                  _resident((N_BRANCH, BRANCH_W, d), lambda b, j: (0, 0, 0)), _resident((d, d), lambda b, j: (0, 0)),
                  const2((1, d)), _resident((2, d, HEAD_W), lambda b, j: (0, 0, 0))],
        out_specs=out_specs,
        out_shape=out_shape,
        compiler_params=_cparams(("arbitrary", "arbitrary"), V7X_VMEM_LIMIT),
        name="merge",
    )(x_lat, x_ctx, mod3, of, ob, proj["og"], gnorm.reshape(1, BRANCH_W), proj["uv"], vnorm.reshape(1, BRANCH_W),
      ws_bf, bs_full, yc_lat, yc_ctx, proj["gt"], wb_bf, wo_bf, g2.reshape(1, d), wr_split)


def _ffn_kernel(x_ref, h_ref, mod_ref, wg_ref, wu_ref, wd_ref, o_ref, *, d):
    h = h_ref[0]
    g = _dot(h, wg_ref[...])
    a = (g * _sigmoid(g) * _dot(h, wu_ref[...])).astype(BF16)
    y = _dot(a, wd_ref[...])
    o_ref[0] = x_ref[0] + mod_ref[0][:, 5 * d:6 * d] * y


def _ffn(xc, h2, mod3, wg_bf, wu_bf, wd_bf, n_lat_blocks):
    bsz, t, d = xc.shape
    tm = ROW_TILE
    dff = wg_bf.shape[1]
    ctx_row = bsz

    def mod_map(b, j):
        return (jnp.where(j < n_lat_blocks, b, ctx_row), 0, 0)

    rows = pl.BlockSpec((1, tm, d), lambda b, j: (b, j, 0))
    return pl.pallas_call(
        functools.partial(_ffn_kernel, d=d),
        grid=(bsz, t // tm),
        in_specs=[rows, rows, pl.BlockSpec((1, 1, mod3.shape[2]), mod_map),
                  _resident((d, dff), lambda b, j: (0, 0)), _resident((d, dff), lambda b, j: (0, 0)),
                  _resident((dff, d), lambda b, j: (0, 0))],
        out_specs=rows,
        out_shape=jax.ShapeDtypeStruct((bsz, t, d), F32),
        compiler_params=_cparams(("arbitrary", "arbitrary"), V7X_VMEM_LIMIT),
        name="dense_ffn",
    )(xc, h2, mod3, wg_bf, wu_bf, wd_bf)


def _moe_kernel(tile_expert_ref, n_tiles_ref, x_ref, wg_ref, wu_ref, wd_ref, o_ref):
    del tile_expert_ref

    @pl.when(pl.program_id(0) < n_tiles_ref[0])
    def _():
        x = x_ref[...]
        dff = wg_ref.shape[2]
        acc = None
        for c0 in range(0, dff, FF_CHUNK):
            g = _dot(x, wg_ref[0, :, c0:c0 + FF_CHUNK])
            a = (g * _sigmoid(g) * _dot(x, wu_ref[0, :, c0:c0 + FF_CHUNK])).astype(BF16)
            y = _dot(a, wd_ref[0, c0:c0 + FF_CHUNK, :])
            acc = y if acc is None else acc + y
        o_ref[...] = acc.astype(o_ref.dtype)

    @pl.when(pl.program_id(0) >= n_tiles_ref[0])
    def _():
        o_ref[...] = jnp.zeros_like(o_ref)


def _moe_experts(x_sorted, tile_expert, n_tiles, wg_bf, wu_bf, wd_bf):
    p, d = x_sorted.shape
    tm = MOE_TILE
    dff = wg_bf.shape[2]
    grid_spec = pltpu.PrefetchScalarGridSpec(
        num_scalar_prefetch=2,
        grid=(p // tm,),
        in_specs=[pl.BlockSpec((tm, d), lambda t, te, nt: (t, 0)),
                  pl.BlockSpec((1, d, dff), lambda t, te, nt: (te[t], 0, 0)),
                  pl.BlockSpec((1, d, dff), lambda t, te, nt: (te[t], 0, 0)),
                  pl.BlockSpec((1, dff, d), lambda t, te, nt: (te[t], 0, 0))],
        out_specs=pl.BlockSpec((tm, d), lambda t, te, nt: (t, 0)),
    )
    return pl.pallas_call(
        _moe_kernel,
        grid_spec=grid_spec,
        out_shape=jax.ShapeDtypeStruct((p, d), BF16),
        compiler_params=_cparams(("arbitrary",), V7X_VMEM_LIMIT),
        name="moe_experts",
    )(tile_expert, n_tiles, x_sorted, wg_bf, wu_bf, wd_bf)


def _route(logits_t, tm):
    n = logits_t.shape[1]
    experts = jnp.arange(N_EXPERTS, dtype=jnp.int32)[:, None]
    e1 = jnp.argmax(logits_t, axis=0).astype(jnp.int32)
    e2 = jnp.argmax(jnp.where(experts == e1[None, :], -jnp.inf, logits_t), axis=0).astype(jnp.int32)
    oh1 = (experts == e1[None, :]).astype(jnp.int32)
    oh2 = (experts == e2[None, :]).astype(jnp.int32)
    c1 = jnp.cumsum(oh1, axis=1)
    c2 = jnp.cumsum(oh2, axis=1) + c1[:, -1:]
    counts = c2[:, -1]
    padded = ((counts + tm - 1) // tm) * tm
    ends = jnp.cumsum(padded)
    starts = ends - padded
    dest1 = jnp.sum(oh1 * (starts[:, None] + c1 - 1), axis=0)
    dest2 = jnp.sum(oh2 * (starts[:, None] + c2 - 1), axis=0)
    m = TOP_K * n
    p = m + N_EXPERTS * tm
    tile_start = jnp.arange(p // tm, dtype=jnp.int32) * tm
    tile_expert = jnp.minimum(jnp.sum((tile_start[:, None] >= ends[None, :]).astype(jnp.int32), axis=1),
                              N_EXPERTS - 1).astype(jnp.int32)
    n_tiles = (ends[-1] // tm).astype(jnp.int32).reshape(1)
    token = jnp.arange(n, dtype=jnp.int32)
    by_expert = jnp.sort(jnp.concatenate([e1 * m + token, e2 * m + n + token])) % m
    first = jnp.cumsum(counts) - counts
    row_expert = jnp.repeat(tile_expert, tm)
    pos = jnp.arange(p, dtype=jnp.int32)
    entry = jnp.take(by_expert, jnp.clip(first[row_expert] + pos - starts[row_expert], 0, m - 1))
    src_token = jnp.where(entry < n, entry, entry - n)
    return dest1, dest2, src_token, tile_expert, n_tiles


def _final_kernel(x_ref, y0_ref, y1_ref, lg_ref, mod_ref, g_ref, o_ref, *, d):
    lane = lax.broadcasted_iota(jnp.int32, lg_ref.shape[1:], 1)
    lg = jnp.where(lane < N_EXPERTS, lg_ref[0], NEG_BIG)
    m1 = jnp.max(lg, axis=-1, keepdims=True)
    first = jnp.min(jnp.where(lg == m1, lane, HEAD_W), axis=-1, keepdims=True)
    m2 = jnp.max(jnp.where(lane == first, NEG_BIG, lg), axis=-1, keepdims=True)
    e = jnp.exp(m2 - m1)
    w1 = 1.0 / (1.0 + e)
    y = w1 * y0_ref[0].astype(F32) + (e * w1) * y1_ref[0].astype(F32)
    x = x_ref[0] + mod_ref[0][:, 5 * d:6 * d] * y
    o_ref[0] = _rms(x) * g_ref[...]


def _final(x_lat, y0, y1, logits, mod3, g_final):
    bsz, s, d = x_lat.shape
    tm = ROW_TILE
    rows = pl.BlockSpec((1, tm, d), lambda b, j: (b, j, 0))
    return pl.pallas_call(
        functools.partial(_final_kernel, d=d),
        grid=(bsz, s // tm),
        in_specs=[rows, rows, rows, pl.BlockSpec((1, tm, HEAD_W), lambda b, j: (b, j, 0)),
                  pl.BlockSpec((1, 1, mod3.shape[2]), lambda b, j: (b, 0, 0)),
                  pl.BlockSpec((1, d), lambda b, j: (0, 0))],
        out_specs=rows,
        out_shape=jax.ShapeDtypeStruct((bsz, s, d), F32),
        compiler_params=_cparams(("arbitrary", "arbitrary")),
        name="moe_combine_final_norm",
    )(x_lat, y0, y1, logits, mod3, g_final.reshape(1, d))


def _rope_tables(n_lat, n_ctx):
    pairs = C_DH // 4
    tpos = jnp.arange(n_lat)
    pos = jnp.stack([(tpos // GRID_W).astype(F32), (tpos % GRID_W).astype(F32)], axis=1)
    freqs = ROPE_BASE ** (-jnp.arange(pairs, dtype=F32) / pairs)
    ang = pos[:, :, None] * freqs[None, None, :]
    cos = jnp.repeat(jnp.cos(ang)[:, :, None, :], 2, axis=2).reshape(n_lat, C_DH)
    sin = jnp.sin(ang)[:, :, None, :] * jnp.array([-1.0, 1.0], F32)[None, None, :, None]
    sin = sin.reshape(n_lat, C_DH)
    cos = jnp.concatenate([cos, jnp.ones((n_ctx, C_DH), F32)], axis=0)
    sin = jnp.concatenate([sin, jnp.zeros((n_ctx, C_DH), F32)], axis=0)
    reps = 512 // C_DH
    return jnp.tile(cos, (1, reps)), jnp.tile(sin, (1, reps))


def _lower_bound(p, layer):
    cs = jnp.cumsum(jax.nn.softmax(p.astype(F32), axis=0), axis=0)
    return cs[layer] - cs[0]


def kernel(x, c, ctx, c_ctx, w_ada, b_ada, g_norm1, g_norm2, w_in, hgrn_lb, hgrn_gnorm, mlp_vnorm, mlp_ws, mlp_bs,
           diff_lambda, diff_subln, w_branch, w_out, ffn_wg, ffn_wu, ffn_wd, moe_router, moe_wg, moe_wu, moe_wd,
           g_final):
    bsz, n_lat, d = x.shape
    n_ctx = ctx.shape[1]
    depth = w_ada.shape[0]
    t = n_lat + n_ctx
    tm = ROW_TILE
    assert depth == 2 and bsz < MOD_ROWS and n_lat % GRID_W == 0
    assert n_lat % ATT_TQ == 0 and n_lat % ATT_TK == 0 and n_lat % n_ctx == 0 and n_ctx % tm == 0
    n_lat_blocks = n_lat // tm

    cvec = jnp.zeros((MOD_ROWS, d), F32).at[:bsz].set(c).at[bsz].set(c_ctx)
    mods = _adaln(cvec, w_ada, b_ada)
    cos_t, sin_t = _rope_tables(n_lat, n_ctx)
    x_lat, x_ctx, ctx_block0 = x, ctx, 0
    w_in_bf = w_in.astype(BF16)

    out = None
    for layer in range(depth):
        last = layer == depth - 1
        mod3 = mods[layer].reshape(MOD_ROWS, 1, 6 * d)
        lb_f = _lower_bound(hgrn_lb[0], layer)
        lb_b = _lower_bound(hgrn_lb[1], layer)
        proj = _inproj(x_lat, x_ctx, ctx_block0, t, mod3, g_norm1[layer], w_in_bf, layer, cos_t, sin_t,
                       lb_f, lb_b, n_lat_blocks)
        of, ob = _hgrn(proj["qa"], proj["bf"], proj["kf"], proj["bb"], proj["kb"], proj["ia"], n_lat)

        lam_init = 0.8 - 0.6 * math.exp(-0.3 * layer)
        lam_p = diff_lambda[layer]
        lam = jnp.exp(jnp.sum(lam_p[0] * lam_p[1])) - jnp.exp(jnp.sum(lam_p[2] * lam_p[3])) + lam_init
        lat_keys = tuple((i * ATT_TK, ATT_TK) for i in range(n_lat // ATT_TK)) + ((n_lat, n_ctx),)
        qkv = (proj["qc"], proj["kc"], proj["vc"], lam.astype(F32), diff_subln[layer], lam_init)
        j = layer // 2
        is_moe = layer % 2 == 1
        expert_w = ()
        if is_moe:
            n_e, _, dff = moe_wg[j].shape
            expert_w = (moe_wg[j].reshape(n_e * d, dff), moe_wu[j].reshape(n_e * d, dff),
                        moe_wd[j].reshape(n_e * dff, d))
        yc_lat, expert_w_bf = _attention(*qkv, ATT_TQ, 0, n_lat // ATT_TQ, (t, 0), lat_keys, expert_w)
        yc_ctx = yc_lat if last else _attention(*qkv, n_ctx, n_lat // n_ctx, 1, (n_ctx, n_lat // n_ctx),
                                                ((0, n_ctx),))[0]

        bs_full = jnp.repeat(mlp_bs[layer].T, HEAD_W, axis=1)
        wr_split = jnp.zeros((2, d, HEAD_W), BF16)
        if is_moe:
            wr_hi = moe_router[j].astype(BF16)
            wr_lo = (moe_router[j] - wr_hi.astype(F32)).astype(BF16)
            wr_split = wr_split.at[:, :, :N_EXPERTS].set(jnp.stack([wr_hi, wr_lo]))
        n_blocks = n_lat_blocks if last else t // tm
        merged = _merge(x_lat, x_ctx, ctx_block0, mod3, proj, of, ob, yc_lat, yc_ctx, hgrn_gnorm[layer], mlp_vnorm[layer],
                        mlp_ws[layer].astype(BF16), bs_full, w_branch[layer].astype(BF16), w_out[layer].astype(BF16),
                        g_norm2[layer], wr_split, n_lat_blocks, n_blocks, is_moe)
        x_new, h2 = merged[0], merged[1]

        if not is_moe:
            assert not last
            xc = _ffn(x_new, h2, mod3, ffn_wg[j].astype(BF16), ffn_wu[j].astype(BF16), ffn_wd[j].astype(BF16),
                      n_lat_blocks)
            x_lat, x_ctx, ctx_block0 = xc, xc, n_lat_blocks
        else:
            assert last
            n = bsz * n_lat
            logits, logits_t = merged[2], merged[3]
            dest1, dest2, src_token, tile_expert, n_tiles = _route(logits_t[:N_EXPERTS], MOE_TILE)
            x_sorted = jnp.take(h2.reshape(n, d), src_token, axis=0, mode="clip")
            y = _moe_experts(x_sorted, tile_expert, n_tiles, expert_w_bf[0].reshape(n_e, d, dff),
                             expert_w_bf[1].reshape(n_e, d, dff), expert_w_bf[2].reshape(n_e, dff, d))
            y0 = jnp.take(y, dest1, axis=0, mode="clip").reshape(bsz, n_lat, d)
            y1 = jnp.take(y, dest2, axis=0, mode="clip").reshape(bsz, n_lat, d)
            out = _final(x_new, y0, y1, logits, mod3, g_final)
    return out
```

```python
import functools
import math

import jax
import jax.numpy as jnp
from jax import lax
from jax.experimental import pallas as pl
from jax.experimental.pallas import tpu as pltpu

F32 = jnp.float32
BF16 = jnp.bfloat16

EPS = 1e-6
GRID_W = 64
ROPE_BASE = 10000.0

A_HEADS = 4
A_CHUNK = 64
A_SUB = 16
HGRN_ROWS = 256
MOE_TILE = 512
MOE_PARTS = 2
B_GROUPS = 4
B_CHUNK = 128
C_HEADS = 4
C_DH = 64
HEAD_W = 128
BRANCH_W = 512
N_BRANCH = 3
N_EXPERTS = 8
TOP_K = 2

BF16_ROWS = 16
ROW_TILE = 256
MOD_ROWS = 16
ATT_TQ = 512
ATT_TK = 512
FF_CHUNK = 512
EXP_CLAMP = 80.0
LOG2_E = 1.4426950408889634
NEG_BIG = -1e30
V7X_VMEM_LIMIT = 56 * 1024 * 1024


def _cparams(sem, vmem=None):
    return pltpu.CompilerParams(dimension_semantics=sem, vmem_limit_bytes=vmem)


def _resident(shape, index_map):
    return pl.BlockSpec(shape, index_map, pipeline_mode=pl.Buffered(1))


def _lat_ctx_specs(tm, d, n_lat_blocks, ctx_block0):
    lat = pl.BlockSpec((1, tm, d), lambda b, j: (b, jnp.minimum(j, n_lat_blocks - 1), 0))
    ctx = pl.BlockSpec((1, tm, d), lambda b, j: (b, ctx_block0 + jnp.maximum(j - n_lat_blocks, 0), 0))
    return lat, ctx


def _rms(xf):
    return xf * lax.rsqrt(jnp.mean(xf * xf, axis=-1, keepdims=True) + EPS)


def _sigmoid(x):
    return 1.0 / (1.0 + jnp.exp(-x))


def _dot(a, b):
    return jnp.dot(a, b, preferred_element_type=F32)


def _dot_nt(a, b):
    return lax.dot_general(a, b, (((1,), (1,)), ((), ())), preferred_element_type=F32)


def _dot_tn(a, b):
    return lax.dot_general(a, b, (((0,), (0,)), ((), ())), preferred_element_type=F32)


def _adaln_kernel(c_ref, w_ref, b_ref, o_ref):
    c = c_ref[...]
    a = c * _sigmoid(c)
    o_ref[0] = jnp.dot(a, w_ref[0], preferred_element_type=F32, precision=lax.Precision.HIGHEST) + b_ref[0]


def _adaln(cvec, w_ada, b_ada):
    depth, d, n = w_ada.shape
    tn = 1536
    return pl.pallas_call(
        _adaln_kernel,
        grid=(depth, n // tn),
        in_specs=[pl.BlockSpec((MOD_ROWS, d), lambda l, j: (0, 0)),
                  pl.BlockSpec((1, d, tn), lambda l, j: (l, 0, j)),
                  pl.BlockSpec((1, 1, tn), lambda l, j: (l, 0, j))],
        out_specs=pl.BlockSpec((1, MOD_ROWS, tn), lambda l, j: (l, 0, j)),
        out_shape=jax.ShapeDtypeStruct((depth, MOD_ROWS, n), F32),
        compiler_params=_cparams(("arbitrary", "arbitrary")),
        name="adaln",
    )(cvec, w_ada, b_ada.reshape(depth, 1, n))


_IN_COLS = (("qa", 512), ("zf", 512), ("zb", 512), ("ia", 512), ("og", 512), ("uv", 1024), ("qc", 512), ("kc", 512),
            ("vc", 512), ("gt", 3072))
_IN_OUTS = (("qa", 512, BF16), ("bf", 512, F32), ("kf", 512, BF16), ("bb", 512, F32), ("kb", 512, BF16),
            ("ia", 512, BF16), ("og", 512, BF16), ("uv", 1024, BF16), ("qc", 512, BF16), ("kc", 512, BF16),
            ("vc", 512, BF16), ("gt", 3072, BF16))


def _gelu_tanh(x):
    return 0.5 * x * (1.0 + jnp.tanh(math.sqrt(2.0 / math.pi) * (x + 0.044715 * (x * x * x))))


_IN_ACTIVATIONS = {"og": lambda p: p * _sigmoid(p), "uv": _gelu_tanh, "gt": _sigmoid}


def _forget_gate(z, lb):
    sp = jnp.maximum(-z, 0.0) + jnp.log(1.0 + jnp.exp(-jnp.abs(z)))
    la = jnp.log(lb)
    lc = jnp.log1p(-lb) - sp
    logf = jnp.maximum(la, lc) + jnp.log(1.0 + jnp.exp(-jnp.abs(la - lc)))
    kin = (1.0 - lb) * jnp.exp(-z - sp)
    hi = logf.astype(BF16)
    rest = logf - hi.astype(F32)
    mid = rest.astype(BF16)
    low = (rest - mid.astype(F32)).astype(BF16)
    return (hi, mid, low), kin


def _inproj_kernel(xl_ref, xc_ref, mod_ref, g_ref, w_ref, cos_ref, sin_ref, lbf_ref, lbb_ref, trif_ref, trib_ref,
                   *out_refs, d, n_lat_blocks):
    x = jnp.where(pl.program_id(1) < n_lat_blocks, xl_ref[0], xc_ref[0])
    mod = mod_ref[0]
    h = _rms(x) * g_ref[...]
    h = (h * (1.0 + mod[:, d:2 * d]) + mod[:, 0:d]).astype(BF16)

    tm = x.shape[0]
    lane = lax.broadcasted_iota(jnp.int32, (tm, 512), 1)
    low_half = (lane % 32) < 16

    def rope(p):
        partner = jnp.where(low_half, pltpu.roll(p, 512 - 16, 1), pltpu.roll(p, 16, 1))
        return p * cos_ref[...] + partner * sin_ref[...]

    outs = {name: ref for (name, _, _), ref in zip(_IN_OUTS, out_refs)}
    col0 = {}
    lo = 0
    for name, width in _IN_COLS:
        col0[name] = lo
        lo += width

    gates = {}
    for name, lb_ref in (("zf", lbf_ref), ("zb", lbb_ref)):
        terms, kin = _forget_gate(_dot(h, w_ref[0, :, col0[name]:col0[name] + 512]), lb_ref[...])
        outs["k" + name[1]][0] = kin.astype(BF16)
        gates[name] = terms
    for name, width in _IN_COLS:
        if name in gates:
            continue
        for c0 in range(0, width, 512):
            p = _dot(h, w_ref[0, :, col0[name] + c0:col0[name] + c0 + 512])
            if name in ("qc", "kc"):
                p = rope(p)
            elif name in _IN_ACTIVATIONS:
                p = _IN_ACTIVATIONS[name](p)
            outs[name][0, :, c0:c0 + 512] = p.astype(outs[name].dtype)
    for name, tri_ref in (("zf", trif_ref), ("zb", trib_ref)):
        hi, mid, low = gates[name]
        outs["b" + name[1]][0] = _dot(tri_ref[...], hi) + _dot(tri_ref[...], mid) + _dot(tri_ref[...], low)


def _scan_matrices(tm):
    row = jnp.arange(tm)[:, None]
    col = jnp.arange(tm)[None, :]
    same = (row // A_CHUNK) == (col // A_CHUNK)
    return (same & (col <= row)).astype(BF16), (same & (col >= row)).astype(BF16)


def _inproj(x_lat, x_ctx, ctx_block0, t, mod3, g, w_bf, layer, cos_t, sin_t, lb_f, lb_b, n_lat_blocks):
    bsz, _, d = x_lat.shape
    tm = ROW_TILE
    ncols = w_bf.shape[2]
    ctx_row = bsz

    def mod_map(b, j):
        return (jnp.where(j < n_lat_blocks, b, ctx_row), 0, 0)

    out_shape = [jax.ShapeDtypeStruct((bsz, t, w), dt) for _, w, dt in _IN_OUTS]
    out_specs = [pl.BlockSpec((1, tm, w), lambda b, j: (b, j, 0)) for _, w, _ in _IN_OUTS]
    tri_f, tri_b = _scan_matrices(tm)
    vec = pl.BlockSpec((1, 512), lambda b, j: (0, 0))
    lat_spec, ctx_spec = _lat_ctx_specs(tm, d, n_lat_blocks, ctx_block0)
    outs = pl.pallas_call(
        functools.partial(_inproj_kernel, d=d, n_lat_blocks=n_lat_blocks),
        grid=(bsz, t // tm),
        in_specs=[lat_spec, ctx_spec,
                  pl.BlockSpec((1, 1, mod3.shape[2]), mod_map),
                  pl.BlockSpec((1, d), lambda b, j: (0, 0)),
                  _resident((1, d, ncols), lambda b, j: (layer, 0, 0)),
                  pl.BlockSpec((tm, 512), lambda b, j: (j, 0)),
                  pl.BlockSpec((tm, 512), lambda b, j: (j, 0)),
                  vec, vec, _resident((tm, tm), lambda b, j: (0, 0)), _resident((tm, tm), lambda b, j: (0, 0))],
        out_specs=out_specs,
        out_shape=out_shape,
        compiler_params=_cparams(("arbitrary", "arbitrary"), V7X_VMEM_LIMIT),
        name="inproj",
    )(x_lat, x_ctx, mod3, g.reshape(1, d), w_bf, cos_t, sin_t, lb_f.reshape(1, 512), lb_b.reshape(1, 512), tri_f, tri_b)
    return {name: o for (name, _, _), o in zip(_IN_OUTS, outs)}


def _hgrn_direction(q_ref, b_ref, k_ref, v_ref, o_ref, st_ref, reverse, rows, cross_group_only):
    c = A_CHUNK
    row = lax.broadcasted_iota(jnp.int32, (c, c), 0)
    col = lax.broadcasted_iota(jnp.int32, (c, c), 1)
    allowed = (col >= row) if reverse else (col <= row)
    if cross_group_only:
        allowed = allowed & ((row // A_SUB) != (col // A_SUB))

    b = b_ref[0, rows, :]
    tot_row = 0 if reverse else c - 1
    tot = b[tot_row:tot_row + 1, :]

    q = q_ref[0, rows, :].astype(F32)
    kin = k_ref[0, rows, :].astype(F32)
    v = v_ref[0, rows, :]
    zero_blk = jnp.zeros((A_SUB, HEAD_W), F32)
    heads = []
    for h in range(A_HEADS):
        hs = slice(h * HEAD_W, (h + 1) * HEAD_W)
        bh, qh, kh = b[:, hs], q[:, hs], kin[:, hs]
        q_state = qh * jnp.exp(bh)
        k_state = kh * jnp.exp(tot[:, hs] - bh)
        q_rows, k_cols = [], []
        for i in range(c // A_SUB):
            r0 = i * A_SUB
            ref = r0 + A_SUB - 1 if reverse else r0
            r = bh[ref:ref + 1, :]
            qt = qh[r0:r0 + A_SUB] * jnp.exp(bh[r0:r0 + A_SUB] - r)
            q_rows.append(jnp.concatenate([qt if j == i else zero_blk for j in range(c // A_SUB)], axis=1))
            live = slice(r0, c) if reverse else slice(0, r0 + A_SUB)
            kt = kh[live] * jnp.exp(jnp.minimum(r - bh[live], EXP_CLAMP))
            dead = jnp.zeros((c - kt.shape[0], HEAD_W), F32)
            k_cols.append(jnp.concatenate([dead, kt] if reverse else [kt, dead], axis=0) if dead.shape[0] else kt)
        heads.append(dict(
            hs=hs, h=h, rows=rows, allowed=allowed, vh=v[:, hs], decay=jnp.exp(tot[:, hs]), o_ref=o_ref, st_ref=st_ref,
            q_wide=jnp.concatenate(q_rows, axis=0).astype(BF16),
            k_wide=jnp.concatenate(k_cols, axis=1).astype(BF16),
            q_state=q_state.astype(BF16), k_state=k_state.astype(BF16)))
    return heads


def _hgrn_exact_groups(q_ref, b_ref, k_ref, v_ref, o_ref, reverse):
    sub = lax.broadcasted_iota(jnp.int32, (A_SUB, HEAD_W), 0)

    def chunk(c, carry):
        for g in range(A_CHUNK // A_SUB):
            rows = pl.ds(pl.multiple_of(c * A_CHUNK + g * A_SUB, A_SUB), A_SUB)
            for h in range(A_HEADS):
                hs = slice(h * HEAD_W, (h + 1) * HEAD_W)
                bg, qg = b_ref[0, rows, hs], q_ref[0, rows, hs].astype(F32)
                kg, vg = k_ref[0, rows, hs].astype(F32), v_ref[0, rows, hs].astype(F32)
                acc = jnp.zeros((A_SUB, HEAD_W), F32)
                for t in range(A_SUB):
                    live = (sub >= t) if reverse else (sub <= t)
                    w = jnp.exp(jnp.minimum(bg[t:t + 1] - bg, 0.0))
                    a = jnp.sum(jnp.where(live, qg[t:t + 1] * kg * w, 0.0), axis=-1, keepdims=True)
                    acc = jnp.where(sub == t, jnp.sum(a * vg, axis=0, keepdims=True), acc)
                o_ref[0, rows, hs] = (o_ref[0, rows, hs].astype(F32) + acc).astype(o_ref.dtype)
        return carry

    lax.fori_loop(0, q_ref.shape[1] // A_CHUNK, chunk, 0)


def _hgrn_kernel(qf_ref, bf_ref, kf_ref, vf_ref, qb_ref, bb_ref, kb_ref, vb_ref, of_ref, ob_ref, sf_ref, sb_ref):
    @pl.when(pl.program_id(1) == 0)
    def _():
        sf_ref[...] = jnp.zeros_like(sf_ref)
        sb_ref[...] = jnp.zeros_like(sb_ref)

    drops = None
    for b_ref in (bf_ref, bb_ref):
        for r0 in range(0, qf_ref.shape[1], A_SUB):
            d = jnp.abs(b_ref[0, r0:r0 + 1, :] - b_ref[0, r0 + A_SUB - 1:r0 + A_SUB, :])
            drops = d if drops is None else jnp.maximum(drops, d)
    within_range = jnp.max(drops) <= EXP_CLAMP

    @pl.when(within_range)
    def _():
        _hgrn_block(qf_ref, bf_ref, kf_ref, vf_ref, qb_ref, bb_ref, kb_ref, vb_ref, of_ref, ob_ref, sf_ref, sb_ref,
                    cross_group_only=False)

    @pl.when(jnp.logical_not(within_range))
    def _():
        _hgrn_block(qf_ref, bf_ref, kf_ref, vf_ref, qb_ref, bb_ref, kb_ref, vb_ref, of_ref, ob_ref, sf_ref, sb_ref,
                    cross_group_only=True)
        _hgrn_exact_groups(qf_ref, bf_ref, kf_ref, vf_ref, of_ref, reverse=False)
        _hgrn_exact_groups(qb_ref, bb_ref, kb_ref, vb_ref, ob_ref, reverse=True)


def _hgrn_block(qf_ref, bf_ref, kf_ref, vf_ref, qb_ref, bb_ref, kb_ref, vb_ref, of_ref, ob_ref, sf_ref, sb_ref,
                cross_group_only):
    n_sub = qf_ref.shape[1] // A_CHUNK
    seq = []
    for i in range(n_sub):
        up = slice(i * A_CHUNK, (i + 1) * A_CHUNK)
        down = slice((n_sub - 1 - i) * A_CHUNK, (n_sub - i) * A_CHUNK)
        seq.append(_hgrn_direction(qf_ref, bf_ref, kf_ref, vf_ref, of_ref, sf_ref, False, up, cross_group_only)
                   + _hgrn_direction(qb_ref, bb_ref, kb_ref, vb_ref, ob_ref, sb_ref, True, down, cross_group_only))
    for chains in seq:
        for ch in chains:
            ch["att"] = _dot_nt(ch["q_wide"], ch["k_wide"])
    state = [ch["st_ref"][ch["h"]] for ch in seq[0]]
    for chains in seq:
        for ch, st in zip(chains, state):
            ch["inter"] = _dot_nt(ch["q_state"], st.astype(BF16))
            ch["upd"] = _dot_tn(ch["vh"], ch["k_state"])
        for n, ch in enumerate(chains):
            att = jnp.where(ch["allowed"], ch["att"], 0.0).astype(BF16)
            o = _dot(att, ch["vh"]) + ch["inter"]
            ch["o_ref"][0, ch["rows"], ch["hs"]] = o.astype(ch["o_ref"].dtype)
            state[n] = state[n] * ch["decay"] + ch["upd"]
    for ch, st in zip(seq[0], state):
        ch["st_ref"][ch["h"]] = st


def _hgrn(qa, bf, kf, bb, kb, ia, n_lat):
    bsz, t, w = qa.shape
    c = HGRN_ROWS
    n_chunks = t // c
    lat_chunks = n_lat // c

    def fwd(b, s):
        return (b, (s + lat_chunks) % n_chunks, 0)

    def bwd(b, s):
        return (b, n_chunks - 1 - s, 0)

    blk = (1, c, w)
    return pl.pallas_call(
        _hgrn_kernel,
        grid=(bsz, n_chunks),
        in_specs=[pl.BlockSpec(blk, fwd)] * 4 + [pl.BlockSpec(blk, bwd)] * 4,
        out_specs=[pl.BlockSpec(blk, fwd), pl.BlockSpec(blk, bwd)],
        out_shape=[jax.ShapeDtypeStruct((bsz, t, w), BF16)] * 2,
        scratch_shapes=[pltpu.VMEM((A_HEADS, HEAD_W, HEAD_W), F32)] * 2,
        compiler_params=_cparams(("arbitrary", "arbitrary")),
        name="hgrn2",
    )(qa, bf, kf, ia, qa, bb, kb, ia)


def _attn_kernel(q_ref, k_ref, v_ref, lam_ref, sub_ref, *rest, key_blocks, lam_init, n_cast):
    cast_in, (o_ref, *cast_out), (v1_ref, m_ref, acc_ref) = rest[:n_cast], rest[n_cast:2 * n_cast + 1], rest[-3:]
    for src, dst in zip(cast_in, cast_out):
        dst[...] = src[...].astype(BF16)
    tq = q_ref.shape[1]

    @pl.when(pl.program_id(2) == 0)
    def _():
        v1_ref[:, :HEAD_W] = v_ref[0]
        v1_ref[:, HEAD_W:] = jnp.ones((v1_ref.shape[0], HEAD_W), BF16)

    lane = lax.broadcasted_iota(jnp.int32, (tq, HEAD_W), 1)
    qf = q_ref[0].astype(F32) * (C_DH ** -0.5 * LOG2_E)
    q2 = jnp.concatenate([jnp.where(lane < C_DH, qf, 0.0), jnp.where(lane >= C_DH, qf, 0.0)], axis=0).astype(BF16)

    m_ref[...] = jnp.full_like(m_ref, NEG_BIG)
    acc_ref[...] = jnp.zeros_like(acc_ref)

    def scores(start, size):
        return _dot_nt(q2, k_ref[0, pl.ds(start, size), :])

    def absorb(s, start, size):
        m_old = m_ref[...]
        m_new = jnp.maximum(m_old, jnp.max(s, axis=-1, keepdims=True))
        alpha = jnp.exp2(m_old - m_new)
        p = jnp.exp2(s - jnp.concatenate([m_new] * (size // HEAD_W), axis=1))
        acc_ref[...] = (jnp.concatenate([alpha, alpha], axis=1) * acc_ref[...]
                        + _dot(p.astype(BF16), v1_ref[pl.ds(start, size), :]))
        m_ref[...] = m_new

    cur = scores(*key_blocks[0])
    for n, blk in enumerate(key_blocks):
        nxt = scores(*key_blocks[n + 1]) if n + 1 < len(key_blocks) else None
        absorb(cur, *blk)
        cur = nxt

    o12 = acc_ref[:, :HEAD_W] / acc_ref[:, HEAD_W:]
    o = o12[:tq] - lam_ref[...] * o12[tq:]
    o_ref[0] = (_rms(o) * sub_ref[...] * (1.0 - lam_init)).astype(o_ref.dtype)


def _attention(qc, kc, vc, lam, subln, lam_init, tq, q_block0, n_q, kv_block, key_blocks, to_bf16=()):
    bsz, _, w = qc.shape
    kv_rows, kv_idx = kv_block
    n_steps = bsz * C_HEADS * n_q
    cast_specs = []
    for a in to_bf16:
        rows = a.shape[0] // n_steps
        assert a.shape[0] == rows * n_steps and rows % BF16_ROWS == 0
        cast_specs.append(pl.BlockSpec((rows, a.shape[1]), lambda b, h, i: ((b * C_HEADS + h) * n_q + i, 0)))
    outs = pl.pallas_call(
        functools.partial(_attn_kernel, key_blocks=key_blocks, lam_init=lam_init, n_cast=len(to_bf16)),
        grid=(bsz, C_HEADS, n_q),
        in_specs=[pl.BlockSpec((1, tq, HEAD_W), lambda b, h, i: (b, q_block0 + i, h)),
                  pl.BlockSpec((1, kv_rows, HEAD_W), lambda b, h, i: (b, kv_idx, h)),
                  pl.BlockSpec((1, kv_rows, HEAD_W), lambda b, h, i: (b, kv_idx, h)),
                  pl.BlockSpec((1, HEAD_W), lambda b, h, i: (0, 0)),
                  pl.BlockSpec((1, HEAD_W), lambda b, h, i: (0, h))] + cast_specs,
        out_specs=[pl.BlockSpec((1, tq, HEAD_W), lambda b, h, i: (b, i, h))] + cast_specs,
        out_shape=[jax.ShapeDtypeStruct((bsz, n_q * tq, w), BF16)]
        + [jax.ShapeDtypeStruct(a.shape, BF16) for a in to_bf16],
        scratch_shapes=[pltpu.VMEM((kv_rows, 2 * HEAD_W), BF16), pltpu.VMEM((2 * tq, HEAD_W), F32),
                        pltpu.VMEM((2 * tq, 2 * HEAD_W), F32)],
        compiler_params=_cparams(("arbitrary", "arbitrary", "arbitrary")),
        name="diff_attention",
    )(qc, kc, vc, jnp.full((1, HEAD_W), lam, F32), subln.reshape(1, w), *to_bf16)
    return outs[0], outs[1:]


def _merge_kernel(xl_ref, xc_ref, mod_ref, of_ref, ob_ref, og_ref, gn_ref, uv_ref, vn_ref, ws_ref, bs_ref, ycl_ref,
                  ycc_ref, gt_ref, wb_ref, wo_ref, g2_ref, wr_ref, x_out_ref, h_out_ref, *maybe_logits_ref, d, n_lat_blocks):
    tm = xl_ref.shape[1]
    is_lat = pl.program_id(1) < n_lat_blocks
    mod = mod_ref[0]
    groups = [slice(r0, r0 + B_CHUNK) for r0 in range(0, tm, B_CHUNK)]
    st = [dict() for _ in groups]

    for s, rows in zip(st, groups):
        o = of_ref[0, rows, :].astype(F32) + ob_ref[0, rows, :].astype(F32)
        o = jnp.concatenate([_rms(o[:, h * HEAD_W:(h + 1) * HEAD_W]) for h in range(A_HEADS)], axis=1)
        s["ya"] = (o * gn_ref[...] * og_ref[0, rows, :].astype(F32)).astype(BF16)
        uv = uv_ref[0, rows, :].astype(F32)
        s["u"] = uv[:, :BRANCH_W]
        s["vv"] = (_rms(uv[:, BRANCH_W:]) * vn_ref[...]).astype(BF16)
    for s in st:
        s["mixed"] = jnp.concatenate(
            [_dot(ws_ref[g], s["vv"][:, g * HEAD_W:(g + 1) * HEAD_W]) for g in range(B_GROUPS)], axis=1)
    for s, rows in zip(st, groups):
        yb = (s["u"] * (s["mixed"] + bs_ref[...])).astype(BF16)
        yc = jnp.where(is_lat, ycl_ref[0, rows, :], ycc_ref[0, rows, :])
        s["proj"] = [_dot(y, wb_ref[i]) for i, y in enumerate((s["ya"], yb, yc))]
    for s, rows in zip(st, groups):
        merged = None
        for i in range(N_BRANCH):
            term = gt_ref[0, rows, i * d:(i + 1) * d].astype(F32) * s["proj"][i]
            merged = term if merged is None else merged + term
        s["mix"] = _dot(merged.astype(BF16), wo_ref[...])
    for s, rows in zip(st, groups):
        x_old = jnp.where(is_lat, xl_ref[0, rows, :], xc_ref[0, rows, :])
        x_new = x_old + mod[:, 2 * d:3 * d] * s["mix"]
        x_out_ref[0, rows, :] = x_new
        h = _rms(x_new) * g2_ref[...]
        h = h * (1.0 + mod[:, 4 * d:5 * d]) + mod[:, 3 * d:4 * d]
        h_hi = h.astype(BF16)
        h_out_ref[0, rows, :] = h_hi
        if maybe_logits_ref:
            h_lo = (h - h_hi.astype(F32)).astype(BF16)
            lg = (_dot(h_hi, wr_ref[0]) + _dot(h_lo, wr_ref[0])) + _dot(h_hi, wr_ref[1])
            maybe_logits_ref[0][0, rows, :] = lg
            maybe_logits_ref[1][:, rows] = lg.T


def _merge(x_lat, x_ctx, ctx_block0, mod3, proj, of, ob, yc_lat, yc_ctx, gnorm, vnorm, ws_bf, bs_full, wb_bf, wo_bf, g2, wr_split,
           n_lat_blocks, n_blocks, with_logits):
    bsz, _, d = x_lat.shape
    tm = ROW_TILE
    ctx_row = bsz

    def rows(w):
        return pl.BlockSpec((1, tm, w), lambda b, j: (b, j, 0))

    def const2(shape):
        return pl.BlockSpec(shape, lambda b, j: (0, 0))

    def mod_map(b, j):
        return (jnp.where(j < n_lat_blocks, b, ctx_row), 0, 0)

    n_rows = n_blocks * tm
    out_shape = [jax.ShapeDtypeStruct((bsz, n_rows, d), F32), jax.ShapeDtypeStruct((bsz, n_rows, d), BF16)]
    out_specs = [rows(d), rows(d)]
    if with_logits:
        out_shape += [jax.ShapeDtypeStruct((bsz, n_rows, HEAD_W), F32),
                      jax.ShapeDtypeStruct((HEAD_W, bsz * n_rows), F32)]
        out_specs += [rows(HEAD_W), pl.BlockSpec((HEAD_W, tm), lambda b, j: (0, b * n_blocks + j))]
    lat_spec, ctx_spec = _lat_ctx_specs(tm, d, n_lat_blocks, ctx_block0)
    return pl.pallas_call(
        functools.partial(_merge_kernel, d=d, n_lat_blocks=n_lat_blocks),
        grid=(bsz, n_blocks),
        in_specs=[lat_spec, ctx_spec, pl.BlockSpec((1, 1, mod3.shape[2]), mod_map),
                  rows(BRANCH_W), rows(BRANCH_W), rows(BRANCH_W), const2((1, BRANCH_W)),
                  rows(2 * BRANCH_W), const2((1, BRANCH_W)),
                  _resident((B_GROUPS, B_CHUNK, B_CHUNK), lambda b, j: (0, 0, 0)), const2((B_CHUNK, BRANCH_W)),
                  *_lat_ctx_specs(tm, BRANCH_W, n_lat_blocks, 0), rows(N_BRANCH * d),
                  _resident((N_BRANCH, BRANCH_W, d), lambda b, j: (0, 0, 0)), _resident((d, d), lambda b, j: (0, 0)),
                  const2((1, d)), _resident((2, d, HEAD_W), lambda b, j: (0, 0, 0))],
        out_specs=out_specs,
        out_shape=out_shape,
        compiler_params=_cparams(("arbitrary", "arbitrary"), V7X_VMEM_LIMIT),
        name="merge",
    )(x_lat, x_ctx, mod3, of, ob, proj["og"], gnorm.reshape(1, BRANCH_W), proj["uv"], vnorm.reshape(1, BRANCH_W),
      ws_bf, bs_full, yc_lat, yc_ctx, proj["gt"], wb_bf, wo_bf, g2.reshape(1, d), wr_split)


def _ffn_kernel(x_ref, h_ref, mod_ref, wg_ref, wu_ref, wd_ref, o_ref, *, d):
    h = h_ref[0]
    g = _dot(h, wg_ref[...])
    a = (g * _sigmoid(g) * _dot(h, wu_ref[...])).astype(BF16)
    y = _dot(a, wd_ref[...])
    o_ref[0] = x_ref[0] + mod_ref[0][:, 5 * d:6 * d] * y


def _ffn(xc, h2, mod3, wg_bf, wu_bf, wd_bf, n_lat_blocks):
    bsz, t, d = xc.shape
    tm = ROW_TILE
    dff = wg_bf.shape[1]
    ctx_row = bsz

    def mod_map(b, j):
        return (jnp.where(j < n_lat_blocks, b, ctx_row), 0, 0)

    rows = pl.BlockSpec((1, tm, d), lambda b, j: (b, j, 0))
    return pl.pallas_call(
        functools.partial(_ffn_kernel, d=d),
        grid=(bsz, t // tm),
        in_specs=[rows, rows, pl.BlockSpec((1, 1, mod3.shape[2]), mod_map),
                  _resident((d, dff), lambda b, j: (0, 0)), _resident((d, dff), lambda b, j: (0, 0)),
                  _resident((dff, d), lambda b, j: (0, 0))],
        out_specs=rows,
        out_shape=jax.ShapeDtypeStruct((bsz, t, d), F32),
        compiler_params=_cparams(("arbitrary", "arbitrary"), V7X_VMEM_LIMIT),
        name="dense_ffn",
    )(xc, h2, mod3, wg_bf, wu_bf, wd_bf)


def _moe_kernel(tile_expert_ref, n_tiles_ref, x_ref, wg_ref, wu_ref, wd_ref, *rest, tile0):
    del tile_expert_ref
    o_ref = rest[-1]
    in_use = pl.program_id(0) + tile0 < n_tiles_ref[0]

    @pl.when(in_use)
    def _():
        x = x_ref[...]
        dff = wg_ref.shape[2]
        acc = None
        for c0 in range(0, dff, FF_CHUNK):
            g = _dot(x, wg_ref[0, :, c0:c0 + FF_CHUNK])
            a = (g * _sigmoid(g) * _dot(x, wu_ref[0, :, c0:c0 + FF_CHUNK])).astype(BF16)
            y = _dot(a, wd_ref[0, c0:c0 + FF_CHUNK, :])
            acc = y if acc is None else acc + y
        o_ref[...] = acc.astype(o_ref.dtype)

    @pl.when(jnp.logical_not(in_use))
    def _():
        o_ref[...] = jnp.zeros_like(o_ref)


def _moe_experts(x_part, tile0, p, tile_expert, n_tiles, wg_bf, wu_bf, wd_bf, y_prev):
    d = x_part.shape[1]
    tm = MOE_TILE
    dff = wg_bf.shape[2]
    in_specs = [pl.BlockSpec((tm, d), lambda t, te, nt: (t, 0)),
                pl.BlockSpec((1, d, dff), lambda t, te, nt: (te[tile0 + t], 0, 0)),
                pl.BlockSpec((1, d, dff), lambda t, te, nt: (te[tile0 + t], 0, 0)),
                pl.BlockSpec((1, dff, d), lambda t, te, nt: (te[tile0 + t], 0, 0))]
    operands = [tile_expert, n_tiles, x_part, wg_bf, wu_bf, wd_bf]
    aliases = {}
    if y_prev is not None:
        in_specs.append(pl.BlockSpec(memory_space=pl.ANY))
        aliases = {len(operands): 0}
        operands.append(y_prev)
    grid_spec = pltpu.PrefetchScalarGridSpec(
        num_scalar_prefetch=2,
        grid=(x_part.shape[0] // tm,),
        in_specs=in_specs,
        out_specs=pl.BlockSpec((tm, d), lambda t, te, nt: (tile0 + t, 0)),
    )
    return pl.pallas_call(
        functools.partial(_moe_kernel, tile0=tile0),
        grid_spec=grid_spec,
        out_shape=jax.ShapeDtypeStruct((p, d), BF16),
        input_output_aliases=aliases,
        compiler_params=_cparams(("arbitrary",), V7X_VMEM_LIMIT),
        name="moe_experts",
    )(*operands)


def _route(logits_t, tm):
    n = logits_t.shape[1]
    experts = jnp.arange(N_EXPERTS, dtype=jnp.int32)[:, None]
    e1 = jnp.argmax(logits_t, axis=0).astype(jnp.int32)
    e2 = jnp.argmax(jnp.where(experts == e1[None, :], -jnp.inf, logits_t), axis=0).astype(jnp.int32)
    oh1 = (experts == e1[None, :]).astype(jnp.int32)
    oh2 = (experts == e2[None, :]).astype(jnp.int32)
    c1 = jnp.cumsum(oh1, axis=1)
    c2 = jnp.cumsum(oh2, axis=1) + c1[:, -1:]
    counts = c2[:, -1]
    padded = ((counts + tm - 1) // tm) * tm
    ends = jnp.cumsum(padded)
    starts = ends - padded
    dest1 = jnp.sum(oh1 * (starts[:, None] + c1 - 1), axis=0)
    dest2 = jnp.sum(oh2 * (starts[:, None] + c2 - 1), axis=0)
    m = TOP_K * n
    p = m + N_EXPERTS * tm
    tile_start = jnp.arange(p // tm, dtype=jnp.int32) * tm
    tile_expert = jnp.minimum(jnp.sum((tile_start[:, None] >= ends[None, :]).astype(jnp.int32), axis=1),
                              N_EXPERTS - 1).astype(jnp.int32)
    n_tiles = (ends[-1] // tm).astype(jnp.int32).reshape(1)
    token = jnp.arange(n, dtype=jnp.int32)
    by_expert = jnp.sort(jnp.concatenate([e1 * m + token, e2 * m + n + token])) % m
    first = jnp.cumsum(counts) - counts
    row_expert = jnp.repeat(tile_expert, tm)
    pos = jnp.arange(p, dtype=jnp.int32)
    entry = jnp.take(by_expert, (first[row_expert] + pos - starts[row_expert]) % m)
    src_token = jnp.where(entry < n, entry, entry - n)
    return dest1, dest2, src_token, tile_expert, n_tiles


def _final_kernel(x_ref, y0_ref, y1_ref, lg_ref, mod_ref, g_ref, o_ref, *, d):
    lane = lax.broadcasted_iota(jnp.int32, lg_ref.shape[1:], 1)
    lg = jnp.where(lane < N_EXPERTS, lg_ref[0], NEG_BIG)
    m1 = jnp.max(lg, axis=-1, keepdims=True)
    first = jnp.min(jnp.where(lg == m1, lane, HEAD_W), axis=-1, keepdims=True)
    m2 = jnp.max(jnp.where(lane == first, NEG_BIG, lg), axis=-1, keepdims=True)
    e = jnp.exp(m2 - m1)
    w1 = 1.0 / (1.0 + e)
    y = w1 * y0_ref[0].astype(F32) + (e * w1) * y1_ref[0].astype(F32)
    x = x_ref[0] + mod_ref[0][:, 5 * d:6 * d] * y
    o_ref[0] = _rms(x) * g_ref[...]


def _final(x_lat, y0, y1, logits, mod3, g_final):
    bsz, s, d = x_lat.shape
    tm = ROW_TILE
    rows = pl.BlockSpec((1, tm, d), lambda b, j: (b, j, 0))
    return pl.pallas_call(
        functools.partial(_final_kernel, d=d),
        grid=(bsz, s // tm),
        in_specs=[rows, rows, rows, pl.BlockSpec((1, tm, HEAD_W), lambda b, j: (b, j, 0)),
                  pl.BlockSpec((1, 1, mod3.shape[2]), lambda b, j: (b, 0, 0)),
                  pl.BlockSpec((1, d), lambda b, j: (0, 0))],
        out_specs=rows,
        out_shape=jax.ShapeDtypeStruct((bsz, s, d), F32),
        compiler_params=_cparams(("arbitrary", "arbitrary")),
        name="moe_combine_final_norm",
    )(x_lat, y0, y1, logits, mod3, g_final.reshape(1, d))


def _rope_tables(n_lat, n_ctx):
    pairs = C_DH // 4
    tpos = jnp.arange(n_lat)
    pos = jnp.stack([(tpos // GRID_W).astype(F32), (tpos % GRID_W).astype(F32)], axis=1)
    freqs = ROPE_BASE ** (-jnp.arange(pairs, dtype=F32) / pairs)
    ang = pos[:, :, None] * freqs[None, None, :]
    cos = jnp.repeat(jnp.cos(ang)[:, :, None, :], 2, axis=2).reshape(n_lat, C_DH)
    sin = jnp.sin(ang)[:, :, None, :] * jnp.array([-1.0, 1.0], F32)[None, None, :, None]
    sin = sin.reshape(n_lat, C_DH)
    cos = jnp.concatenate([cos, jnp.ones((n_ctx, C_DH), F32)], axis=0)
    sin = jnp.concatenate([sin, jnp.zeros((n_ctx, C_DH), F32)], axis=0)
    reps = 512 // C_DH
    return jnp.tile(cos, (1, reps)), jnp.tile(sin, (1, reps))


def _lower_bound(p, layer):
    cs = jnp.cumsum(jax.nn.softmax(p.astype(F32), axis=0), axis=0)
    return cs[layer] - cs[0]


def kernel(x, c, ctx, c_ctx, w_ada, b_ada, g_norm1, g_norm2, w_in, hgrn_lb, hgrn_gnorm, mlp_vnorm, mlp_ws, mlp_bs,
           diff_lambda, diff_subln, w_branch, w_out, ffn_wg, ffn_wu, ffn_wd, moe_router, moe_wg, moe_wu, moe_wd,
           g_final):
    bsz, n_lat, d = x.shape
    n_ctx = ctx.shape[1]
    depth = w_ada.shape[0]
    t = n_lat + n_ctx
    tm = ROW_TILE
    assert depth == 2 and bsz < MOD_ROWS and n_lat % GRID_W == 0
    assert n_lat % ATT_TQ == 0 and n_lat % ATT_TK == 0 and n_lat % n_ctx == 0 and n_ctx % tm == 0
    n_lat_blocks = n_lat // tm

    cvec = jnp.zeros((MOD_ROWS, d), F32).at[:bsz].set(c).at[bsz].set(c_ctx)
    mods = _adaln(cvec, w_ada, b_ada)
    cos_t, sin_t = _rope_tables(n_lat, n_ctx)
    x_lat, x_ctx, ctx_block0 = x, ctx, 0
    w_in_bf = w_in.astype(BF16)

    out = None
    for layer in range(depth):
        last = layer == depth - 1
        mod3 = mods[layer].reshape(MOD_ROWS, 1, 6 * d)
        lb_f = _lower_bound(hgrn_lb[0], layer)
        lb_b = _lower_bound(hgrn_lb[1], layer)
        proj = _inproj(x_lat, x_ctx, ctx_block0, t, mod3, g_norm1[layer], w_in_bf, layer, cos_t, sin_t,
                       lb_f, lb_b, n_lat_blocks)
        of, ob = _hgrn(proj["qa"], proj["bf"], proj["kf"], proj["bb"], proj["kb"], proj["ia"], n_lat)

        lam_init = 0.8 - 0.6 * math.exp(-0.3 * layer)
        lam_p = diff_lambda[layer]
        lam = jnp.exp(jnp.sum(lam_p[0] * lam_p[1])) - jnp.exp(jnp.sum(lam_p[2] * lam_p[3])) + lam_init
        lat_keys = tuple((i * ATT_TK, ATT_TK) for i in range(n_lat // ATT_TK)) + ((n_lat, n_ctx),)
        qkv = (proj["qc"], proj["kc"], proj["vc"], lam.astype(F32), diff_subln[layer], lam_init)
        j = layer // 2
        is_moe = layer % 2 == 1
        expert_w = ()
        if is_moe:
            n_e, _, dff = moe_wg[j].shape
            expert_w = (moe_wg[j].reshape(n_e * d, dff), moe_wu[j].reshape(n_e * d, dff),
                        moe_wd[j].reshape(n_e * dff, d))
        yc_lat, expert_w_bf = _attention(*qkv, ATT_TQ, 0, n_lat // ATT_TQ, (t, 0), lat_keys, expert_w)
        yc_ctx = yc_lat if last else _attention(*qkv, n_ctx, n_lat // n_ctx, 1, (n_ctx, n_lat // n_ctx),
                                                ((0, n_ctx),))[0]

        bs_full = jnp.repeat(mlp_bs[layer].T, HEAD_W, axis=1)
        wr_split = jnp.zeros((2, d, HEAD_W), BF16)
        if is_moe:
            wr_hi = moe_router[j].astype(BF16)
            wr_lo = (moe_router[j] - wr_hi.astype(F32)).astype(BF16)
            wr_split = wr_split.at[:, :, :N_EXPERTS].set(jnp.stack([wr_hi, wr_lo]))
        n_blocks = n_lat_blocks if last else t // tm
        merged = _merge(x_lat, x_ctx, ctx_block0, mod3, proj, of, ob, yc_lat, yc_ctx, hgrn_gnorm[layer], mlp_vnorm[layer],
                        mlp_ws[layer].astype(BF16), bs_full, w_branch[layer].astype(BF16), w_out[layer].astype(BF16),
                        g_norm2[layer], wr_split, n_lat_blocks, n_blocks, is_moe)
        x_new, h2 = merged[0], merged[1]

        if not is_moe:
            assert not last
            xc = _ffn(x_new, h2, mod3, ffn_wg[j].astype(BF16), ffn_wu[j].astype(BF16), ffn_wd[j].astype(BF16),
                      n_lat_blocks)
            x_lat, x_ctx, ctx_block0 = xc, xc, n_lat_blocks
        else:
            assert last
            n = bsz * n_lat
            logits, logits_t = merged[2], merged[3]
            dest1, dest2, src_token, tile_expert, n_tiles = _route(logits_t[:N_EXPERTS], MOE_TILE)
            p = src_token.shape[0]
            part_rows = p // MOE_PARTS
            assert part_rows % MOE_TILE == 0 and part_rows * MOE_PARTS == p
            expert_w = (expert_w_bf[0].reshape(n_e, d, dff), expert_w_bf[1].reshape(n_e, d, dff),
                        expert_w_bf[2].reshape(n_e, dff, d))
            y = None
            for i in range(MOE_PARTS):
                x_part = jnp.take(h2.reshape(n, d), src_token[i * part_rows:(i + 1) * part_rows], axis=0, mode="clip")
                y = _moe_experts(x_part, i * (part_rows // MOE_TILE), p, tile_expert, n_tiles, *expert_w, y)
            y0 = jnp.take(y, dest1, axis=0, mode="clip").reshape(bsz, n_lat, d)
            y1 = jnp.take(y, dest2, axis=0, mode="clip").reshape(bsz, n_lat, d)
            out = _final(x_new, y0, y1, logits, mod3, g_final)
    return out
```

```python
import functools
import math

import jax
import jax.numpy as jnp
from jax import lax
from jax.experimental import pallas as pl
from jax.experimental.pallas import tpu as pltpu

F32 = jnp.float32
BF16 = jnp.bfloat16

EPS = 1e-6
GRID_W = 64
ROPE_BASE = 10000.0

A_HEADS = 4
A_CHUNK = 64
A_SUB = 16
HGRN_ROWS = 256
MOE_TILE = 512
MOE_PARTS = 4
FINAL_PARTS = 2
B_GROUPS = 4
B_CHUNK = 128
C_HEADS = 4
C_DH = 64
HEAD_W = 128
BRANCH_W = 512
N_BRANCH = 3
N_EXPERTS = 8
TOP_K = 2

BF16_ROWS = 16
ROW_TILE = 256
MOD_ROWS = 16
ATT_TQ = 512
ATT_TK = 512
FF_CHUNK = 512
EXP_CLAMP = 80.0
LOG2_E = 1.4426950408889634
NEG_BIG = -1e30
V7X_VMEM_LIMIT = 56 * 1024 * 1024


def _cparams(sem, vmem=None):
    return pltpu.CompilerParams(dimension_semantics=sem, vmem_limit_bytes=vmem)


def _resident(shape, index_map):
    return pl.BlockSpec(shape, index_map, pipeline_mode=pl.Buffered(1))


def _lat_ctx_specs(tm, d, n_lat_blocks, ctx_block0):
    lat = pl.BlockSpec((1, tm, d), lambda b, j: (b, jnp.minimum(j, n_lat_blocks - 1), 0))
    ctx = pl.BlockSpec((1, tm, d), lambda b, j: (b, ctx_block0 + jnp.maximum(j - n_lat_blocks, 0), 0))
    return lat, ctx


def _rms(xf):
    return xf * lax.rsqrt(jnp.mean(xf * xf, axis=-1, keepdims=True) + EPS)


def _sigmoid(x):
    return 1.0 / (1.0 + jnp.exp(-x))


def _dot(a, b):
    return jnp.dot(a, b, preferred_element_type=F32)


def _dot_nt(a, b):
    return lax.dot_general(a, b, (((1,), (1,)), ((), ())), preferred_element_type=F32)


def _dot_tn(a, b):
    return lax.dot_general(a, b, (((0,), (0,)), ((), ())), preferred_element_type=F32)


def _adaln_kernel(c_ref, w_ref, b_ref, o_ref):
    c = c_ref[...]
    a = c * _sigmoid(c)
    o_ref[0] = jnp.dot(a, w_ref[0], preferred_element_type=F32, precision=lax.Precision.HIGHEST) + b_ref[0]


def _adaln(cvec, w_ada, b_ada):
    depth, d, n = w_ada.shape
    tn = 1536
    return pl.pallas_call(
        _adaln_kernel,
        grid=(depth, n // tn),
        in_specs=[pl.BlockSpec((MOD_ROWS, d), lambda l, j: (0, 0)),
                  pl.BlockSpec((1, d, tn), lambda l, j: (l, 0, j)),
                  pl.BlockSpec((1, 1, tn), lambda l, j: (l, 0, j))],
        out_specs=pl.BlockSpec((1, MOD_ROWS, tn), lambda l, j: (l, 0, j)),
        out_shape=jax.ShapeDtypeStruct((depth, MOD_ROWS, n), F32),
        compiler_params=_cparams(("arbitrary", "arbitrary")),
        name="adaln",
    )(cvec, w_ada, b_ada.reshape(depth, 1, n))


_IN_COLS = (("qa", 512), ("zf", 512), ("zb", 512), ("ia", 512), ("og", 512), ("uv", 1024), ("qc", 512), ("kc", 512),
            ("vc", 512), ("gt", 3072))
_IN_OUTS = (("qa", 512, BF16), ("bf", 512, F32), ("kf", 512, BF16), ("bb", 512, F32), ("kb", 512, BF16),
            ("ia", 512, BF16), ("og", 512, BF16), ("uv", 1024, BF16), ("qc", 512, BF16), ("kc", 512, BF16),
            ("vc", 512, BF16), ("gt", 3072, BF16))


def _gelu_tanh(x):
    return 0.5 * x * (1.0 + jnp.tanh(math.sqrt(2.0 / math.pi) * (x + 0.044715 * (x * x * x))))


_IN_ACTIVATIONS = {"og": lambda p: p * _sigmoid(p), "uv": _gelu_tanh, "gt": _sigmoid}


def _forget_gate(z, lb):
    sp = jnp.maximum(-z, 0.0) + jnp.log(1.0 + jnp.exp(-jnp.abs(z)))
    la = jnp.log(lb)
    lc = jnp.log1p(-lb) - sp
    logf = jnp.maximum(la, lc) + jnp.log(1.0 + jnp.exp(-jnp.abs(la - lc)))
    kin = (1.0 - lb) * jnp.exp(-z - sp)
    hi = logf.astype(BF16)
    rest = logf - hi.astype(F32)
    mid = rest.astype(BF16)
    low = (rest - mid.astype(F32)).astype(BF16)
    return (hi, mid, low), kin


def _inproj_kernel(xl_ref, xc_ref, mod_ref, g_ref, w_ref, cos_ref, sin_ref, lbf_ref, lbb_ref, trif_ref, trib_ref,
                   *out_refs, d, n_lat_blocks):
    x = jnp.where(pl.program_id(1) < n_lat_blocks, xl_ref[0], xc_ref[0])
    mod = mod_ref[0]
    h = _rms(x) * g_ref[...]
    h = (h * (1.0 + mod[:, d:2 * d]) + mod[:, 0:d]).astype(BF16)

    tm = x.shape[0]
    lane = lax.broadcasted_iota(jnp.int32, (tm, 512), 1)
    low_half = (lane % 32) < 16

    def rope(p):
        partner = jnp.where(low_half, pltpu.roll(p, 512 - 16, 1), pltpu.roll(p, 16, 1))
        return p * cos_ref[...] + partner * sin_ref[...]

    outs = {name: ref for (name, _, _), ref in zip(_IN_OUTS, out_refs)}
    col0 = {}
    lo = 0
    for name, width in _IN_COLS:
        col0[name] = lo
        lo += width

    gates = {}
    for name, lb_ref in (("zf", lbf_ref), ("zb", lbb_ref)):
        terms, kin = _forget_gate(_dot(h, w_ref[0, :, col0[name]:col0[name] + 512]), lb_ref[...])
        outs["k" + name[1]][0] = kin.astype(BF16)
        gates[name] = terms
    for name, width in _IN_COLS:
        if name in gates:
            continue
        for c0 in range(0, width, 512):
            p = _dot(h, w_ref[0, :, col0[name] + c0:col0[name] + c0 + 512])
            if name in ("qc", "kc"):
                p = rope(p)
            elif name in _IN_ACTIVATIONS:
                p = _IN_ACTIVATIONS[name](p)
            outs[name][0, :, c0:c0 + 512] = p.astype(outs[name].dtype)
    for name, tri_ref in (("zf", trif_ref), ("zb", trib_ref)):
        hi, mid, low = gates[name]
        outs["b" + name[1]][0] = _dot(tri_ref[...], hi) + _dot(tri_ref[...], mid) + _dot(tri_ref[...], low)


def _scan_matrices(tm):
    row = jnp.arange(tm)[:, None]
    col = jnp.arange(tm)[None, :]
    same = (row // A_CHUNK) == (col // A_CHUNK)
    return (same & (col <= row)).astype(BF16), (same & (col >= row)).astype(BF16)


def _inproj(x_lat, x_ctx, ctx_block0, t, mod3, g, w_bf, layer, cos_t, sin_t, lb_f, lb_b, n_lat_blocks):
    bsz, _, d = x_lat.shape
    tm = ROW_TILE
    ncols = w_bf.shape[2]
    ctx_row = bsz

    def mod_map(b, j):
        return (jnp.where(j < n_lat_blocks, b, ctx_row), 0, 0)

    out_shape = [jax.ShapeDtypeStruct((bsz, t, w), dt) for _, w, dt in _IN_OUTS]
    out_specs = [pl.BlockSpec((1, tm, w), lambda b, j: (b, j, 0)) for _, w, _ in _IN_OUTS]
    tri_f, tri_b = _scan_matrices(tm)
    vec = pl.BlockSpec((1, 512), lambda b, j: (0, 0))
    lat_spec, ctx_spec = _lat_ctx_specs(tm, d, n_lat_blocks, ctx_block0)
    outs = pl.pallas_call(
        functools.partial(_inproj_kernel, d=d, n_lat_blocks=n_lat_blocks),
        grid=(bsz, t // tm),
        in_specs=[lat_spec, ctx_spec,
                  pl.BlockSpec((1, 1, mod3.shape[2]), mod_map),
                  pl.BlockSpec((1, d), lambda b, j: (0, 0)),
                  _resident((1, d, ncols), lambda b, j: (layer, 0, 0)),
                  pl.BlockSpec((tm, 512), lambda b, j: (j, 0)),
                  pl.BlockSpec((tm, 512), lambda b, j: (j, 0)),
                  vec, vec, _resident((tm, tm), lambda b, j: (0, 0)), _resident((tm, tm), lambda b, j: (0, 0))],
        out_specs=out_specs,
        out_shape=out_shape,
        compiler_params=_cparams(("arbitrary", "arbitrary"), V7X_VMEM_LIMIT),
        name="inproj",
    )(x_lat, x_ctx, mod3, g.reshape(1, d), w_bf, cos_t, sin_t, lb_f.reshape(1, 512), lb_b.reshape(1, 512), tri_f, tri_b)
    return {name: o for (name, _, _), o in zip(_IN_OUTS, outs)}


def _hgrn_direction(q_ref, b_ref, k_ref, v_ref, o_ref, st_ref, reverse, rows, cross_group_only):
    c = A_CHUNK
    row = lax.broadcasted_iota(jnp.int32, (c, c), 0)
    col = lax.broadcasted_iota(jnp.int32, (c, c), 1)
    allowed = (col >= row) if reverse else (col <= row)
    if cross_group_only:
        allowed = allowed & ((row // A_SUB) != (col // A_SUB))

    b = b_ref[0, rows, :]
    tot_row = 0 if reverse else c - 1
    tot = b[tot_row:tot_row + 1, :]

    q = q_ref[0, rows, :].astype(F32)
    kin = k_ref[0, rows, :].astype(F32)
    v = v_ref[0, rows, :]
    zero_blk = jnp.zeros((A_SUB, HEAD_W), F32)
    heads = []
    for h in range(A_HEADS):
        hs = slice(h * HEAD_W, (h + 1) * HEAD_W)
        bh, qh, kh = b[:, hs], q[:, hs], kin[:, hs]
        q_state = qh * jnp.exp(bh)
        k_state = kh * jnp.exp(tot[:, hs] - bh)
        q_rows, k_cols = [], []
        for i in range(c // A_SUB):
            r0 = i * A_SUB
            ref = r0 + A_SUB - 1 if reverse else r0
            r = bh[ref:ref + 1, :]
            qt = qh[r0:r0 + A_SUB] * jnp.exp(bh[r0:r0 + A_SUB] - r)
            q_rows.append(jnp.concatenate([qt if j == i else zero_blk for j in range(c // A_SUB)], axis=1))
            live = slice(r0, c) if reverse else slice(0, r0 + A_SUB)
            kt = kh[live] * jnp.exp(jnp.minimum(r - bh[live], EXP_CLAMP))
            dead = jnp.zeros((c - kt.shape[0], HEAD_W), F32)
            k_cols.append(jnp.concatenate([dead, kt] if reverse else [kt, dead], axis=0) if dead.shape[0] else kt)
        heads.append(dict(
            hs=hs, h=h, rows=rows, allowed=allowed, vh=v[:, hs], decay=jnp.exp(tot[:, hs]), o_ref=o_ref, st_ref=st_ref,
            q_wide=jnp.concatenate(q_rows, axis=0).astype(BF16),
            k_wide=jnp.concatenate(k_cols, axis=1).astype(BF16),
            q_state=q_state.astype(BF16), k_state=k_state.astype(BF16)))
    return heads


def _hgrn_exact_groups(q_ref, b_ref, k_ref, v_ref, o_ref, reverse):
    sub = lax.broadcasted_iota(jnp.int32, (A_SUB, HEAD_W), 0)

    def chunk(c, carry):
        for g in range(A_CHUNK // A_SUB):
            rows = pl.ds(pl.multiple_of(c * A_CHUNK + g * A_SUB, A_SUB), A_SUB)
            for h in range(A_HEADS):
                hs = slice(h * HEAD_W, (h + 1) * HEAD_W)
                bg, qg = b_ref[0, rows, hs], q_ref[0, rows, hs].astype(F32)
                kg, vg = k_ref[0, rows, hs].astype(F32), v_ref[0, rows, hs].astype(F32)
                acc = jnp.zeros((A_SUB, HEAD_W), F32)
                for t in range(A_SUB):
                    live = (sub >= t) if reverse else (sub <= t)
                    w = jnp.exp(jnp.minimum(bg[t:t + 1] - bg, 0.0))
                    a = jnp.sum(jnp.where(live, qg[t:t + 1] * kg * w, 0.0), axis=-1, keepdims=True)
                    acc = jnp.where(sub == t, jnp.sum(a * vg, axis=0, keepdims=True), acc)
                o_ref[0, rows, hs] = (o_ref[0, rows, hs].astype(F32) + acc).astype(o_ref.dtype)
        return carry

    lax.fori_loop(0, q_ref.shape[1] // A_CHUNK, chunk, 0)


def _hgrn_kernel(qf_ref, bf_ref, kf_ref, vf_ref, qb_ref, bb_ref, kb_ref, vb_ref, of_ref, ob_ref, sf_ref, sb_ref):
    @pl.when(pl.program_id(1) == 0)
    def _():
        sf_ref[...] = jnp.zeros_like(sf_ref)
        sb_ref[...] = jnp.zeros_like(sb_ref)

    drops = None
    for b_ref in (bf_ref, bb_ref):
        for r0 in range(0, qf_ref.shape[1], A_SUB):
            d = jnp.abs(b_ref[0, r0:r0 + 1, :] - b_ref[0, r0 + A_SUB - 1:r0 + A_SUB, :])
            drops = d if drops is None else jnp.maximum(drops, d)
    within_range = jnp.max(drops) <= EXP_CLAMP

    @pl.when(within_range)
    def _():
        _hgrn_block(qf_ref, bf_ref, kf_ref, vf_ref, qb_ref, bb_ref, kb_ref, vb_ref, of_ref, ob_ref, sf_ref, sb_ref,
                    cross_group_only=False)

    @pl.when(jnp.logical_not(within_range))
    def _():
        _hgrn_block(qf_ref, bf_ref, kf_ref, vf_ref, qb_ref, bb_ref, kb_ref, vb_ref, of_ref, ob_ref, sf_ref, sb_ref,
                    cross_group_only=True)
        _hgrn_exact_groups(qf_ref, bf_ref, kf_ref, vf_ref, of_ref, reverse=False)
        _hgrn_exact_groups(qb_ref, bb_ref, kb_ref, vb_ref, ob_ref, reverse=True)


def _hgrn_block(qf_ref, bf_ref, kf_ref, vf_ref, qb_ref, bb_ref, kb_ref, vb_ref, of_ref, ob_ref, sf_ref, sb_ref,
                cross_group_only):
    n_sub = qf_ref.shape[1] // A_CHUNK
    seq = []
    for i in range(n_sub):
        up = slice(i * A_CHUNK, (i + 1) * A_CHUNK)
        down = slice((n_sub - 1 - i) * A_CHUNK, (n_sub - i) * A_CHUNK)
        seq.append(_hgrn_direction(qf_ref, bf_ref, kf_ref, vf_ref, of_ref, sf_ref, False, up, cross_group_only)
                   + _hgrn_direction(qb_ref, bb_ref, kb_ref, vb_ref, ob_ref, sb_ref, True, down, cross_group_only))
    for chains in seq:
        for ch in chains:
            ch["att"] = _dot_nt(ch["q_wide"], ch["k_wide"])
    state = [ch["st_ref"][ch["h"]] for ch in seq[0]]
    for chains in seq:
        for ch, st in zip(chains, state):
            ch["inter"] = _dot_nt(ch["q_state"], st.astype(BF16))
            ch["upd"] = _dot_tn(ch["vh"], ch["k_state"])
        for n, ch in enumerate(chains):
            att = jnp.where(ch["allowed"], ch["att"], 0.0).astype(BF16)
            o = _dot(att, ch["vh"]) + ch["inter"]
            ch["o_ref"][0, ch["rows"], ch["hs"]] = o.astype(ch["o_ref"].dtype)
            state[n] = state[n] * ch["decay"] + ch["upd"]
    for ch, st in zip(seq[0], state):
        ch["st_ref"][ch["h"]] = st


def _hgrn(qa, bf, kf, bb, kb, ia, n_lat):
    bsz, t, w = qa.shape
    c = HGRN_ROWS
    n_chunks = t // c
    lat_chunks = n_lat // c

    def fwd(b, s):
        return (b, (s + lat_chunks) % n_chunks, 0)

    def bwd(b, s):
        return (b, n_chunks - 1 - s, 0)

    blk = (1, c, w)
    return pl.pallas_call(
        _hgrn_kernel,
        grid=(bsz, n_chunks),
        in_specs=[pl.BlockSpec(blk, fwd)] * 4 + [pl.BlockSpec(blk, bwd)] * 4,
        out_specs=[pl.BlockSpec(blk, fwd), pl.BlockSpec(blk, bwd)],
        out_shape=[jax.ShapeDtypeStruct((bsz, t, w), BF16)] * 2,
        scratch_shapes=[pltpu.VMEM((A_HEADS, HEAD_W, HEAD_W), F32)] * 2,
        compiler_params=_cparams(("arbitrary", "arbitrary")),
        name="hgrn2",
    )(qa, bf, kf, ia, qa, bb, kb, ia)


def _attn_kernel(q_ref, k_ref, v_ref, lam_ref, sub_ref, *rest, key_blocks, lam_init, n_cast):
    cast_in, (o_ref, *cast_out), (v1_ref, m_ref, acc_ref) = rest[:n_cast], rest[n_cast:2 * n_cast + 1], rest[-3:]
    for src, dst in zip(cast_in, cast_out):
        dst[...] = src[...].astype(BF16)
    tq = q_ref.shape[1]

    @pl.when(pl.program_id(2) == 0)
    def _():
        v1_ref[:, :HEAD_W] = v_ref[0]
        v1_ref[:, HEAD_W:] = jnp.ones((v1_ref.shape[0], HEAD_W), BF16)

    lane = lax.broadcasted_iota(jnp.int32, (tq, HEAD_W), 1)
    qf = q_ref[0].astype(F32) * (C_DH ** -0.5 * LOG2_E)
    q2 = jnp.concatenate([jnp.where(lane < C_DH, qf, 0.0), jnp.where(lane >= C_DH, qf, 0.0)], axis=0).astype(BF16)

    m_ref[...] = jnp.full_like(m_ref, NEG_BIG)
    acc_ref[...] = jnp.zeros_like(acc_ref)

    def scores(start, size):
        return _dot_nt(q2, k_ref[0, pl.ds(start, size), :])

    def absorb(s, start, size):
        m_old = m_ref[...]
        m_new = jnp.maximum(m_old, jnp.max(s, axis=-1, keepdims=True))
        alpha = jnp.exp2(m_old - m_new)
        p = jnp.exp2(s - jnp.concatenate([m_new] * (size // HEAD_W), axis=1))
        acc_ref[...] = (jnp.concatenate([alpha, alpha], axis=1) * acc_ref[...]
                        + _dot(p.astype(BF16), v1_ref[pl.ds(start, size), :]))
        m_ref[...] = m_new

    cur = scores(*key_blocks[0])
    for n, blk in enumerate(key_blocks):
        nxt = scores(*key_blocks[n + 1]) if n + 1 < len(key_blocks) else None
        absorb(cur, *blk)
        cur = nxt

    o12 = acc_ref[:, :HEAD_W] / acc_ref[:, HEAD_W:]
    o = o12[:tq] - lam_ref[...] * o12[tq:]
    o_ref[0] = (_rms(o) * sub_ref[...] * (1.0 - lam_init)).astype(o_ref.dtype)


def _attention(qc, kc, vc, lam, subln, lam_init, tq, q_block0, n_q, kv_block, key_blocks, to_bf16=()):
    bsz, _, w = qc.shape
    kv_rows, kv_idx = kv_block
    n_steps = bsz * C_HEADS * n_q
    cast_specs = []
    for a in to_bf16:
        rows = a.shape[0] // n_steps
        assert a.shape[0] == rows * n_steps and rows % BF16_ROWS == 0
        cast_specs.append(pl.BlockSpec((rows, a.shape[1]), lambda b, h, i: ((b * C_HEADS + h) * n_q + i, 0)))
    outs = pl.pallas_call(
        functools.partial(_attn_kernel, key_blocks=key_blocks, lam_init=lam_init, n_cast=len(to_bf16)),
        grid=(bsz, C_HEADS, n_q),
        in_specs=[pl.BlockSpec((1, tq, HEAD_W), lambda b, h, i: (b, q_block0 + i, h)),
                  pl.BlockSpec((1, kv_rows, HEAD_W), lambda b, h, i: (b, kv_idx, h)),
                  pl.BlockSpec((1, kv_rows, HEAD_W), lambda b, h, i: (b, kv_idx, h)),
                  pl.BlockSpec((1, HEAD_W), lambda b, h, i: (0, 0)),
                  pl.BlockSpec((1, HEAD_W), lambda b, h, i: (0, h))] + cast_specs,
        out_specs=[pl.BlockSpec((1, tq, HEAD_W), lambda b, h, i: (b, i, h))] + cast_specs,
        out_shape=[jax.ShapeDtypeStruct((bsz, n_q * tq, w), BF16)]
        + [jax.ShapeDtypeStruct(a.shape, BF16) for a in to_bf16],
        scratch_shapes=[pltpu.VMEM((kv_rows, 2 * HEAD_W), BF16), pltpu.VMEM((2 * tq, HEAD_W), F32),
                        pltpu.VMEM((2 * tq, 2 * HEAD_W), F32)],
        compiler_params=_cparams(("arbitrary", "arbitrary", "arbitrary")),
        name="diff_attention",
    )(qc, kc, vc, jnp.full((1, HEAD_W), lam, F32), subln.reshape(1, w), *to_bf16)
    return outs[0], outs[1:]


def _merge_kernel(xl_ref, xc_ref, mod_ref, of_ref, ob_ref, og_ref, gn_ref, uv_ref, vn_ref, ws_ref, bs_ref, ycl_ref,
                  ycc_ref, gt_ref, wb_ref, wo_ref, g2_ref, wr_ref, x_out_ref, h_out_ref, *maybe_logits_ref, d, n_lat_blocks):
    tm = xl_ref.shape[1]
    is_lat = pl.program_id(1) < n_lat_blocks
    mod = mod_ref[0]
    groups = [slice(r0, r0 + B_CHUNK) for r0 in range(0, tm, B_CHUNK)]
    st = [dict() for _ in groups]

    for s, rows in zip(st, groups):
        o = of_ref[0, rows, :].astype(F32) + ob_ref[0, rows, :].astype(F32)
        o = jnp.concatenate([_rms(o[:, h * HEAD_W:(h + 1) * HEAD_W]) for h in range(A_HEADS)], axis=1)
        s["ya"] = (o * gn_ref[...] * og_ref[0, rows, :].astype(F32)).astype(BF16)
        uv = uv_ref[0, rows, :].astype(F32)
        s["u"] = uv[:, :BRANCH_W]
        s["vv"] = (_rms(uv[:, BRANCH_W:]) * vn_ref[...]).astype(BF16)
    for s in st:
        s["mixed"] = jnp.concatenate(
            [_dot(ws_ref[g], s["vv"][:, g * HEAD_W:(g + 1) * HEAD_W]) for g in range(B_GROUPS)], axis=1)
    for s, rows in zip(st, groups):
        yb = (s["u"] * (s["mixed"] + bs_ref[...])).astype(BF16)
        yc = jnp.where(is_lat, ycl_ref[0, rows, :], ycc_ref[0, rows, :])
        s["proj"] = [_dot(y, wb_ref[i]) for i, y in enumerate((s["ya"], yb, yc))]
    for s, rows in zip(st, groups):
        merged = None
        for i in range(N_BRANCH):
            term = gt_ref[0, rows, i * d:(i + 1) * d].astype(F32) * s["proj"][i]
            merged = term if merged is None else merged + term
        s["mix"] = _dot(merged.astype(BF16), wo_ref[...])
    for s, rows in zip(st, groups):
        x_old = jnp.where(is_lat, xl_ref[0, rows, :], xc_ref[0, rows, :])
        x_new = x_old + mod[:, 2 * d:3 * d] * s["mix"]
        x_out_ref[0, rows, :] = x_new
        h = _rms(x_new) * g2_ref[...]
        h = h * (1.0 + mod[:, 4 * d:5 * d]) + mod[:, 3 * d:4 * d]
        h_hi = h.astype(BF16)
        h_out_ref[0, rows, :] = h_hi
        if maybe_logits_ref:
            h_lo = (h - h_hi.astype(F32)).astype(BF16)
            lg = (_dot(h_hi, wr_ref[0]) + _dot(h_lo, wr_ref[0])) + _dot(h_hi, wr_ref[1])
            maybe_logits_ref[0][0, rows, :] = lg
            maybe_logits_ref[1][:, rows] = lg.T


def _merge(x_lat, x_ctx, ctx_block0, mod3, proj, of, ob, yc_lat, yc_ctx, gnorm, vnorm, ws_bf, bs_full, wb_bf, wo_bf, g2, wr_split,
           n_lat_blocks, n_blocks, with_logits):
    bsz, _, d = x_lat.shape
    tm = ROW_TILE
    ctx_row = bsz

    def rows(w):
        return pl.BlockSpec((1, tm, w), lambda b, j: (b, j, 0))

    def const2(shape):
        return pl.BlockSpec(shape, lambda b, j: (0, 0))

    def mod_map(b, j):
        return (jnp.where(j < n_lat_blocks, b, ctx_row), 0, 0)

    n_rows = n_blocks * tm
    out_shape = [jax.ShapeDtypeStruct((bsz, n_rows, d), F32), jax.ShapeDtypeStruct((bsz, n_rows, d), BF16)]
    out_specs = [rows(d), rows(d)]
    if with_logits:
        out_shape += [jax.ShapeDtypeStruct((bsz, n_rows, HEAD_W), F32),
                      jax.ShapeDtypeStruct((HEAD_W, bsz * n_rows), F32)]
        out_specs += [rows(HEAD_W), pl.BlockSpec((HEAD_W, tm), lambda b, j: (0, b * n_blocks + j))]
    lat_spec, ctx_spec = _lat_ctx_specs(tm, d, n_lat_blocks, ctx_block0)
    return pl.pallas_call(
        functools.partial(_merge_kernel, d=d, n_lat_blocks=n_lat_blocks),
        grid=(bsz, n_blocks),
        in_specs=[lat_spec, ctx_spec, pl.BlockSpec((1, 1, mod3.shape[2]), mod_map),
                  rows(BRANCH_W), rows(BRANCH_W), rows(BRANCH_W), const2((1, BRANCH_W)),
                  rows(2 * BRANCH_W), const2((1, BRANCH_W)),
                  _resident((B_GROUPS, B_CHUNK, B_CHUNK), lambda b, j: (0, 0, 0)), const2((B_CHUNK, BRANCH_W)),
                  *_lat_ctx_specs(tm, BRANCH_W, n_lat_blocks, 0), rows(N_BRANCH * d),
                  _resident((N_BRANCH, BRANCH_W, d), lambda b, j: (0, 0, 0)), _resident((d, d), lambda b, j: (0, 0)),
                  const2((1, d)), _resident((2, d, HEAD_W), lambda b, j: (0, 0, 0))],
        out_specs=out_specs,
        out_shape=out_shape,
        compiler_params=_cparams(("arbitrary", "arbitrary"), V7X_VMEM_LIMIT),
        name="merge",
    )(x_lat, x_ctx, mod3, of, ob, proj["og"], gnorm.reshape(1, BRANCH_W), proj["uv"], vnorm.reshape(1, BRANCH_W),
      ws_bf, bs_full, yc_lat, yc_ctx, proj["gt"], wb_bf, wo_bf, g2.reshape(1, d), wr_split)


def _ffn_kernel(x_ref, h_ref, mod_ref, wg_ref, wu_ref, wd_ref, o_ref, *, d):
    h = h_ref[0]
    g = _dot(h, wg_ref[...])
    a = (g * _sigmoid(g) * _dot(h, wu_ref[...])).astype(BF16)
    y = _dot(a, wd_ref[...])
    o_ref[0] = x_ref[0] + mod_ref[0][:, 5 * d:6 * d] * y


def _ffn(xc, h2, mod3, wg_bf, wu_bf, wd_bf, n_lat_blocks):
    bsz, t, d = xc.shape
    tm = ROW_TILE
    dff = wg_bf.shape[1]
    ctx_row = bsz

    def mod_map(b, j):
        return (jnp.where(j < n_lat_blocks, b, ctx_row), 0, 0)

    rows = pl.BlockSpec((1, tm, d), lambda b, j: (b, j, 0))
    return pl.pallas_call(
        functools.partial(_ffn_kernel, d=d),
        grid=(bsz, t // tm),
        in_specs=[rows, rows, pl.BlockSpec((1, 1, mod3.shape[2]), mod_map),
                  _resident((d, dff), lambda b, j: (0, 0)), _resident((d, dff), lambda b, j: (0, 0)),
                  _resident((dff, d), lambda b, j: (0, 0))],
        out_specs=rows,
        out_shape=jax.ShapeDtypeStruct((bsz, t, d), F32),
        compiler_params=_cparams(("arbitrary", "arbitrary"), V7X_VMEM_LIMIT),
        name="dense_ffn",
    )(xc, h2, mod3, wg_bf, wu_bf, wd_bf)


def _moe_kernel(tile_expert_ref, n_tiles_ref, x_ref, wg_ref, wu_ref, wd_ref, *rest, tile0):
    del tile_expert_ref
    o_ref = rest[-1]
    in_use = pl.program_id(0) + tile0 < n_tiles_ref[0]

    @pl.when(in_use)
    def _():
        x = x_ref[...]
        dff = wg_ref.shape[2]
        acc = None
        for c0 in range(0, dff, FF_CHUNK):
            g = _dot(x, wg_ref[0, :, c0:c0 + FF_CHUNK])
            a = (g * _sigmoid(g) * _dot(x, wu_ref[0, :, c0:c0 + FF_CHUNK])).astype(BF16)
            y = _dot(a, wd_ref[0, c0:c0 + FF_CHUNK, :])
            acc = y if acc is None else acc + y
        o_ref[...] = acc.astype(o_ref.dtype)

    @pl.when(jnp.logical_not(in_use))
    def _():
        o_ref[...] = jnp.zeros_like(o_ref)


def _moe_experts(x_part, tile0, p, tile_expert, n_tiles, wg_bf, wu_bf, wd_bf, y_prev):
    d = x_part.shape[1]
    tm = MOE_TILE
    dff = wg_bf.shape[2]
    in_specs = [pl.BlockSpec((tm, d), lambda t, te, nt: (t, 0)),
                pl.BlockSpec((1, d, dff), lambda t, te, nt: (te[tile0 + t], 0, 0)),
                pl.BlockSpec((1, d, dff), lambda t, te, nt: (te[tile0 + t], 0, 0)),
                pl.BlockSpec((1, dff, d), lambda t, te, nt: (te[tile0 + t], 0, 0))]
    operands = [tile_expert, n_tiles, x_part, wg_bf, wu_bf, wd_bf]
    aliases = {}
    if y_prev is not None:
        in_specs.append(pl.BlockSpec(memory_space=pl.ANY))
        aliases = {len(operands): 0}
        operands.append(y_prev)
    grid_spec = pltpu.PrefetchScalarGridSpec(
        num_scalar_prefetch=2,
        grid=(x_part.shape[0] // tm,),
        in_specs=in_specs,
        out_specs=pl.BlockSpec((tm, d), lambda t, te, nt: (tile0 + t, 0)),
    )
    return pl.pallas_call(
        functools.partial(_moe_kernel, tile0=tile0),
        grid_spec=grid_spec,
        out_shape=jax.ShapeDtypeStruct((p, d), BF16),
        input_output_aliases=aliases,
        compiler_params=_cparams(("arbitrary",), V7X_VMEM_LIMIT),
        name="moe_experts",
    )(*operands)


def _route(logits_t, tm):
    n = logits_t.shape[1]
    experts = jnp.arange(N_EXPERTS, dtype=jnp.int32)[:, None]
    e1 = jnp.argmax(logits_t, axis=0).astype(jnp.int32)
    e2 = jnp.argmax(jnp.where(experts == e1[None, :], -jnp.inf, logits_t), axis=0).astype(jnp.int32)
    oh1 = (experts == e1[None, :]).astype(jnp.int32)
    oh2 = (experts == e2[None, :]).astype(jnp.int32)
    c1 = jnp.cumsum(oh1, axis=1)
    c2 = jnp.cumsum(oh2, axis=1) + c1[:, -1:]
    counts = c2[:, -1]
    padded = ((counts + tm - 1) // tm) * tm
    ends = jnp.cumsum(padded)
    starts = ends - padded
    dest1 = jnp.sum(oh1 * (starts[:, None] + c1 - 1), axis=0)
    dest2 = jnp.sum(oh2 * (starts[:, None] + c2 - 1), axis=0)
    m = TOP_K * n
    p = m + N_EXPERTS * tm
    tile_start = jnp.arange(p // tm, dtype=jnp.int32) * tm
    tile_expert = jnp.minimum(jnp.sum((tile_start[:, None] >= ends[None, :]).astype(jnp.int32), axis=1),
                              N_EXPERTS - 1).astype(jnp.int32)
    n_tiles = (ends[-1] // tm).astype(jnp.int32).reshape(1)
    token = jnp.arange(n, dtype=jnp.int32)
    by_expert = jnp.sort(jnp.concatenate([e1 * m + token, e2 * m + n + token])) % m
    first = jnp.cumsum(counts) - counts
    row_expert = jnp.repeat(tile_expert, tm)
    pos = jnp.arange(p, dtype=jnp.int32)
    entry = jnp.take(by_expert, (first[row_expert] + pos - starts[row_expert]) % m)
    src_token = jnp.where(entry < n, entry, entry - n)
    return dest1, dest2, src_token, tile_expert, n_tiles


def _final_kernel(x_ref, y0_ref, y1_ref, lg_ref, mod_ref, g_ref, *rest, d):
    o_ref = rest[-1]
    lane = lax.broadcasted_iota(jnp.int32, lg_ref.shape[1:], 1)
    lg = jnp.where(lane < N_EXPERTS, lg_ref[0], NEG_BIG)
    m1 = jnp.max(lg, axis=-1, keepdims=True)
    first = jnp.min(jnp.where(lg == m1, lane, HEAD_W), axis=-1, keepdims=True)
    m2 = jnp.max(jnp.where(lane == first, NEG_BIG, lg), axis=-1, keepdims=True)
    e = jnp.exp(m2 - m1)
    w1 = 1.0 / (1.0 + e)
    y = w1 * y0_ref[0].astype(F32) + (e * w1) * y1_ref[0].astype(F32)
    x = x_ref[0] + mod_ref[0][:, 5 * d:6 * d] * y
    o_ref[0] = _rms(x) * g_ref[...]


def _final(x_lat, y0, y1, logits, mod3, g_final, b0, out_prev):
    bsz, s, d = x_lat.shape
    tm = ROW_TILE
    here = pl.BlockSpec((1, tm, d), lambda b, j: (b0 + b, j, 0))
    part = pl.BlockSpec((1, tm, d), lambda b, j: (b, j, 0))
    in_specs = [here, part, part, pl.BlockSpec((1, tm, HEAD_W), lambda b, j: (b0 + b, j, 0)),
                pl.BlockSpec((1, 1, mod3.shape[2]), lambda b, j: (b0 + b, 0, 0)),
                pl.BlockSpec((1, d), lambda b, j: (0, 0))]
    operands = [x_lat, y0, y1, logits, mod3, g_final.reshape(1, d)]
    aliases = {}
    if out_prev is not None:
        in_specs.append(pl.BlockSpec(memory_space=pl.ANY))
        aliases = {len(operands): 0}
        operands.append(out_prev)
    return pl.pallas_call(
        functools.partial(_final_kernel, d=d),
        grid=(y0.shape[0], s // tm),
        in_specs=in_specs,
        out_specs=here,
        out_shape=jax.ShapeDtypeStruct((bsz, s, d), F32),
        input_output_aliases=aliases,
        compiler_params=_cparams(("arbitrary", "arbitrary")),
        name="moe_combine_final_norm",
    )(*operands)


def _rope_tables(n_lat, n_ctx):
    pairs = C_DH // 4
    tpos = jnp.arange(n_lat)
    pos = jnp.stack([(tpos // GRID_W).astype(F32), (tpos % GRID_W).astype(F32)], axis=1)
    freqs = ROPE_BASE ** (-jnp.arange(pairs, dtype=F32) / pairs)
    ang = pos[:, :, None] * freqs[None, None, :]
    cos = jnp.repeat(jnp.cos(ang)[:, :, None, :], 2, axis=2).reshape(n_lat, C_DH)
    sin = jnp.sin(ang)[:, :, None, :] * jnp.array([-1.0, 1.0], F32)[None, None, :, None]
    sin = sin.reshape(n_lat, C_DH)
    cos = jnp.concatenate([cos, jnp.ones((n_ctx, C_DH), F32)], axis=0)
    sin = jnp.concatenate([sin, jnp.zeros((n_ctx, C_DH), F32)], axis=0)
    reps = 512 // C_DH
    return jnp.tile(cos, (1, reps)), jnp.tile(sin, (1, reps))


def _lower_bound(p, layer):
    cs = jnp.cumsum(jax.nn.softmax(p.astype(F32), axis=0), axis=0)
    return cs[layer] - cs[0]


def kernel(x, c, ctx, c_ctx, w_ada, b_ada, g_norm1, g_norm2, w_in, hgrn_lb, hgrn_gnorm, mlp_vnorm, mlp_ws, mlp_bs,
           diff_lambda, diff_subln, w_branch, w_out, ffn_wg, ffn_wu, ffn_wd, moe_router, moe_wg, moe_wu, moe_wd,
           g_final):
    bsz, n_lat, d = x.shape
    n_ctx = ctx.shape[1]
    depth = w_ada.shape[0]
    t = n_lat + n_ctx
    tm = ROW_TILE
    assert depth == 2 and bsz < MOD_ROWS and n_lat % GRID_W == 0
    assert n_lat % ATT_TQ == 0 and n_lat % ATT_TK == 0 and n_lat % n_ctx == 0 and n_ctx % tm == 0
    n_lat_blocks = n_lat // tm

    cvec = jnp.zeros((MOD_ROWS, d), F32).at[:bsz].set(c).at[bsz].set(c_ctx)
    mods = _adaln(cvec, w_ada, b_ada)
    cos_t, sin_t = _rope_tables(n_lat, n_ctx)
    x_lat, x_ctx, ctx_block0 = x, ctx, 0
    w_in_bf = w_in.astype(BF16)

    out = None
    for layer in range(depth):
        last = layer == depth - 1
        mod3 = mods[layer].reshape(MOD_ROWS, 1, 6 * d)
        lb_f = _lower_bound(hgrn_lb[0], layer)
        lb_b = _lower_bound(hgrn_lb[1], layer)
        proj = _inproj(x_lat, x_ctx, ctx_block0, t, mod3, g_norm1[layer], w_in_bf, layer, cos_t, sin_t,
                       lb_f, lb_b, n_lat_blocks)
        of, ob = _hgrn(proj["qa"], proj["bf"], proj["kf"], proj["bb"], proj["kb"], proj["ia"], n_lat)

        lam_init = 0.8 - 0.6 * math.exp(-0.3 * layer)
        lam_p = diff_lambda[layer]
        lam = jnp.exp(jnp.sum(lam_p[0] * lam_p[1])) - jnp.exp(jnp.sum(lam_p[2] * lam_p[3])) + lam_init
        lat_keys = tuple((i * ATT_TK, ATT_TK) for i in range(n_lat // ATT_TK)) + ((n_lat, n_ctx),)
        qkv = (proj["qc"], proj["kc"], proj["vc"], lam.astype(F32), diff_subln[layer], lam_init)
        j = layer // 2
        is_moe = layer % 2 == 1
        expert_w = ()
        if is_moe:
            n_e, _, dff = moe_wg[j].shape
            expert_w = (moe_wg[j].reshape(n_e * d, dff), moe_wu[j].reshape(n_e * d, dff),
                        moe_wd[j].reshape(n_e * dff, d))
        yc_lat, expert_w_bf = _attention(*qkv, ATT_TQ, 0, n_lat // ATT_TQ, (t, 0), lat_keys, expert_w)
        yc_ctx = yc_lat if last else _attention(*qkv, n_ctx, n_lat // n_ctx, 1, (n_ctx, n_lat // n_ctx),
                                                ((0, n_ctx),))[0]

        bs_full = jnp.repeat(mlp_bs[layer].T, HEAD_W, axis=1)
        wr_split = jnp.zeros((2, d, HEAD_W), BF16)
        if is_moe:
            wr_hi = moe_router[j].astype(BF16)
            wr_lo = (moe_router[j] - wr_hi.astype(F32)).astype(BF16)
            wr_split = wr_split.at[:, :, :N_EXPERTS].set(jnp.stack([wr_hi, wr_lo]))
        n_blocks = n_lat_blocks if last else t // tm
        merged = _merge(x_lat, x_ctx, ctx_block0, mod3, proj, of, ob, yc_lat, yc_ctx, hgrn_gnorm[layer], mlp_vnorm[layer],
                        mlp_ws[layer].astype(BF16), bs_full, w_branch[layer].astype(BF16), w_out[layer].astype(BF16),
                        g_norm2[layer], wr_split, n_lat_blocks, n_blocks, is_moe)
        x_new, h2 = merged[0], merged[1]

        if not is_moe:
            assert not last
            xc = _ffn(x_new, h2, mod3, ffn_wg[j].astype(BF16), ffn_wu[j].astype(BF16), ffn_wd[j].astype(BF16),
                      n_lat_blocks)
            x_lat, x_ctx, ctx_block0 = xc, xc, n_lat_blocks
        else:
            assert last
            n = bsz * n_lat
            logits, logits_t = merged[2], merged[3]
            dest1, dest2, src_token, tile_expert, n_tiles = _route(logits_t[:N_EXPERTS], MOE_TILE)
            p = src_token.shape[0]
            part_rows = p // MOE_PARTS
            assert part_rows % MOE_TILE == 0 and part_rows * MOE_PARTS == p
            expert_w = (expert_w_bf[0].reshape(n_e, d, dff), expert_w_bf[1].reshape(n_e, d, dff),
                        expert_w_bf[2].reshape(n_e, dff, d))
            y = None
            for i in range(MOE_PARTS):
                x_part = jnp.take(h2.reshape(n, d), src_token[i * part_rows:(i + 1) * part_rows], axis=0, mode="clip")
                y = _moe_experts(x_part, i * (part_rows // MOE_TILE), p, tile_expert, n_tiles, *expert_w, y)
            assert bsz % FINAL_PARTS == 0
            part_b = bsz // FINAL_PARTS
            for i in range(FINAL_PARTS):
                tokens = slice(i * part_b * n_lat, (i + 1) * part_b * n_lat)
                y0 = jnp.take(y, dest1[tokens], axis=0, mode="clip").reshape(part_b, n_lat, d)
                y1 = jnp.take(y, dest2[tokens], axis=0, mode="clip").reshape(part_b, n_lat, d)
                out = _final(x_new, y0, y1, logits, mod3, g_final, i * part_b, out)
    return out
```

```python
import functools
import math

import jax
import jax.numpy as jnp
from jax import lax
from jax.experimental import pallas as pl
from jax.experimental.pallas import tpu as pltpu

F32 = jnp.float32
BF16 = jnp.bfloat16

EPS = 1e-6
GRID_W = 64
ROPE_BASE = 10000.0

A_HEADS = 4
A_CHUNK = 64
A_SUB = 16
HGRN_ROWS = 256
MOE_TILE = 512
MOE_PARTS = 4
FINAL_PARTS = 2
B_GROUPS = 4
B_CHUNK = 128
C_HEADS = 4
C_DH = 64
ROPE_PAIRS = C_DH // 4
HEAD_W = 128
BRANCH_W = 512
D_MODEL = 1024
N_BRANCH = 3
N_EXPERTS = 8
TOP_K = 2

BF16_ROWS = 16
ROW_TILE = 256
MOD_ROWS = 16
ATT_TQ = 512
ATT_TK = 512
ADALN_TN = 1536
FF_CHUNK = 512
EXP_CLAMP = 80.0
LOG2_E = 1.4426950408889634
NEG_BIG = -1e30
V7X_VMEM_LIMIT = 56 * 1024 * 1024


def _cparams(sem, vmem=None):
    return pltpu.CompilerParams(dimension_semantics=sem, vmem_limit_bytes=vmem)


def _resident(shape, index_map):
    return pl.BlockSpec(shape, index_map, pipeline_mode=pl.Buffered(1))


def _lat_ctx_specs(tm, d, n_lat_blocks, ctx_block0):
    lat = pl.BlockSpec((1, tm, d), lambda b, j: (b, jnp.minimum(j, n_lat_blocks - 1), 0))
    ctx = pl.BlockSpec((1, tm, d), lambda b, j: (b, ctx_block0 + jnp.maximum(j - n_lat_blocks, 0), 0))
    return lat, ctx


def _rms(xf):
    return xf * lax.rsqrt(jnp.mean(xf * xf, axis=-1, keepdims=True) + EPS)


def _sigmoid(x):
    return 1.0 / (1.0 + jnp.exp(-x))


def _dot(a, b):
    return jnp.dot(a, b, preferred_element_type=F32)


def _dot_nt(a, b):
    return lax.dot_general(a, b, (((1,), (1,)), ((), ())), preferred_element_type=F32)


def _dot_tn(a, b):
    return lax.dot_general(a, b, (((0,), (0,)), ((), ())), preferred_element_type=F32)


def _adaln_kernel(c_ref, w_ref, b_ref, o_ref):
    c = c_ref[...]
    a = c * _sigmoid(c)
    o_ref[0] = jnp.dot(a, w_ref[0], preferred_element_type=F32, precision=lax.Precision.HIGHEST) + b_ref[0]


def _adaln(cvec, w_ada, b_ada):
    depth, d, n = w_ada.shape
    tn = ADALN_TN
    return pl.pallas_call(
        _adaln_kernel,
        grid=(depth, n // tn),
        in_specs=[pl.BlockSpec((MOD_ROWS, d), lambda l, j: (0, 0)),
                  pl.BlockSpec((1, d, tn), lambda l, j: (l, 0, j)),
                  pl.BlockSpec((1, 1, tn), lambda l, j: (l, 0, j))],
        out_specs=pl.BlockSpec((1, MOD_ROWS, tn), lambda l, j: (l, 0, j)),
        out_shape=jax.ShapeDtypeStruct((depth, MOD_ROWS, n), F32),
        compiler_params=_cparams(("arbitrary", "arbitrary")),
        name="adaln",
    )(cvec, w_ada, b_ada.reshape(depth, 1, n))


_W = BRANCH_W
_IN_COLS = (("qa", _W), ("zf", _W), ("zb", _W), ("ia", _W), ("og", _W), ("uv", 2 * _W), ("qc", _W), ("kc", _W),
            ("vc", _W), ("gt", N_BRANCH * D_MODEL))
_IN_OUTS = (("qa", _W, BF16), ("bf", _W, F32), ("kf", _W, BF16), ("bb", _W, F32), ("kb", _W, BF16),
            ("ia", _W, BF16), ("og", _W, BF16), ("uv", 2 * _W, BF16), ("qc", _W, BF16), ("kc", _W, BF16),
            ("vc", _W, BF16), ("gt", N_BRANCH * D_MODEL, BF16))


def _gelu_tanh(x):
    return 0.5 * x * (1.0 + jnp.tanh(math.sqrt(2.0 / math.pi) * (x + 0.044715 * (x * x * x))))


_IN_ACTIVATIONS = {"og": lambda p: p * _sigmoid(p), "uv": _gelu_tanh, "gt": _sigmoid}


def _forget_gate(z, lb):
    sp = jnp.maximum(-z, 0.0) + jnp.log(1.0 + jnp.exp(-jnp.abs(z)))
    la = jnp.log(lb)
    lc = jnp.log1p(-lb) - sp
    logf = jnp.maximum(la, lc) + jnp.log(1.0 + jnp.exp(-jnp.abs(la - lc)))
    kin = (1.0 - lb) * jnp.exp(-z - sp)
    hi = logf.astype(BF16)
    rest = logf - hi.astype(F32)
    mid = rest.astype(BF16)
    low = (rest - mid.astype(F32)).astype(BF16)
    return (hi, mid, low), kin


def _inproj_kernel(xl_ref, xc_ref, mod_ref, g_ref, w_ref, cos_ref, sin_ref, lbf_ref, lbb_ref, trif_ref, trib_ref,
                   *out_refs, d, n_lat_blocks):
    x = jnp.where(pl.program_id(1) < n_lat_blocks, xl_ref[0], xc_ref[0])
    mod = mod_ref[0]
    h = _rms(x) * g_ref[...]
    h = (h * (1.0 + mod[:, d:2 * d]) + mod[:, 0:d]).astype(BF16)

    tm = x.shape[0]
    lane = lax.broadcasted_iota(jnp.int32, (tm, BRANCH_W), 1)
    low_half = (lane % (2 * ROPE_PAIRS)) < ROPE_PAIRS

    def rope(p):
        partner = jnp.where(low_half, pltpu.roll(p, BRANCH_W - ROPE_PAIRS, 1), pltpu.roll(p, ROPE_PAIRS, 1))
        return p * cos_ref[...] + partner * sin_ref[...]

    outs = {name: ref for (name, _, _), ref in zip(_IN_OUTS, out_refs)}
    col0 = {}
    lo = 0
    for name, width in _IN_COLS:
        col0[name] = lo
        lo += width

    gates = {}
    for name, lb_ref in (("zf", lbf_ref), ("zb", lbb_ref)):
        terms, kin = _forget_gate(_dot(h, w_ref[0, :, col0[name]:col0[name] + BRANCH_W]), lb_ref[...])
        outs["k" + name[1]][0] = kin.astype(BF16)
        gates[name] = terms
    for name, width in _IN_COLS:
        if name in gates:
            continue
        for c0 in range(0, width, BRANCH_W):
            p = _dot(h, w_ref[0, :, col0[name] + c0:col0[name] + c0 + BRANCH_W])
            if name in ("qc", "kc"):
                p = rope(p)
            elif name in _IN_ACTIVATIONS:
                p = _IN_ACTIVATIONS[name](p)
            outs[name][0, :, c0:c0 + BRANCH_W] = p.astype(outs[name].dtype)
    for name, tri_ref in (("zf", trif_ref), ("zb", trib_ref)):
        hi, mid, low = gates[name]
        outs["b" + name[1]][0] = _dot(tri_ref[...], hi) + _dot(tri_ref[...], mid) + _dot(tri_ref[...], low)


def _scan_matrices(tm):
    row = jnp.arange(tm)[:, None]
    col = jnp.arange(tm)[None, :]
    same = (row // A_CHUNK) == (col // A_CHUNK)
    return (same & (col <= row)).astype(BF16), (same & (col >= row)).astype(BF16)


def _inproj(x_lat, x_ctx, ctx_block0, t, mod3, g, w_bf, layer, cos_t, sin_t, lb_f, lb_b, n_lat_blocks):
    bsz, _, d = x_lat.shape
    tm = ROW_TILE
    ncols = w_bf.shape[2]
    ctx_row = bsz

    def mod_map(b, j):
        return (jnp.where(j < n_lat_blocks, b, ctx_row), 0, 0)

    out_shape = [jax.ShapeDtypeStruct((bsz, t, w), dt) for _, w, dt in _IN_OUTS]
    out_specs = [pl.BlockSpec((1, tm, w), lambda b, j: (b, j, 0)) for _, w, _ in _IN_OUTS]
    tri_f, tri_b = _scan_matrices(tm)
    vec = pl.BlockSpec((1, BRANCH_W), lambda b, j: (0, 0))
    lat_spec, ctx_spec = _lat_ctx_specs(tm, d, n_lat_blocks, ctx_block0)
    outs = pl.pallas_call(
        functools.partial(_inproj_kernel, d=d, n_lat_blocks=n_lat_blocks),
        grid=(bsz, t // tm),
        in_specs=[lat_spec, ctx_spec,
                  pl.BlockSpec((1, 1, mod3.shape[2]), mod_map),
                  pl.BlockSpec((1, d), lambda b, j: (0, 0)),
                  _resident((1, d, ncols), lambda b, j: (layer, 0, 0)),
                  pl.BlockSpec((tm, BRANCH_W), lambda b, j: (j, 0)),
                  pl.BlockSpec((tm, BRANCH_W), lambda b, j: (j, 0)),
                  vec, vec, _resident((tm, tm), lambda b, j: (0, 0)), _resident((tm, tm), lambda b, j: (0, 0))],
        out_specs=out_specs,
        out_shape=out_shape,
        compiler_params=_cparams(("arbitrary", "arbitrary"), V7X_VMEM_LIMIT),
        name="inproj",
    )(x_lat, x_ctx, mod3, g.reshape(1, d), w_bf, cos_t, sin_t, lb_f.reshape(1, BRANCH_W), lb_b.reshape(1, BRANCH_W), tri_f, tri_b)
    return {name: o for (name, _, _), o in zip(_IN_OUTS, outs)}


def _hgrn_direction(q_ref, b_ref, k_ref, v_ref, o_ref, st_ref, reverse, rows, cross_group_only):
    c = A_CHUNK
    row = lax.broadcasted_iota(jnp.int32, (c, c), 0)
    col = lax.broadcasted_iota(jnp.int32, (c, c), 1)
    allowed = (col >= row) if reverse else (col <= row)
    if cross_group_only:
        allowed = allowed & ((row // A_SUB) != (col // A_SUB))

    b = b_ref[0, rows, :]
    tot_row = 0 if reverse else c - 1
    tot = b[tot_row:tot_row + 1, :]

    q = q_ref[0, rows, :].astype(F32)
    kin = k_ref[0, rows, :].astype(F32)
    v = v_ref[0, rows, :]
    zero_blk = jnp.zeros((A_SUB, HEAD_W), F32)
    heads = []
    for h in range(A_HEADS):
        hs = slice(h * HEAD_W, (h + 1) * HEAD_W)
        bh, qh, kh = b[:, hs], q[:, hs], kin[:, hs]
        q_state = qh * jnp.exp(bh)
        k_state = kh * jnp.exp(tot[:, hs] - bh)
        q_rows, k_cols = [], []
        for i in range(c // A_SUB):
            r0 = i * A_SUB
            ref = r0 + A_SUB - 1 if reverse else r0
            r = bh[ref:ref + 1, :]
            qt = qh[r0:r0 + A_SUB] * jnp.exp(bh[r0:r0 + A_SUB] - r)
            q_rows.append(jnp.concatenate([qt if j == i else zero_blk for j in range(c // A_SUB)], axis=1))
            live = slice(r0, c) if reverse else slice(0, r0 + A_SUB)
            kt = kh[live] * jnp.exp(jnp.minimum(r - bh[live], EXP_CLAMP))
            dead = jnp.zeros((c - kt.shape[0], HEAD_W), F32)
            k_cols.append(jnp.concatenate([dead, kt] if reverse else [kt, dead], axis=0) if dead.shape[0] else kt)
        heads.append(dict(
            hs=hs, h=h, rows=rows, allowed=allowed, vh=v[:, hs], decay=jnp.exp(tot[:, hs]), o_ref=o_ref, st_ref=st_ref,
            q_wide=jnp.concatenate(q_rows, axis=0).astype(BF16),
            k_wide=jnp.concatenate(k_cols, axis=1).astype(BF16),
            q_state=q_state.astype(BF16), k_state=k_state.astype(BF16)))
    return heads


def _hgrn_exact_groups(q_ref, b_ref, k_ref, v_ref, o_ref, reverse):
    sub = lax.broadcasted_iota(jnp.int32, (A_SUB, HEAD_W), 0)

    def chunk(c, carry):
        for g in range(A_CHUNK // A_SUB):
            rows = pl.ds(pl.multiple_of(c * A_CHUNK + g * A_SUB, A_SUB), A_SUB)
            for h in range(A_HEADS):
                hs = slice(h * HEAD_W, (h + 1) * HEAD_W)
                bg, qg = b_ref[0, rows, hs], q_ref[0, rows, hs].astype(F32)
                kg, vg = k_ref[0, rows, hs].astype(F32), v_ref[0, rows, hs].astype(F32)
                acc = jnp.zeros((A_SUB, HEAD_W), F32)
                for t in range(A_SUB):
                    live = (sub >= t) if reverse else (sub <= t)
                    w = jnp.exp(jnp.minimum(bg[t:t + 1] - bg, 0.0))
                    a = jnp.sum(jnp.where(live, qg[t:t + 1] * kg * w, 0.0), axis=-1, keepdims=True)
                    acc = jnp.where(sub == t, jnp.sum(a * vg, axis=0, keepdims=True), acc)
                o_ref[0, rows, hs] = (o_ref[0, rows, hs].astype(F32) + acc).astype(o_ref.dtype)
        return carry

    lax.fori_loop(0, q_ref.shape[1] // A_CHUNK, chunk, 0)


def _hgrn_kernel(qf_ref, bf_ref, kf_ref, vf_ref, qb_ref, bb_ref, kb_ref, vb_ref, of_ref, ob_ref, sf_ref, sb_ref):
    @pl.when(pl.program_id(1) == 0)
    def _():
        sf_ref[...] = jnp.zeros_like(sf_ref)
        sb_ref[...] = jnp.zeros_like(sb_ref)

    drops = None
    for b_ref in (bf_ref, bb_ref):
        for r0 in range(0, qf_ref.shape[1], A_SUB):
            d = jnp.abs(b_ref[0, r0:r0 + 1, :] - b_ref[0, r0 + A_SUB - 1:r0 + A_SUB, :])
            drops = d if drops is None else jnp.maximum(drops, d)
    within_range = jnp.max(drops) <= EXP_CLAMP

    @pl.when(within_range)
    def _():
        _hgrn_block(qf_ref, bf_ref, kf_ref, vf_ref, qb_ref, bb_ref, kb_ref, vb_ref, of_ref, ob_ref, sf_ref, sb_ref,
                    cross_group_only=False)

    @pl.when(jnp.logical_not(within_range))
    def _():
        _hgrn_block(qf_ref, bf_ref, kf_ref, vf_ref, qb_ref, bb_ref, kb_ref, vb_ref, of_ref, ob_ref, sf_ref, sb_ref,
                    cross_group_only=True)
        _hgrn_exact_groups(qf_ref, bf_ref, kf_ref, vf_ref, of_ref, reverse=False)
        _hgrn_exact_groups(qb_ref, bb_ref, kb_ref, vb_ref, ob_ref, reverse=True)


def _hgrn_block(qf_ref, bf_ref, kf_ref, vf_ref, qb_ref, bb_ref, kb_ref, vb_ref, of_ref, ob_ref, sf_ref, sb_ref,
                cross_group_only):
    n_sub = qf_ref.shape[1] // A_CHUNK
    seq = []
    for i in range(n_sub):
        up = slice(i * A_CHUNK, (i + 1) * A_CHUNK)
        down = slice((n_sub - 1 - i) * A_CHUNK, (n_sub - i) * A_CHUNK)
        seq.append(_hgrn_direction(qf_ref, bf_ref, kf_ref, vf_ref, of_ref, sf_ref, False, up, cross_group_only)
                   + _hgrn_direction(qb_ref, bb_ref, kb_ref, vb_ref, ob_ref, sb_ref, True, down, cross_group_only))
    for chains in seq:
        for ch in chains:
            ch["att"] = _dot_nt(ch["q_wide"], ch["k_wide"])
    state = [ch["st_ref"][ch["h"]] for ch in seq[0]]
    for chains in seq:
        for ch, st in zip(chains, state):
            ch["inter"] = _dot_nt(ch["q_state"], st.astype(BF16))
            ch["upd"] = _dot_tn(ch["vh"], ch["k_state"])
        for n, ch in enumerate(chains):
            att = jnp.where(ch["allowed"], ch["att"], 0.0).astype(BF16)
            o = _dot(att, ch["vh"]) + ch["inter"]
            ch["o_ref"][0, ch["rows"], ch["hs"]] = o.astype(ch["o_ref"].dtype)
            state[n] = state[n] * ch["decay"] + ch["upd"]
    for ch, st in zip(seq[0], state):
        ch["st_ref"][ch["h"]] = st


def _hgrn(qa, bf, kf, bb, kb, ia, n_lat):
    bsz, t, w = qa.shape
    c = HGRN_ROWS
    n_chunks = t // c
    lat_chunks = n_lat // c

    def fwd(b, s):
        return (b, (s + lat_chunks) % n_chunks, 0)

    def bwd(b, s):
        return (b, n_chunks - 1 - s, 0)

    blk = (1, c, w)
    return pl.pallas_call(
        _hgrn_kernel,
        grid=(bsz, n_chunks),
        in_specs=[pl.BlockSpec(blk, fwd)] * 4 + [pl.BlockSpec(blk, bwd)] * 4,
        out_specs=[pl.BlockSpec(blk, fwd), pl.BlockSpec(blk, bwd)],
        out_shape=[jax.ShapeDtypeStruct((bsz, t, w), BF16)] * 2,
        scratch_shapes=[pltpu.VMEM((A_HEADS, HEAD_W, HEAD_W), F32)] * 2,
        compiler_params=_cparams(("arbitrary", "arbitrary")),
        name="hgrn2",
    )(qa, bf, kf, ia, qa, bb, kb, ia)


def _attn_kernel(q_ref, k_ref, v_ref, lam_ref, sub_ref, *rest, key_blocks, lam_init, n_cast):
    cast_in, (o_ref, *cast_out), (v1_ref, m_ref, acc_ref) = rest[:n_cast], rest[n_cast:2 * n_cast + 1], rest[-3:]
    for src, dst in zip(cast_in, cast_out):
        dst[...] = src[...].astype(BF16)
    tq = q_ref.shape[1]

    @pl.when(pl.program_id(2) == 0)
    def _():
        v1_ref[:, :HEAD_W] = v_ref[0]
        v1_ref[:, HEAD_W:] = jnp.ones((v1_ref.shape[0], HEAD_W), BF16)

    lane = lax.broadcasted_iota(jnp.int32, (tq, HEAD_W), 1)
    qf = q_ref[0].astype(F32) * (C_DH ** -0.5 * LOG2_E)
    q2 = jnp.concatenate([jnp.where(lane < C_DH, qf, 0.0), jnp.where(lane >= C_DH, qf, 0.0)], axis=0).astype(BF16)

    m_ref[...] = jnp.full_like(m_ref, NEG_BIG)
    acc_ref[...] = jnp.zeros_like(acc_ref)

    def scores(start, size):
        return _dot_nt(q2, k_ref[0, pl.ds(start, size), :])

    def absorb(s, start, size):
        m_old = m_ref[...]
        m_new = jnp.maximum(m_old, jnp.max(s, axis=-1, keepdims=True))
        alpha = jnp.exp2(m_old - m_new)
        p = jnp.exp2(s - jnp.concatenate([m_new] * (size // HEAD_W), axis=1))
        acc_ref[...] = (jnp.concatenate([alpha, alpha], axis=1) * acc_ref[...]
                        + _dot(p.astype(BF16), v1_ref[pl.ds(start, size), :]))
        m_ref[...] = m_new

    cur = scores(*key_blocks[0])
    for n, blk in enumerate(key_blocks):
        nxt = scores(*key_blocks[n + 1]) if n + 1 < len(key_blocks) else None
        absorb(cur, *blk)
        cur = nxt

    o12 = acc_ref[:, :HEAD_W] / acc_ref[:, HEAD_W:]
    o = o12[:tq] - lam_ref[...] * o12[tq:]
    o_ref[0] = (_rms(o) * sub_ref[...] * (1.0 - lam_init)).astype(o_ref.dtype)


def _attention(qc, kc, vc, lam, subln, lam_init, tq, q_block0, n_q, kv_block, key_blocks, to_bf16=()):
    bsz, _, w = qc.shape
    kv_rows, kv_idx = kv_block
    n_steps = bsz * C_HEADS * n_q
    cast_specs = []
    for a in to_bf16:
        rows = a.shape[0] // n_steps
        assert a.shape[0] == rows * n_steps and rows % BF16_ROWS == 0
        cast_specs.append(pl.BlockSpec((rows, a.shape[1]), lambda b, h, i: ((b * C_HEADS + h) * n_q + i, 0)))
    outs = pl.pallas_call(
        functools.partial(_attn_kernel, key_blocks=key_blocks, lam_init=lam_init, n_cast=len(to_bf16)),
        grid=(bsz, C_HEADS, n_q),
        in_specs=[pl.BlockSpec((1, tq, HEAD_W), lambda b, h, i: (b, q_block0 + i, h)),
                  pl.BlockSpec((1, kv_rows, HEAD_W), lambda b, h, i: (b, kv_idx, h)),
                  pl.BlockSpec((1, kv_rows, HEAD_W), lambda b, h, i: (b, kv_idx, h)),
                  pl.BlockSpec((1, HEAD_W), lambda b, h, i: (0, 0)),
                  pl.BlockSpec((1, HEAD_W), lambda b, h, i: (0, h))] + cast_specs,
        out_specs=[pl.BlockSpec((1, tq, HEAD_W), lambda b, h, i: (b, i, h))] + cast_specs,
        out_shape=[jax.ShapeDtypeStruct((bsz, n_q * tq, w), BF16)]
        + [jax.ShapeDtypeStruct(a.shape, BF16) for a in to_bf16],
        scratch_shapes=[pltpu.VMEM((kv_rows, 2 * HEAD_W), BF16), pltpu.VMEM((2 * tq, HEAD_W), F32),
                        pltpu.VMEM((2 * tq, 2 * HEAD_W), F32)],
        compiler_params=_cparams(("arbitrary", "arbitrary", "arbitrary")),
        name="diff_attention",
    )(qc, kc, vc, jnp.full((1, HEAD_W), lam, F32), subln.reshape(1, w), *to_bf16)
    return outs[0], outs[1:]


def _merge_kernel(xl_ref, xc_ref, mod_ref, of_ref, ob_ref, og_ref, gn_ref, uv_ref, vn_ref, ws_ref, bs_ref, ycl_ref,
                  ycc_ref, gt_ref, wb_ref, wo_ref, g2_ref, wr_ref, x_out_ref, h_out_ref, *maybe_logits_ref, d, n_lat_blocks):
    tm = xl_ref.shape[1]
    is_lat = pl.program_id(1) < n_lat_blocks
    mod = mod_ref[0]
    groups = [slice(r0, r0 + B_CHUNK) for r0 in range(0, tm, B_CHUNK)]
    st = [dict() for _ in groups]

    for s, rows in zip(st, groups):
        o = of_ref[0, rows, :].astype(F32) + ob_ref[0, rows, :].astype(F32)
        o = jnp.concatenate([_rms(o[:, h * HEAD_W:(h + 1) * HEAD_W]) for h in range(A_HEADS)], axis=1)
        s["ya"] = (o * gn_ref[...] * og_ref[0, rows, :].astype(F32)).astype(BF16)
        uv = uv_ref[0, rows, :].astype(F32)
        s["u"] = uv[:, :BRANCH_W]
        s["vv"] = (_rms(uv[:, BRANCH_W:]) * vn_ref[...]).astype(BF16)
    for s in st:
        s["mixed"] = jnp.concatenate(
            [_dot(ws_ref[g], s["vv"][:, g * HEAD_W:(g + 1) * HEAD_W]) for g in range(B_GROUPS)], axis=1)
    for s, rows in zip(st, groups):
        yb = (s["u"] * (s["mixed"] + bs_ref[...])).astype(BF16)
        yc = jnp.where(is_lat, ycl_ref[0, rows, :], ycc_ref[0, rows, :])
        s["proj"] = [_dot(y, wb_ref[i]) for i, y in enumerate((s["ya"], yb, yc))]
    for s, rows in zip(st, groups):
        merged = None
        for i in range(N_BRANCH):
            term = gt_ref[0, rows, i * d:(i + 1) * d].astype(F32) * s["proj"][i]
            merged = term if merged is None else merged + term
        s["mix"] = _dot(merged.astype(BF16), wo_ref[...])
    for s, rows in zip(st, groups):
        x_old = jnp.where(is_lat, xl_ref[0, rows, :], xc_ref[0, rows, :])
        x_new = x_old + mod[:, 2 * d:3 * d] * s["mix"]
        x_out_ref[0, rows, :] = x_new
        h = _rms(x_new) * g2_ref[...]
        h = h * (1.0 + mod[:, 4 * d:5 * d]) + mod[:, 3 * d:4 * d]
        h_hi = h.astype(BF16)
        h_out_ref[0, rows, :] = h_hi
        if maybe_logits_ref:
            h_lo = (h - h_hi.astype(F32)).astype(BF16)
            lg = (_dot(h_hi, wr_ref[0]) + _dot(h_lo, wr_ref[0])) + _dot(h_hi, wr_ref[1])
            maybe_logits_ref[0][0, rows, :] = lg
            maybe_logits_ref[1][:, rows] = lg.T


def _merge(x_lat, x_ctx, ctx_block0, mod3, proj, of, ob, yc_lat, yc_ctx, gnorm, vnorm, ws_bf, bs_full, wb_bf, wo_bf, g2, wr_split,
           n_lat_blocks, n_blocks, with_logits):
    bsz, _, d = x_lat.shape
    tm = ROW_TILE
    ctx_row = bsz

    def rows(w):
        return pl.BlockSpec((1, tm, w), lambda b, j: (b, j, 0))

    def const2(shape):
        return pl.BlockSpec(shape, lambda b, j: (0, 0))

    def mod_map(b, j):
        return (jnp.where(j < n_lat_blocks, b, ctx_row), 0, 0)

    n_rows = n_blocks * tm
    out_shape = [jax.ShapeDtypeStruct((bsz, n_rows, d), F32), jax.ShapeDtypeStruct((bsz, n_rows, d), BF16)]
    out_specs = [rows(d), rows(d)]
    if with_logits:
        out_shape += [jax.ShapeDtypeStruct((bsz, n_rows, HEAD_W), F32),
                      jax.ShapeDtypeStruct((HEAD_W, bsz * n_rows), F32)]
        out_specs += [rows(HEAD_W), pl.BlockSpec((HEAD_W, tm), lambda b, j: (0, b * n_blocks + j))]
    lat_spec, ctx_spec = _lat_ctx_specs(tm, d, n_lat_blocks, ctx_block0)
    return pl.pallas_call(
        functools.partial(_merge_kernel, d=d, n_lat_blocks=n_lat_blocks),
        grid=(bsz, n_blocks),
        in_specs=[lat_spec, ctx_spec, pl.BlockSpec((1, 1, mod3.shape[2]), mod_map),
                  rows(BRANCH_W), rows(BRANCH_W), rows(BRANCH_W), const2((1, BRANCH_W)),
                  rows(2 * BRANCH_W), const2((1, BRANCH_W)),
                  _resident((B_GROUPS, B_CHUNK, B_CHUNK), lambda b, j: (0, 0, 0)), const2((B_CHUNK, BRANCH_W)),
                  *_lat_ctx_specs(tm, BRANCH_W, n_lat_blocks, 0), rows(N_BRANCH * d),
                  _resident((N_BRANCH, BRANCH_W, d), lambda b, j: (0, 0, 0)), _resident((d, d), lambda b, j: (0, 0)),
                  const2((1, d)), _resident((2, d, HEAD_W), lambda b, j: (0, 0, 0))],
        out_specs=out_specs,
        out_shape=out_shape,
        compiler_params=_cparams(("arbitrary", "arbitrary"), V7X_VMEM_LIMIT),
        name="merge",
    )(x_lat, x_ctx, mod3, of, ob, proj["og"], gnorm.reshape(1, BRANCH_W), proj["uv"], vnorm.reshape(1, BRANCH_W),
      ws_bf, bs_full, yc_lat, yc_ctx, proj["gt"], wb_bf, wo_bf, g2.reshape(1, d), wr_split)


def _ffn_kernel(x_ref, h_ref, mod_ref, wg_ref, wu_ref, wd_ref, o_ref, *, d):
    h = h_ref[0]
    g = _dot(h, wg_ref[...])
    a = (g * _sigmoid(g) * _dot(h, wu_ref[...])).astype(BF16)
    y = _dot(a, wd_ref[...])
    o_ref[0] = x_ref[0] + mod_ref[0][:, 5 * d:6 * d] * y


def _ffn(xc, h2, mod3, wg_bf, wu_bf, wd_bf, n_lat_blocks):
    bsz, t, d = xc.shape
    tm = ROW_TILE
    dff = wg_bf.shape[1]
    ctx_row = bsz

    def mod_map(b, j):
        return (jnp.where(j < n_lat_blocks, b, ctx_row), 0, 0)

    rows = pl.BlockSpec((1, tm, d), lambda b, j: (b, j, 0))
    return pl.pallas_call(
        functools.partial(_ffn_kernel, d=d),
        grid=(bsz, t // tm),
        in_specs=[rows, rows, pl.BlockSpec((1, 1, mod3.shape[2]), mod_map),
                  _resident((d, dff), lambda b, j: (0, 0)), _resident((d, dff), lambda b, j: (0, 0)),
                  _resident((dff, d), lambda b, j: (0, 0))],
        out_specs=rows,
        out_shape=jax.ShapeDtypeStruct((bsz, t, d), F32),
        compiler_params=_cparams(("arbitrary", "arbitrary"), V7X_VMEM_LIMIT),
        name="dense_ffn",
    )(xc, h2, mod3, wg_bf, wu_bf, wd_bf)


def _moe_kernel(tile_expert_ref, n_tiles_ref, x_ref, wg_ref, wu_ref, wd_ref, *rest, tile0):
    del tile_expert_ref
    o_ref = rest[-1]
    in_use = pl.program_id(0) + tile0 < n_tiles_ref[0]

    @pl.when(in_use)
    def _():
        x = x_ref[...]
        dff = wg_ref.shape[2]
        acc = None
        for c0 in range(0, dff, FF_CHUNK):
            g = _dot(x, wg_ref[0, :, c0:c0 + FF_CHUNK])
            a = (g * _sigmoid(g) * _dot(x, wu_ref[0, :, c0:c0 + FF_CHUNK])).astype(BF16)
            y = _dot(a, wd_ref[0, c0:c0 + FF_CHUNK, :])
            acc = y if acc is None else acc + y
        o_ref[...] = acc.astype(o_ref.dtype)

    @pl.when(jnp.logical_not(in_use))
    def _():
        o_ref[...] = jnp.zeros_like(o_ref)


def _moe_experts(x_part, tile0, p, tile_expert, n_tiles, wg_bf, wu_bf, wd_bf, y_prev):
    d = x_part.shape[1]
    tm = MOE_TILE
    dff = wg_bf.shape[2]
    in_specs = [pl.BlockSpec((tm, d), lambda t, te, nt: (t, 0)),
                pl.BlockSpec((1, d, dff), lambda t, te, nt: (te[tile0 + t], 0, 0)),
                pl.BlockSpec((1, d, dff), lambda t, te, nt: (te[tile0 + t], 0, 0)),
                pl.BlockSpec((1, dff, d), lambda t, te, nt: (te[tile0 + t], 0, 0))]
    operands = [tile_expert, n_tiles, x_part, wg_bf, wu_bf, wd_bf]
    aliases = {}
    if y_prev is not None:
        in_specs.append(pl.BlockSpec(memory_space=pl.ANY))
        aliases = {len(operands): 0}
        operands.append(y_prev)
    grid_spec = pltpu.PrefetchScalarGridSpec(
        num_scalar_prefetch=2,
        grid=(x_part.shape[0] // tm,),
        in_specs=in_specs,
        out_specs=pl.BlockSpec((tm, d), lambda t, te, nt: (tile0 + t, 0)),
    )
    return pl.pallas_call(
        functools.partial(_moe_kernel, tile0=tile0),
        grid_spec=grid_spec,
        out_shape=jax.ShapeDtypeStruct((p, d), BF16),
        input_output_aliases=aliases,
        compiler_params=_cparams(("arbitrary",), V7X_VMEM_LIMIT),
        name="moe_experts",
    )(*operands)


def _route(logits_t, tm):
    n = logits_t.shape[1]
    experts = jnp.arange(N_EXPERTS, dtype=jnp.int32)[:, None]
    e1 = jnp.argmax(logits_t, axis=0).astype(jnp.int32)
    e2 = jnp.argmax(jnp.where(experts == e1[None, :], -jnp.inf, logits_t), axis=0).astype(jnp.int32)
    oh1 = (experts == e1[None, :]).astype(jnp.int32)
    oh2 = (experts == e2[None, :]).astype(jnp.int32)
    c1 = jnp.cumsum(oh1, axis=1)
    c2 = jnp.cumsum(oh2, axis=1) + c1[:, -1:]
    counts = c2[:, -1]
    padded = ((counts + tm - 1) // tm) * tm
    ends = jnp.cumsum(padded)
    starts = ends - padded
    dest1 = jnp.sum(oh1 * (starts[:, None] + c1 - 1), axis=0)
    dest2 = jnp.sum(oh2 * (starts[:, None] + c2 - 1), axis=0)
    m = TOP_K * n
    p = m + N_EXPERTS * tm
    tile_start = jnp.arange(p // tm, dtype=jnp.int32) * tm
    tile_expert = jnp.minimum(jnp.sum((tile_start[:, None] >= ends[None, :]).astype(jnp.int32), axis=1),
                              N_EXPERTS - 1).astype(jnp.int32)
    n_tiles = (ends[-1] // tm).astype(jnp.int32).reshape(1)
    token = jnp.arange(n, dtype=jnp.int32)
    by_expert = jnp.sort(jnp.concatenate([e1 * m + token, e2 * m + n + token])) % m
    first = jnp.cumsum(counts) - counts
    row_expert = jnp.repeat(tile_expert, tm)
    pos = jnp.arange(p, dtype=jnp.int32)
    entry = jnp.take(by_expert, (first[row_expert] + pos - starts[row_expert]) % m)
    src_token = jnp.where(entry < n, entry, entry - n)
    return dest1, dest2, src_token, tile_expert, n_tiles


def _final_kernel(x_ref, y0_ref, y1_ref, lg_ref, mod_ref, g_ref, *rest, d):
    o_ref = rest[-1]
    lane = lax.broadcasted_iota(jnp.int32, lg_ref.shape[1:], 1)
    lg = jnp.where(lane < N_EXPERTS, lg_ref[0], NEG_BIG)
    m1 = jnp.max(lg, axis=-1, keepdims=True)
    first = jnp.min(jnp.where(lg == m1, lane, HEAD_W), axis=-1, keepdims=True)
    m2 = jnp.max(jnp.where(lane == first, NEG_BIG, lg), axis=-1, keepdims=True)
    e = jnp.exp(m2 - m1)
    w1 = 1.0 / (1.0 + e)
    y = w1 * y0_ref[0].astype(F32) + (e * w1) * y1_ref[0].astype(F32)
    x = x_ref[0] + mod_ref[0][:, 5 * d:6 * d] * y
    o_ref[0] = _rms(x) * g_ref[...]


def _final(x_lat, y0, y1, logits, mod3, g_final, b0, out_prev):
    bsz, s, d = x_lat.shape
    tm = ROW_TILE
    here = pl.BlockSpec((1, tm, d), lambda b, j: (b0 + b, j, 0))
    part = pl.BlockSpec((1, tm, d), lambda b, j: (b, j, 0))
    in_specs = [here, part, part, pl.BlockSpec((1, tm, HEAD_W), lambda b, j: (b0 + b, j, 0)),
                pl.BlockSpec((1, 1, mod3.shape[2]), lambda b, j: (b0 + b, 0, 0)),
                pl.BlockSpec((1, d), lambda b, j: (0, 0))]
    operands = [x_lat, y0, y1, logits, mod3, g_final.reshape(1, d)]
    aliases = {}
    if out_prev is not None:
        in_specs.append(pl.BlockSpec(memory_space=pl.ANY))
        aliases = {len(operands): 0}
        operands.append(out_prev)
    return pl.pallas_call(
        functools.partial(_final_kernel, d=d),
        grid=(y0.shape[0], s // tm),
        in_specs=in_specs,
        out_specs=here,
        out_shape=jax.ShapeDtypeStruct((bsz, s, d), F32),
        input_output_aliases=aliases,
        compiler_params=_cparams(("arbitrary", "arbitrary")),
        name="moe_combine_final_norm",
    )(*operands)


def _rope_tables(n_lat, n_ctx):
    pairs = ROPE_PAIRS
    tpos = jnp.arange(n_lat)
    pos = jnp.stack([(tpos // GRID_W).astype(F32), (tpos % GRID_W).astype(F32)], axis=1)
    freqs = ROPE_BASE ** (-jnp.arange(pairs, dtype=F32) / pairs)
    ang = pos[:, :, None] * freqs[None, None, :]
    cos = jnp.repeat(jnp.cos(ang)[:, :, None, :], 2, axis=2).reshape(n_lat, C_DH)
    sin = jnp.sin(ang)[:, :, None, :] * jnp.array([-1.0, 1.0], F32)[None, None, :, None]
    sin = sin.reshape(n_lat, C_DH)
    cos = jnp.concatenate([cos, jnp.ones((n_ctx, C_DH), F32)], axis=0)
    sin = jnp.concatenate([sin, jnp.zeros((n_ctx, C_DH), F32)], axis=0)
    reps = BRANCH_W // C_DH
    return jnp.tile(cos, (1, reps)), jnp.tile(sin, (1, reps))


def _lower_bound(p, layer):
    cs = jnp.cumsum(jax.nn.softmax(p.astype(F32), axis=0), axis=0)
    return cs[layer] - cs[0]


def kernel(x, c, ctx, c_ctx, w_ada, b_ada, g_norm1, g_norm2, w_in, hgrn_lb, hgrn_gnorm, mlp_vnorm, mlp_ws, mlp_bs,
           diff_lambda, diff_subln, w_branch, w_out, ffn_wg, ffn_wu, ffn_wd, moe_router, moe_wg, moe_wu, moe_wd,
           g_final):
    bsz, n_lat, d = x.shape
    n_ctx = ctx.shape[1]
    depth = w_ada.shape[0]
    t = n_lat + n_ctx
    tm = ROW_TILE
    assert depth == 2 and d == D_MODEL and bsz < MOD_ROWS and n_lat % GRID_W == 0
    assert n_lat % ATT_TQ == 0 and n_lat % ATT_TK == 0 and n_lat % n_ctx == 0 and n_ctx % tm == 0
    n_lat_blocks = n_lat // tm

    cvec = jnp.zeros((MOD_ROWS, d), F32).at[:bsz].set(c).at[bsz].set(c_ctx)
    mods = _adaln(cvec, w_ada, b_ada)
    cos_t, sin_t = _rope_tables(n_lat, n_ctx)
    x_lat, x_ctx, ctx_block0 = x, ctx, 0
    w_in_bf = w_in.astype(BF16)

    out = None
    for layer in range(depth):
        last = layer == depth - 1
        mod3 = mods[layer].reshape(MOD_ROWS, 1, 6 * d)
        lb_f = _lower_bound(hgrn_lb[0], layer)
        lb_b = _lower_bound(hgrn_lb[1], layer)
        proj = _inproj(x_lat, x_ctx, ctx_block0, t, mod3, g_norm1[layer], w_in_bf, layer, cos_t, sin_t,
                       lb_f, lb_b, n_lat_blocks)
        of, ob = _hgrn(proj["qa"], proj["bf"], proj["kf"], proj["bb"], proj["kb"], proj["ia"], n_lat)

        lam_init = 0.8 - 0.6 * math.exp(-0.3 * layer)
        lam_p = diff_lambda[layer]
        lam = jnp.exp(jnp.sum(lam_p[0] * lam_p[1])) - jnp.exp(jnp.sum(lam_p[2] * lam_p[3])) + lam_init
        lat_keys = tuple((i * ATT_TK, ATT_TK) for i in range(n_lat // ATT_TK)) + ((n_lat, n_ctx),)
        qkv = (proj["qc"], proj["kc"], proj["vc"], lam.astype(F32), diff_subln[layer], lam_init)
        j = layer // 2
        is_moe = layer % 2 == 1
        expert_w = ()
        if is_moe:
            n_e, _, dff = moe_wg[j].shape
            expert_w = (moe_wg[j].reshape(n_e * d, dff), moe_wu[j].reshape(n_e * d, dff),
                        moe_wd[j].reshape(n_e * dff, d))
        yc_lat, expert_w_bf = _attention(*qkv, ATT_TQ, 0, n_lat // ATT_TQ, (t, 0), lat_keys, expert_w)
        yc_ctx = yc_lat if last else _attention(*qkv, n_ctx, n_lat // n_ctx, 1, (n_ctx, n_lat // n_ctx),
                                                ((0, n_ctx),))[0]

        bs_full = jnp.repeat(mlp_bs[layer].T, HEAD_W, axis=1)
        wr_split = jnp.zeros((2, d, HEAD_W), BF16)
        if is_moe:
            wr_hi = moe_router[j].astype(BF16)
            wr_lo = (moe_router[j] - wr_hi.astype(F32)).astype(BF16)
            wr_split = wr_split.at[:, :, :N_EXPERTS].set(jnp.stack([wr_hi, wr_lo]))
        n_blocks = n_lat_blocks if last else t // tm
        merged = _merge(x_lat, x_ctx, ctx_block0, mod3, proj, of, ob, yc_lat, yc_ctx, hgrn_gnorm[layer], mlp_vnorm[layer],
                        mlp_ws[layer].astype(BF16), bs_full, w_branch[layer].astype(BF16), w_out[layer].astype(BF16),
                        g_norm2[layer], wr_split, n_lat_blocks, n_blocks, is_moe)
        x_new, h2 = merged[0], merged[1]

        if not is_moe:
            assert not last
            xc = _ffn(x_new, h2, mod3, ffn_wg[j].astype(BF16), ffn_wu[j].astype(BF16), ffn_wd[j].astype(BF16),
                      n_lat_blocks)
            x_lat, x_ctx, ctx_block0 = xc, xc, n_lat_blocks
        else:
            assert last
            n = bsz * n_lat
            logits, logits_t = merged[2], merged[3]
            dest1, dest2, src_token, tile_expert, n_tiles = _route(logits_t[:N_EXPERTS], MOE_TILE)
            p = src_token.shape[0]
            part_rows = p // MOE_PARTS
            assert part_rows % MOE_TILE == 0 and part_rows * MOE_PARTS == p
            expert_w = (expert_w_bf[0].reshape(n_e, d, dff), expert_w_bf[1].reshape(n_e, d, dff),
                        expert_w_bf[2].reshape(n_e, dff, d))
            y = None
            for i in range(MOE_PARTS):
                x_part = jnp.take(h2.reshape(n, d), src_token[i * part_rows:(i + 1) * part_rows], axis=0, mode="clip")
                y = _moe_experts(x_part, i * (part_rows // MOE_TILE), p, tile_expert, n_tiles, *expert_w, y)
            assert bsz % FINAL_PARTS == 0
            part_b = bsz // FINAL_PARTS
            for i in range(FINAL_PARTS):
                tokens = slice(i * part_b * n_lat, (i + 1) * part_b * n_lat)
                y0 = jnp.take(y, dest1[tokens], axis=0, mode="clip").reshape(part_b, n_lat, d)
                y1 = jnp.take(y, dest2[tokens], axis=0, mode="clip").reshape(part_b, n_lat, d)
                out = _final(x_new, y0, y1, logits, mod3, g_final, i * part_b, out)
    return out
```

```python
import functools
import math

import jax
import jax.numpy as jnp
from jax import lax
from jax.experimental import pallas as pl
from jax.experimental.pallas import tpu as pltpu

F32 = jnp.float32
BF16 = jnp.bfloat16

EPS = 1e-6
GRID_W = 64
ROPE_BASE = 10000.0

A_HEADS = 4
A_CHUNK = 64
A_SUB = 16
HGRN_ROWS = 256
MOE_TILE = 512
MOE_PARTS = 4
FINAL_PARTS = 2
B_GROUPS = 4
B_CHUNK = 128
C_HEADS = 4
C_DH = 64
ROPE_PAIRS = C_DH // 4
HEAD_W = 128
BRANCH_W = 512
D_MODEL = 1024
N_BRANCH = 3
N_EXPERTS = 8
TOP_K = 2

BF16_ROWS = 16
ROW_TILE = 256
MOD_ROWS = 16
ATT_TQ = 512
ATT_TK = 512
ADALN_TN = 1536
FF_CHUNK = 512
EXP_CLAMP = 80.0
LOG2_E = 1.4426950408889634
NEG_BIG = -1e30
V7X_VMEM_LIMIT = 56 * 1024 * 1024


def _cparams(sem, vmem=None):
    return pltpu.CompilerParams(dimension_semantics=sem, vmem_limit_bytes=vmem)


def _resident(shape, index_map):
    return pl.BlockSpec(shape, index_map, pipeline_mode=pl.Buffered(1))


def _lat_ctx_specs(tm, d, n_lat_blocks, ctx_block0):
    lat = pl.BlockSpec((1, tm, d), lambda b, j: (b, jnp.minimum(j, n_lat_blocks - 1), 0))
    ctx = pl.BlockSpec((1, tm, d), lambda b, j: (b, ctx_block0 + jnp.maximum(j - n_lat_blocks, 0), 0))
    return lat, ctx


def _rms(xf):
    return xf * lax.rsqrt(jnp.mean(xf * xf, axis=-1, keepdims=True) + EPS)


def _sigmoid(x):
    return 1.0 / (1.0 + jnp.exp(-x))


def _dot(a, b):
    return jnp.dot(a, b, preferred_element_type=F32)


def _dot_nt(a, b):
    return lax.dot_general(a, b, (((1,), (1,)), ((), ())), preferred_element_type=F32)


def _dot_tn(a, b):
    return lax.dot_general(a, b, (((0,), (0,)), ((), ())), preferred_element_type=F32)


def _adaln_kernel(c_ref, w_ref, b_ref, o_ref):
    c = c_ref[...]
    a = c * _sigmoid(c)
    o_ref[0] = jnp.dot(a, w_ref[0], preferred_element_type=F32, precision=lax.Precision.HIGHEST) + b_ref[0]


def _adaln(cvec, w_ada, b_ada):
    depth, d, n = w_ada.shape
    tn = ADALN_TN
    return pl.pallas_call(
        _adaln_kernel,
        grid=(depth, n // tn),
        in_specs=[pl.BlockSpec((MOD_ROWS, d), lambda l, j: (0, 0)),
                  pl.BlockSpec((1, d, tn), lambda l, j: (l, 0, j)),
                  pl.BlockSpec((1, 1, tn), lambda l, j: (l, 0, j))],
        out_specs=pl.BlockSpec((1, MOD_ROWS, tn), lambda l, j: (l, 0, j)),
        out_shape=jax.ShapeDtypeStruct((depth, MOD_ROWS, n), F32),
        compiler_params=_cparams(("arbitrary", "arbitrary")),
        name="adaln",
    )(cvec, w_ada, b_ada.reshape(depth, 1, n))


_W = BRANCH_W
_IN_COLS = (("qa", _W), ("zf", _W), ("zb", _W), ("ia", _W), ("og", _W), ("uv", 2 * _W), ("qc", _W), ("kc", _W),
            ("vc", _W), ("gt", N_BRANCH * D_MODEL))
_IN_OUTS = (("qa", _W, BF16), ("bf", _W, F32), ("kf", _W, BF16), ("bb", _W, F32), ("kb", _W, BF16),
            ("ia", _W, BF16), ("og", _W, BF16), ("uv", 2 * _W, BF16), ("qc", _W, BF16), ("kc", _W, BF16),
            ("vc", _W, BF16), ("gt", N_BRANCH * D_MODEL, BF16))


def _gelu_tanh(x):
    return 0.5 * x * (1.0 + jnp.tanh(math.sqrt(2.0 / math.pi) * (x + 0.044715 * (x * x * x))))


_IN_ACTIVATIONS = {"og": lambda p: p * _sigmoid(p), "uv": _gelu_tanh, "gt": _sigmoid}


def _forget_gate(z, lb):
    sp = jnp.maximum(-z, 0.0) + jnp.log(1.0 + jnp.exp(-jnp.abs(z)))
    la = jnp.log(lb)
    lc = jnp.log1p(-lb) - sp
    logf = jnp.maximum(la, lc) + jnp.log(1.0 + jnp.exp(-jnp.abs(la - lc)))
    kin = (1.0 - lb) * jnp.exp(-z - sp)
    hi = logf.astype(BF16)
    rest = logf - hi.astype(F32)
    mid = rest.astype(BF16)
    low = (rest - mid.astype(F32)).astype(BF16)
    return (hi, mid, low), kin


def _inproj_kernel(xl_ref, xc_ref, mod_ref, g_ref, w_ref, cos_ref, sin_ref, lbf_ref, lbb_ref, trif_ref, trib_ref,
                   *out_refs, d, n_lat_blocks):
    x = jnp.where(pl.program_id(1) < n_lat_blocks, xl_ref[0], xc_ref[0])
    mod = mod_ref[0]
    h = _rms(x) * g_ref[...]
    h = (h * (1.0 + mod[:, d:2 * d]) + mod[:, 0:d]).astype(BF16)

    tm = x.shape[0]
    lane = lax.broadcasted_iota(jnp.int32, (tm, BRANCH_W), 1)
    low_half = (lane % (2 * ROPE_PAIRS)) < ROPE_PAIRS

    def rope(p):
        partner = jnp.where(low_half, pltpu.roll(p, BRANCH_W - ROPE_PAIRS, 1), pltpu.roll(p, ROPE_PAIRS, 1))
        return p * cos_ref[...] + partner * sin_ref[...]

    outs = {name: ref for (name, _, _), ref in zip(_IN_OUTS, out_refs)}
    col0 = {}
    lo = 0
    for name, width in _IN_COLS:
        col0[name] = lo
        lo += width

    gates = {}
    for name, lb_ref in (("zf", lbf_ref), ("zb", lbb_ref)):
        terms, kin = _forget_gate(_dot(h, w_ref[0, :, col0[name]:col0[name] + BRANCH_W]), lb_ref[...])
        outs["k" + name[1]][0] = kin.astype(BF16)
        gates[name] = terms
    for name, width in _IN_COLS:
        if name in gates:
            continue
        for c0 in range(0, width, BRANCH_W):
            p = _dot(h, w_ref[0, :, col0[name] + c0:col0[name] + c0 + BRANCH_W])
            if name in ("qc", "kc"):
                p = rope(p)
            elif name in _IN_ACTIVATIONS:
                p = _IN_ACTIVATIONS[name](p)
            outs[name][0, :, c0:c0 + BRANCH_W] = p.astype(outs[name].dtype)
    for name, tri_ref in (("zf", trif_ref), ("zb", trib_ref)):
        hi, mid, low = gates[name]
        outs["b" + name[1]][0] = _dot(tri_ref[...], hi) + _dot(tri_ref[...], mid) + _dot(tri_ref[...], low)


def _scan_matrices(tm):
    row = jnp.arange(tm)[:, None]
    col = jnp.arange(tm)[None, :]
    same = (row // A_CHUNK) == (col // A_CHUNK)
    return (same & (col <= row)).astype(BF16), (same & (col >= row)).astype(BF16)


def _inproj(x_lat, x_ctx, ctx_block0, t, mod3, g, w_bf, layer, cos_t, sin_t, lb_f, lb_b, n_lat_blocks):
    bsz, _, d = x_lat.shape
    tm = ROW_TILE
    ncols = w_bf.shape[2]
    ctx_row = bsz

    def mod_map(b, j):
        return (jnp.where(j < n_lat_blocks, b, ctx_row), 0, 0)

    out_shape = [jax.ShapeDtypeStruct((bsz, t, w), dt) for _, w, dt in _IN_OUTS]
    out_specs = [pl.BlockSpec((1, tm, w), lambda b, j: (b, j, 0)) for _, w, _ in _IN_OUTS]
    tri_f, tri_b = _scan_matrices(tm)
    vec = pl.BlockSpec((1, BRANCH_W), lambda b, j: (0, 0))
    lat_spec, ctx_spec = _lat_ctx_specs(tm, d, n_lat_blocks, ctx_block0)
    outs = pl.pallas_call(
        functools.partial(_inproj_kernel, d=d, n_lat_blocks=n_lat_blocks),
        grid=(bsz, t // tm),
        in_specs=[lat_spec, ctx_spec,
                  pl.BlockSpec((1, 1, mod3.shape[2]), mod_map),
                  pl.BlockSpec((1, d), lambda b, j: (0, 0)),
                  _resident((1, d, ncols), lambda b, j: (layer, 0, 0)),
                  pl.BlockSpec((tm, BRANCH_W), lambda b, j: (j, 0)),
                  pl.BlockSpec((tm, BRANCH_W), lambda b, j: (j, 0)),
                  vec, vec, _resident((tm, tm), lambda b, j: (0, 0)), _resident((tm, tm), lambda b, j: (0, 0))],
        out_specs=out_specs,
        out_shape=out_shape,
        compiler_params=_cparams(("arbitrary", "arbitrary"), V7X_VMEM_LIMIT),
        name="inproj",
    )(x_lat, x_ctx, mod3, g.reshape(1, d), w_bf, cos_t, sin_t, lb_f.reshape(1, BRANCH_W), lb_b.reshape(1, BRANCH_W), tri_f, tri_b)
    return {name: o for (name, _, _), o in zip(_IN_OUTS, outs)}


def _hgrn_direction(q_ref, b_ref, k_ref, v_ref, o_ref, st_ref, reverse, rows, cross_group_only):
    c = A_CHUNK
    row = lax.broadcasted_iota(jnp.int32, (c, c), 0)
    col = lax.broadcasted_iota(jnp.int32, (c, c), 1)
    allowed = (col >= row) if reverse else (col <= row)
    if cross_group_only:
        allowed = allowed & ((row // A_SUB) != (col // A_SUB))

    b = b_ref[0, rows, :]
    tot_row = 0 if reverse else c - 1
    tot = b[tot_row:tot_row + 1, :]

    q = q_ref[0, rows, :].astype(F32)
    kin = k_ref[0, rows, :].astype(F32)
    v = v_ref[0, rows, :]
    zero_blk = jnp.zeros((A_SUB, HEAD_W), F32)
    heads = []
    for h in range(A_HEADS):
        hs = slice(h * HEAD_W, (h + 1) * HEAD_W)
        bh, qh, kh = b[:, hs], q[:, hs], kin[:, hs]
        q_state = qh * jnp.exp(bh)
        k_state = kh * jnp.exp(tot[:, hs] - bh)
        q_rows, k_cols = [], []
        for i in range(c // A_SUB):
            r0 = i * A_SUB
            ref = r0 + A_SUB - 1 if reverse else r0
            r = bh[ref:ref + 1, :]
            qt = qh[r0:r0 + A_SUB] * jnp.exp(bh[r0:r0 + A_SUB] - r)
            q_rows.append(jnp.concatenate([qt if j == i else zero_blk for j in range(c // A_SUB)], axis=1))
            live = slice(r0, c) if reverse else slice(0, r0 + A_SUB)
            kt = kh[live] * jnp.exp(jnp.minimum(r - bh[live], EXP_CLAMP))
            dead = jnp.zeros((c - kt.shape[0], HEAD_W), F32)
            k_cols.append(jnp.concatenate([dead, kt] if reverse else [kt, dead], axis=0) if dead.shape[0] else kt)
        heads.append(dict(
            hs=hs, h=h, rows=rows, allowed=allowed, vh=v[:, hs], decay=jnp.exp(tot[:, hs]), o_ref=o_ref, st_ref=st_ref,
            q_wide=jnp.concatenate(q_rows, axis=0).astype(BF16),
            k_wide=jnp.concatenate(k_cols, axis=1).astype(BF16),
            q_state=q_state.astype(BF16), k_state=k_state.astype(BF16)))
    return heads


def _hgrn_exact_groups(q_ref, b_ref, k_ref, v_ref, o_ref, reverse):
    sub = lax.broadcasted_iota(jnp.int32, (A_SUB, HEAD_W), 0)

    def chunk(c, carry):
        for g in range(A_CHUNK // A_SUB):
            rows = pl.ds(pl.multiple_of(c * A_CHUNK + g * A_SUB, A_SUB), A_SUB)
            for h in range(A_HEADS):
                hs = slice(h * HEAD_W, (h + 1) * HEAD_W)
                bg, qg = b_ref[0, rows, hs], q_ref[0, rows, hs].astype(F32)
                kg, vg = k_ref[0, rows, hs].astype(F32), v_ref[0, rows, hs].astype(F32)
                acc = jnp.zeros((A_SUB, HEAD_W), F32)
                for t in range(A_SUB):
                    live = (sub >= t) if reverse else (sub <= t)
                    w = jnp.exp(jnp.minimum(bg[t:t + 1] - bg, 0.0))
                    a = jnp.sum(jnp.where(live, qg[t:t + 1] * kg * w, 0.0), axis=-1, keepdims=True)
                    acc = jnp.where(sub == t, jnp.sum(a * vg, axis=0, keepdims=True), acc)
                o_ref[0, rows, hs] = (o_ref[0, rows, hs].astype(F32) + acc).astype(o_ref.dtype)
        return carry

    lax.fori_loop(0, q_ref.shape[1] // A_CHUNK, chunk, 0)


def _hgrn_kernel(qf_ref, bf_ref, kf_ref, vf_ref, qb_ref, bb_ref, kb_ref, vb_ref, of_ref, ob_ref, sf_ref, sb_ref):
    @pl.when(pl.program_id(1) == 0)
    def _():
        sf_ref[...] = jnp.zeros_like(sf_ref)
        sb_ref[...] = jnp.zeros_like(sb_ref)

    drops = None
    for b_ref in (bf_ref, bb_ref):
        for r0 in range(0, qf_ref.shape[1], A_SUB):
            d = jnp.abs(b_ref[0, r0:r0 + 1, :] - b_ref[0, r0 + A_SUB - 1:r0 + A_SUB, :])
            drops = d if drops is None else jnp.maximum(drops, d)
    within_range = jnp.max(drops) <= EXP_CLAMP

    @pl.when(within_range)
    def _():
        _hgrn_block(qf_ref, bf_ref, kf_ref, vf_ref, qb_ref, bb_ref, kb_ref, vb_ref, of_ref, ob_ref, sf_ref, sb_ref,
                    cross_group_only=False)

    @pl.when(jnp.logical_not(within_range))
    def _():
        _hgrn_block(qf_ref, bf_ref, kf_ref, vf_ref, qb_ref, bb_ref, kb_ref, vb_ref, of_ref, ob_ref, sf_ref, sb_ref,
                    cross_group_only=True)
        _hgrn_exact_groups(qf_ref, bf_ref, kf_ref, vf_ref, of_ref, reverse=False)
        _hgrn_exact_groups(qb_ref, bb_ref, kb_ref, vb_ref, ob_ref, reverse=True)


def _hgrn_block(qf_ref, bf_ref, kf_ref, vf_ref, qb_ref, bb_ref, kb_ref, vb_ref, of_ref, ob_ref, sf_ref, sb_ref,
                cross_group_only):
    n_sub = qf_ref.shape[1] // A_CHUNK
    seq = []
    for i in range(n_sub):
        up = slice(i * A_CHUNK, (i + 1) * A_CHUNK)
        down = slice((n_sub - 1 - i) * A_CHUNK, (n_sub - i) * A_CHUNK)
        seq.append(_hgrn_direction(qf_ref, bf_ref, kf_ref, vf_ref, of_ref, sf_ref, False, up, cross_group_only)
                   + _hgrn_direction(qb_ref, bb_ref, kb_ref, vb_ref, ob_ref, sb_ref, True, down, cross_group_only))
    for chains in seq:
        for ch in chains:
            ch["att"] = _dot_nt(ch["q_wide"], ch["k_wide"])
    state = [ch["st_ref"][ch["h"]] for ch in seq[0]]
    for chains in seq:
        for ch, st in zip(chains, state):
            ch["inter"] = _dot_nt(ch["q_state"], st.astype(BF16))
            ch["upd"] = _dot_tn(ch["vh"], ch["k_state"])
        for n, ch in enumerate(chains):
            att = jnp.where(ch["allowed"], ch["att"], 0.0).astype(BF16)
            o = _dot(att, ch["vh"]) + ch["inter"]
            ch["o_ref"][0, ch["rows"], ch["hs"]] = o.astype(ch["o_ref"].dtype)
            state[n] = state[n] * ch["decay"] + ch["upd"]
    for ch, st in zip(seq[0], state):
        ch["st_ref"][ch["h"]] = st


def _hgrn(qa, bf, kf, bb, kb, ia, n_lat):
    bsz, t, w = qa.shape
    c = HGRN_ROWS
    n_chunks = t // c
    lat_chunks = n_lat // c

    def fwd(b, s):
        return (b, (s + lat_chunks) % n_chunks, 0)

    def bwd(b, s):
        return (b, n_chunks - 1 - s, 0)

    blk = (1, c, w)
    return pl.pallas_call(
        _hgrn_kernel,
        grid=(bsz, n_chunks),
        in_specs=[pl.BlockSpec(blk, fwd)] * 4 + [pl.BlockSpec(blk, bwd)] * 4,
        out_specs=[pl.BlockSpec(blk, fwd), pl.BlockSpec(blk, bwd)],
        out_shape=[jax.ShapeDtypeStruct((bsz, t, w), BF16)] * 2,
        scratch_shapes=[pltpu.VMEM((A_HEADS, HEAD_W, HEAD_W), F32)] * 2,
        compiler_params=_cparams(("arbitrary", "arbitrary")),
        name="hgrn2",
    )(qa, bf, kf, ia, qa, bb, kb, ia)


def _attn_kernel(q_ref, k_ref, v_ref, lam_ref, sub_ref, *rest, key_blocks, lam_init, n_cast):
    cast_in, (o_ref, *cast_out), (v1_ref, m_ref, acc_ref) = rest[:n_cast], rest[n_cast:2 * n_cast + 1], rest[-3:]
    for src, dst in zip(cast_in, cast_out):
        dst[...] = src[...].astype(BF16)
    tq = q_ref.shape[1]

    @pl.when(pl.program_id(2) == 0)
    def _():
        v1_ref[:, :HEAD_W] = v_ref[0]
        v1_ref[:, HEAD_W:] = jnp.ones((v1_ref.shape[0], HEAD_W), BF16)

    lane = lax.broadcasted_iota(jnp.int32, (tq, HEAD_W), 1)
    qf = q_ref[0].astype(F32) * (C_DH ** -0.5 * LOG2_E)
    q2 = jnp.concatenate([jnp.where(lane < C_DH, qf, 0.0), jnp.where(lane >= C_DH, qf, 0.0)], axis=0).astype(BF16)

    m_ref[...] = jnp.full_like(m_ref, NEG_BIG)
    acc_ref[...] = jnp.zeros_like(acc_ref)

    def scores(start, size):
        return _dot_nt(q2, k_ref[0, pl.ds(start, size), :])

    def absorb(s, start, size):
        m_old = m_ref[...]
        m_new = jnp.maximum(m_old, jnp.max(s, axis=-1, keepdims=True))
        alpha = jnp.exp2(m_old - m_new)
        p = jnp.exp2(s - jnp.concatenate([m_new] * (size // HEAD_W), axis=1))
        acc_ref[...] = (jnp.concatenate([alpha, alpha], axis=1) * acc_ref[...]
                        + _dot(p.astype(BF16), v1_ref[pl.ds(start, size), :]))
        m_ref[...] = m_new

    cur = scores(*key_blocks[0])
    for n, blk in enumerate(key_blocks):
        nxt = scores(*key_blocks[n + 1]) if n + 1 < len(key_blocks) else None
        absorb(cur, *blk)
        cur = nxt

    o12 = acc_ref[:, :HEAD_W] / acc_ref[:, HEAD_W:]
    o = o12[:tq] - lam_ref[...] * o12[tq:]
    o_ref[0] = (_rms(o) * sub_ref[...] * (1.0 - lam_init)).astype(o_ref.dtype)


def _attention(qc, kc, vc, lam, subln, lam_init, tq, q_block0, n_q, kv_block, key_blocks, to_bf16=()):
    bsz, _, w = qc.shape
    kv_rows, kv_idx = kv_block
    n_steps = bsz * C_HEADS * n_q
    cast_specs = []
    for a in to_bf16:
        rows = a.shape[0] // n_steps
        assert a.shape[0] == rows * n_steps and rows % BF16_ROWS == 0
        cast_specs.append(pl.BlockSpec((rows, a.shape[1]), lambda b, h, i: ((b * C_HEADS + h) * n_q + i, 0)))
    outs = pl.pallas_call(
        functools.partial(_attn_kernel, key_blocks=key_blocks, lam_init=lam_init, n_cast=len(to_bf16)),
        grid=(bsz, C_HEADS, n_q),
        in_specs=[pl.BlockSpec((1, tq, HEAD_W), lambda b, h, i: (b, q_block0 + i, h)),
                  pl.BlockSpec((1, kv_rows, HEAD_W), lambda b, h, i: (b, kv_idx, h)),
                  pl.BlockSpec((1, kv_rows, HEAD_W), lambda b, h, i: (b, kv_idx, h)),
                  pl.BlockSpec((1, HEAD_W), lambda b, h, i: (0, 0)),
                  pl.BlockSpec((1, HEAD_W), lambda b, h, i: (0, h))] + cast_specs,
        out_specs=[pl.BlockSpec((1, tq, HEAD_W), lambda b, h, i: (b, i, h))] + cast_specs,
        out_shape=[jax.ShapeDtypeStruct((bsz, n_q * tq, w), BF16)]
        + [jax.ShapeDtypeStruct(a.shape, BF16) for a in to_bf16],
        scratch_shapes=[pltpu.VMEM((kv_rows, 2 * HEAD_W), BF16), pltpu.VMEM((2 * tq, HEAD_W), F32),
                        pltpu.VMEM((2 * tq, 2 * HEAD_W), F32)],
        compiler_params=_cparams(("arbitrary", "arbitrary", "arbitrary")),
        name="diff_attention",
    )(qc, kc, vc, jnp.full((1, HEAD_W), lam, F32), subln.reshape(1, w), *to_bf16)
    return outs[0], outs[1:]


def _merge_kernel(xl_ref, xc_ref, mod_ref, of_ref, ob_ref, og_ref, gn_ref, uv_ref, vn_ref, ws_ref, bs_ref, ycl_ref,
                  ycc_ref, gt_ref, wb_ref, wo_ref, g2_ref, wr_ref, x_out_ref, h_out_ref, *maybe_logits_ref, d, n_lat_blocks):
    tm = xl_ref.shape[1]
    is_lat = pl.program_id(1) < n_lat_blocks
    mod = mod_ref[0]
    groups = [slice(r0, r0 + B_CHUNK) for r0 in range(0, tm, B_CHUNK)]
    st = [dict() for _ in groups]

    for s, rows in zip(st, groups):
        o = of_ref[0, rows, :].astype(F32) + ob_ref[0, rows, :].astype(F32)
        o = jnp.concatenate([_rms(o[:, h * HEAD_W:(h + 1) * HEAD_W]) for h in range(A_HEADS)], axis=1)
        s["ya"] = (o * gn_ref[...] * og_ref[0, rows, :].astype(F32)).astype(BF16)
        uv = uv_ref[0, rows, :].astype(F32)
        s["u"] = uv[:, :BRANCH_W]
        s["vv"] = (_rms(uv[:, BRANCH_W:]) * vn_ref[...]).astype(BF16)
    for s in st:
        s["mixed"] = jnp.concatenate(
            [_dot(ws_ref[g], s["vv"][:, g * HEAD_W:(g + 1) * HEAD_W]) for g in range(B_GROUPS)], axis=1)
    for s, rows in zip(st, groups):
        yb = (s["u"] * (s["mixed"] + bs_ref[...])).astype(BF16)
        yc = jnp.where(is_lat, ycl_ref[0, rows, :], ycc_ref[0, rows, :])
        s["proj"] = [_dot(y, wb_ref[i]) for i, y in enumerate((s["ya"], yb, yc))]
    for s, rows in zip(st, groups):
        merged = None
        for i in range(N_BRANCH):
            term = gt_ref[0, rows, i * d:(i + 1) * d].astype(F32) * s["proj"][i]
            merged = term if merged is None else merged + term
        s["mix"] = _dot(merged.astype(BF16), wo_ref[...])
    for s, rows in zip(st, groups):
        x_old = jnp.where(is_lat, xl_ref[0, rows, :], xc_ref[0, rows, :])
        x_new = x_old + mod[:, 2 * d:3 * d] * s["mix"]
        x_out_ref[0, rows, :] = x_new
        h = _rms(x_new) * g2_ref[...]
        h = h * (1.0 + mod[:, 4 * d:5 * d]) + mod[:, 3 * d:4 * d]
        h_hi = h.astype(BF16)
        h_out_ref[0, rows, :] = h_hi
        if maybe_logits_ref:
            h_lo = (h - h_hi.astype(F32)).astype(BF16)
            lg = (_dot(h_hi, wr_ref[0]) + _dot(h_lo, wr_ref[0])) + _dot(h_hi, wr_ref[1])
            maybe_logits_ref[0][0, rows, :] = lg
            maybe_logits_ref[1][:, rows] = lg.T


def _merge(x_lat, x_ctx, ctx_block0, mod3, proj, of, ob, yc_lat, yc_ctx, gnorm, vnorm, ws_bf, bs_full, wb_bf, wo_bf, g2, wr_split,
           n_lat_blocks, n_blocks, with_logits):
    bsz, _, d = x_lat.shape
    tm = ROW_TILE
    ctx_row = bsz

    def rows(w):
        return pl.BlockSpec((1, tm, w), lambda b, j: (b, j, 0))

    def const2(shape):
        return pl.BlockSpec(shape, lambda b, j: (0, 0))

    def mod_map(b, j):
        return (jnp.where(j < n_lat_blocks, b, ctx_row), 0, 0)

    n_rows = n_blocks * tm
    out_shape = [jax.ShapeDtypeStruct((bsz, n_rows, d), F32), jax.ShapeDtypeStruct((bsz, n_rows, d), BF16)]
    out_specs = [rows(d), rows(d)]
    if with_logits:
        out_shape += [jax.ShapeDtypeStruct((bsz, n_rows, HEAD_W), F32),
                      jax.ShapeDtypeStruct((HEAD_W, bsz * n_rows), F32)]
        out_specs += [rows(HEAD_W), pl.BlockSpec((HEAD_W, tm), lambda b, j: (0, b * n_blocks + j))]
    lat_spec, ctx_spec = _lat_ctx_specs(tm, d, n_lat_blocks, ctx_block0)
    return pl.pallas_call(
        functools.partial(_merge_kernel, d=d, n_lat_blocks=n_lat_blocks),
        grid=(bsz, n_blocks),
        in_specs=[lat_spec, ctx_spec, pl.BlockSpec((1, 1, mod3.shape[2]), mod_map),
                  rows(BRANCH_W), rows(BRANCH_W), rows(BRANCH_W), const2((1, BRANCH_W)),
                  rows(2 * BRANCH_W), const2((1, BRANCH_W)),
                  _resident((B_GROUPS, B_CHUNK, B_CHUNK), lambda b, j: (0, 0, 0)), const2((B_CHUNK, BRANCH_W)),
                  *_lat_ctx_specs(tm, BRANCH_W, n_lat_blocks, 0), rows(N_BRANCH * d),
                  _resident((N_BRANCH, BRANCH_W, d), lambda b, j: (0, 0, 0)), _resident((d, d), lambda b, j: (0, 0)),
                  const2((1, d)), _resident((2, d, HEAD_W), lambda b, j: (0, 0, 0))],
        out_specs=out_specs,
        out_shape=out_shape,
        compiler_params=_cparams(("arbitrary", "arbitrary"), V7X_VMEM_LIMIT),
        name="merge",
    )(x_lat, x_ctx, mod3, of, ob, proj["og"], gnorm.reshape(1, BRANCH_W), proj["uv"], vnorm.reshape(1, BRANCH_W),
      ws_bf, bs_full, yc_lat, yc_ctx, proj["gt"], wb_bf, wo_bf, g2.reshape(1, d), wr_split)


def _ffn_kernel(x_ref, h_ref, mod_ref, wg_ref, wu_ref, wd_ref, o_ref, *, d):
    h = h_ref[0]
    g = _dot(h, wg_ref[...])
    a = (g * _sigmoid(g) * _dot(h, wu_ref[...])).astype(BF16)
    y = _dot(a, wd_ref[...])
    o_ref[0] = x_ref[0] + mod_ref[0][:, 5 * d:6 * d] * y


def _ffn(xc, h2, mod3, wg_bf, wu_bf, wd_bf, n_lat_blocks):
    bsz, t, d = xc.shape
    tm = ROW_TILE
    dff = wg_bf.shape[1]
    ctx_row = bsz

    def mod_map(b, j):
        return (jnp.where(j < n_lat_blocks, b, ctx_row), 0, 0)

    rows = pl.BlockSpec((1, tm, d), lambda b, j: (b, j, 0))
    return pl.pallas_call(
        functools.partial(_ffn_kernel, d=d),
        grid=(bsz, t // tm),
        in_specs=[rows, rows, pl.BlockSpec((1, 1, mod3.shape[2]), mod_map),
                  _resident((d, dff), lambda b, j: (0, 0)), _resident((d, dff), lambda b, j: (0, 0)),
                  _resident((dff, d), lambda b, j: (0, 0))],
        out_specs=rows,
        out_shape=jax.ShapeDtypeStruct((bsz, t, d), F32),
        compiler_params=_cparams(("arbitrary", "arbitrary"), V7X_VMEM_LIMIT),
        name="dense_ffn",
    )(xc, h2, mod3, wg_bf, wu_bf, wd_bf)


def _moe_kernel(tile_expert_ref, n_tiles_ref, x_ref, wg_ref, wu_ref, wd_ref, *rest, tile0):
    del tile_expert_ref
    o_ref = rest[-1]
    in_use = pl.program_id(0) + tile0 < n_tiles_ref[0]

    @pl.when(in_use)
    def _():
        x = x_ref[...]
        dff = wg_ref.shape[2]
        acc = None
        for c0 in range(0, dff, FF_CHUNK):
            g = _dot(x, wg_ref[0, :, c0:c0 + FF_CHUNK])
            a = (g * _sigmoid(g) * _dot(x, wu_ref[0, :, c0:c0 + FF_CHUNK])).astype(BF16)
            y = _dot(a, wd_ref[0, c0:c0 + FF_CHUNK, :])
            acc = y if acc is None else acc + y
        o_ref[...] = acc.astype(o_ref.dtype)

    @pl.when(jnp.logical_not(in_use))
    def _():
        o_ref[...] = jnp.zeros_like(o_ref)


def _moe_experts(x_part, tile0, p, tile_expert, n_tiles, wg_bf, wu_bf, wd_bf, y_prev):
    d = x_part.shape[1]
    tm = MOE_TILE
    dff = wg_bf.shape[2]
    in_specs = [pl.BlockSpec((tm, d), lambda t, te, nt: (t, 0)),
                pl.BlockSpec((1, d, dff), lambda t, te, nt: (te[tile0 + t], 0, 0)),
                pl.BlockSpec((1, d, dff), lambda t, te, nt: (te[tile0 + t], 0, 0)),
                pl.BlockSpec((1, dff, d), lambda t, te, nt: (te[tile0 + t], 0, 0))]
    operands = [tile_expert, n_tiles, x_part, wg_bf, wu_bf, wd_bf]
    aliases = {}
    if y_prev is not None:
        in_specs.append(pl.BlockSpec(memory_space=pl.ANY))
        aliases = {len(operands): 0}
        operands.append(y_prev)
    grid_spec = pltpu.PrefetchScalarGridSpec(
        num_scalar_prefetch=2,
        grid=(x_part.shape[0] // tm,),
        in_specs=in_specs,
        out_specs=pl.BlockSpec((tm, d), lambda t, te, nt: (tile0 + t, 0)),
    )
    return pl.pallas_call(
        functools.partial(_moe_kernel, tile0=tile0),
        grid_spec=grid_spec,
        out_shape=jax.ShapeDtypeStruct((p, d), BF16),
        input_output_aliases=aliases,
        compiler_params=_cparams(("arbitrary",), V7X_VMEM_LIMIT),
        name="moe_experts",
    )(*operands)


def _route(logits_t, tm):
    n = logits_t.shape[1]
    experts = jnp.arange(N_EXPERTS, dtype=jnp.int32)[:, None]
    e1 = jnp.argmax(logits_t, axis=0).astype(jnp.int32)
    e2 = jnp.argmax(jnp.where(experts == e1[None, :], -jnp.inf, logits_t), axis=0).astype(jnp.int32)
    oh1 = (experts == e1[None, :]).astype(jnp.int32)
    oh2 = (experts == e2[None, :]).astype(jnp.int32)
    c1 = jnp.cumsum(oh1, axis=1)
    c2 = jnp.cumsum(oh2, axis=1) + c1[:, -1:]
    counts = c2[:, -1]
    padded = ((counts + tm - 1) // tm) * tm
    ends = jnp.cumsum(padded)
    starts = ends - padded
    dest1 = jnp.sum(oh1 * (starts[:, None] + c1 - 1), axis=0)
    dest2 = jnp.sum(oh2 * (starts[:, None] + c2 - 1), axis=0)
    m = TOP_K * n
    p = m + N_EXPERTS * tm
    tile_start = jnp.arange(p // tm, dtype=jnp.int32) * tm
    tile_expert = jnp.minimum(jnp.sum((tile_start[:, None] >= ends[None, :]).astype(jnp.int32), axis=1),
                              N_EXPERTS - 1).astype(jnp.int32)
    n_tiles = (ends[-1] // tm).astype(jnp.int32).reshape(1)
    token = jnp.arange(n, dtype=jnp.int32)
    by_expert = jnp.sort(jnp.concatenate([e1 * m + token, e2 * m + n + token])) % m
    first = jnp.cumsum(counts) - counts
    row_expert = jnp.repeat(tile_expert, tm)
    pos = jnp.arange(p, dtype=jnp.int32)
    entry = jnp.take(by_expert, (first[row_expert] + pos - starts[row_expert]) % m)
    src_token = jnp.where(entry < n, entry, entry - n)
    return dest1, dest2, src_token, tile_expert, n_tiles


def _final_kernel(x_ref, y0_ref, y1_ref, lg_ref, mod_ref, g_ref, o_ref, *, d):
    lane = lax.broadcasted_iota(jnp.int32, lg_ref.shape[1:], 1)
    lg = jnp.where(lane < N_EXPERTS, lg_ref[0], NEG_BIG)
    m1 = jnp.max(lg, axis=-1, keepdims=True)
    first = jnp.min(jnp.where(lg == m1, lane, HEAD_W), axis=-1, keepdims=True)
    m2 = jnp.max(jnp.where(lane == first, NEG_BIG, lg), axis=-1, keepdims=True)
    e = jnp.exp(m2 - m1)
    w1 = 1.0 / (1.0 + e)
    y = w1 * y0_ref[0].astype(F32) + (e * w1) * y1_ref[0].astype(F32)
    x = x_ref[0] + mod_ref[0][:, 5 * d:6 * d] * y
    o_ref[0] = _rms(x) * g_ref[...]


def _final(x_io, y0, y1, logits, mod3, g_final, b0):
    bsz, s, d = x_io.shape
    tm = ROW_TILE
    here = pl.BlockSpec((1, tm, d), lambda b, j: (b0 + b, j, 0))
    part = pl.BlockSpec((1, tm, d), lambda b, j: (b, j, 0))
    return pl.pallas_call(
        functools.partial(_final_kernel, d=d),
        grid=(y0.shape[0], s // tm),
        in_specs=[here, part, part, pl.BlockSpec((1, tm, HEAD_W), lambda b, j: (b0 + b, j, 0)),
                  pl.BlockSpec((1, 1, mod3.shape[2]), lambda b, j: (b0 + b, 0, 0)),
                  pl.BlockSpec((1, d), lambda b, j: (0, 0))],
        out_specs=here,
        out_shape=jax.ShapeDtypeStruct((bsz, s, d), F32),
        input_output_aliases={0: 0},
        compiler_params=_cparams(("arbitrary", "arbitrary")),
        name="moe_combine_final_norm",
    )(x_io, y0, y1, logits, mod3, g_final.reshape(1, d))


def _rope_tables(n_lat, n_ctx):
    pairs = ROPE_PAIRS
    tpos = jnp.arange(n_lat)
    pos = jnp.stack([(tpos // GRID_W).astype(F32), (tpos % GRID_W).astype(F32)], axis=1)
    freqs = ROPE_BASE ** (-jnp.arange(pairs, dtype=F32) / pairs)
    ang = pos[:, :, None] * freqs[None, None, :]
    cos = jnp.repeat(jnp.cos(ang)[:, :, None, :], 2, axis=2).reshape(n_lat, C_DH)
    sin = jnp.sin(ang)[:, :, None, :] * jnp.array([-1.0, 1.0], F32)[None, None, :, None]
    sin = sin.reshape(n_lat, C_DH)
    cos = jnp.concatenate([cos, jnp.ones((n_ctx, C_DH), F32)], axis=0)
    sin = jnp.concatenate([sin, jnp.zeros((n_ctx, C_DH), F32)], axis=0)
    reps = BRANCH_W // C_DH
    return jnp.tile(cos, (1, reps)), jnp.tile(sin, (1, reps))


def _lower_bound(p, layer):
    cs = jnp.cumsum(jax.nn.softmax(p.astype(F32), axis=0), axis=0)
    return cs[layer] - cs[0]


def kernel(x, c, ctx, c_ctx, w_ada, b_ada, g_norm1, g_norm2, w_in, hgrn_lb, hgrn_gnorm, mlp_vnorm, mlp_ws, mlp_bs,
           diff_lambda, diff_subln, w_branch, w_out, ffn_wg, ffn_wu, ffn_wd, moe_router, moe_wg, moe_wu, moe_wd,
           g_final):
    bsz, n_lat, d = x.shape
    n_ctx = ctx.shape[1]
    depth = w_ada.shape[0]
    t = n_lat + n_ctx
    tm = ROW_TILE
    assert depth == 2 and d == D_MODEL and bsz < MOD_ROWS and n_lat % GRID_W == 0
    assert n_lat % ATT_TQ == 0 and n_lat % ATT_TK == 0 and n_lat % n_ctx == 0 and n_ctx % tm == 0
    n_lat_blocks = n_lat // tm

    cvec = jnp.zeros((MOD_ROWS, d), F32).at[:bsz].set(c).at[bsz].set(c_ctx)
    mods = _adaln(cvec, w_ada, b_ada)
    cos_t, sin_t = _rope_tables(n_lat, n_ctx)
    x_lat, x_ctx, ctx_block0 = x, ctx, 0
    w_in_bf = w_in.astype(BF16)

    out = None
    for layer in range(depth):
        last = layer == depth - 1
        mod3 = mods[layer].reshape(MOD_ROWS, 1, 6 * d)
        lb_f = _lower_bound(hgrn_lb[0], layer)
        lb_b = _lower_bound(hgrn_lb[1], layer)
        proj = _inproj(x_lat, x_ctx, ctx_block0, t, mod3, g_norm1[layer], w_in_bf, layer, cos_t, sin_t,
                       lb_f, lb_b, n_lat_blocks)
        of, ob = _hgrn(proj["qa"], proj["bf"], proj["kf"], proj["bb"], proj["kb"], proj["ia"], n_lat)

        lam_init = 0.8 - 0.6 * math.exp(-0.3 * layer)
        lam_p = diff_lambda[layer]
        lam = jnp.exp(jnp.sum(lam_p[0] * lam_p[1])) - jnp.exp(jnp.sum(lam_p[2] * lam_p[3])) + lam_init
        lat_keys = tuple((i * ATT_TK, ATT_TK) for i in range(n_lat // ATT_TK)) + ((n_lat, n_ctx),)
        qkv = (proj["qc"], proj["kc"], proj["vc"], lam.astype(F32), diff_subln[layer], lam_init)
        j = layer // 2
        is_moe = layer % 2 == 1
        expert_w = ()
        if is_moe:
            n_e, _, dff = moe_wg[j].shape
            expert_w = (moe_wg[j].reshape(n_e * d, dff), moe_wu[j].reshape(n_e * d, dff),
                        moe_wd[j].reshape(n_e * dff, d))
        yc_lat, expert_w_bf = _attention(*qkv, ATT_TQ, 0, n_lat // ATT_TQ, (t, 0), lat_keys, expert_w)
        yc_ctx = yc_lat if last else _attention(*qkv, n_ctx, n_lat // n_ctx, 1, (n_ctx, n_lat // n_ctx),
                                                ((0, n_ctx),))[0]

        bs_full = jnp.repeat(mlp_bs[layer].T, HEAD_W, axis=1)
        wr_split = jnp.zeros((2, d, HEAD_W), BF16)
        if is_moe:
            wr_hi = moe_router[j].astype(BF16)
            wr_lo = (moe_router[j] - wr_hi.astype(F32)).astype(BF16)
            wr_split = wr_split.at[:, :, :N_EXPERTS].set(jnp.stack([wr_hi, wr_lo]))
        n_blocks = n_lat_blocks if last else t // tm
        merged = _merge(x_lat, x_ctx, ctx_block0, mod3, proj, of, ob, yc_lat, yc_ctx, hgrn_gnorm[layer], mlp_vnorm[layer],
                        mlp_ws[layer].astype(BF16), bs_full, w_branch[layer].astype(BF16), w_out[layer].astype(BF16),
                        g_norm2[layer], wr_split, n_lat_blocks, n_blocks, is_moe)
        x_new, h2 = merged[0], merged[1]

        if not is_moe:
            assert not last
            xc = _ffn(x_new, h2, mod3, ffn_wg[j].astype(BF16), ffn_wu[j].astype(BF16), ffn_wd[j].astype(BF16),
                      n_lat_blocks)
            x_lat, x_ctx, ctx_block0 = xc, xc, n_lat_blocks
        else:
            assert last
            n = bsz * n_lat
            logits, logits_t = merged[2], merged[3]
            dest1, dest2, src_token, tile_expert, n_tiles = _route(logits_t[:N_EXPERTS], MOE_TILE)
            p = src_token.shape[0]
            part_rows = p // MOE_PARTS
            assert part_rows % MOE_TILE == 0 and part_rows * MOE_PARTS == p
            expert_w = (expert_w_bf[0].reshape(n_e, d, dff), expert_w_bf[1].reshape(n_e, d, dff),
                        expert_w_bf[2].reshape(n_e, dff, d))
            y = jnp.zeros((p, d), BF16)
            for i in range(MOE_PARTS):
                x_part = jnp.take(h2.reshape(n, d), src_token[i * part_rows:(i + 1) * part_rows], axis=0, mode="clip")
                y = _moe_experts(x_part, i * (part_rows // MOE_TILE), p, tile_expert, n_tiles, *expert_w, y)
            assert bsz % FINAL_PARTS == 0
            part_b = bsz // FINAL_PARTS
            out = x_new
            for i in range(FINAL_PARTS):
                tokens = slice(i * part_b * n_lat, (i + 1) * part_b * n_lat)
                y0 = jnp.take(y, dest1[tokens], axis=0, mode="clip").reshape(part_b, n_lat, d)
                y1 = jnp.take(y, dest2[tokens], axis=0, mode="clip").reshape(part_b, n_lat, d)
                out = _final(out, y0, y1, logits, mod3, g_final, i * part_b)
    return out
```
